```python
import jax, jax.numpy as jnp
from jax import lax
import numpy as np

D_MODEL = 2048
BATCH = 16
SEQ = 2048
DEPTH = 1

HEAD_DIM = 128
NSA_HEADS = 8
NSA_KV_GROUPS = 2
NSA_REP = NSA_HEADS // NSA_KV_GROUPS
CMP_LEN = 32
CMP_STRIDE = 16
SLC_LEN = 64
SLC_TOPK = 16
WINDOW = 512
WIN_QBLOCK = 128
SLC_QCHUNK = 16
MLSTM_HEADS = 4
MLSTM_QK_DIM = 128
MLSTM_V_DIM = 256
MLSTM_CHUNK = 64
MLSTM_CONV = 4
MEM_LEN = 256
MEM_HEADS = 4
MEM_HEAD_DIM = 256
N_EXPERTS = 64
TOP_K = 6
EXPERT_DIM = 1408
SHARED_DIM = 1408
ROUTED_SCALE = 2.5
MOE_BLOCK = 256
ROPE_THETA = 10000.0
LN_EPS = 1e-5
NEG_INF = -1e30
DN_ALPHA = (2 * DEPTH) ** 0.25
DN_BETA = (8 * DEPTH) ** -0.25

NSA_Q_W = NSA_HEADS * HEAD_DIM
NSA_KV_W = NSA_KV_GROUPS * HEAD_DIM
ML_QK_W = MLSTM_HEADS * MLSTM_QK_DIM
ML_V_W = MLSTM_HEADS * MLSTM_V_DIM
MEM_W = MEM_HEADS * MEM_HEAD_DIM
BRANCH_W = NSA_Q_W
N_BRANCH = 3
IN_SIZES = (NSA_Q_W, NSA_KV_W, NSA_KV_W, NSA_KV_W, NSA_KV_W, NSA_KV_W, NSA_KV_W, NSA_HEADS * 3,
            ML_QK_W, ML_QK_W, ML_V_W, ML_V_W, MLSTM_HEADS, MLSTM_HEADS,
            MEM_W, N_BRANCH * D_MODEL)
IN_COLS = sum(IN_SIZES)

kernel_name = 'nsa_mlstm_memory_moe_hybrid'


def layer_norm(x, g, b):
    xf = x.astype(jnp.float32)
    mu = xf.mean(-1, keepdims=True)
    var = jnp.mean(jnp.square(xf - mu), -1, keepdims=True)
    return ((xf - mu) * lax.rsqrt(var + LN_EPS) * g + b).astype(x.dtype)


def rope(x, pos):
    half = x.shape[-1] // 2
    inv = ROPE_THETA ** (-jnp.arange(half, dtype=jnp.float32) / half)
    ang = pos.astype(jnp.float32)[:, None] * inv[None, :]
    shape = (1, pos.shape[0]) + (1,) * (x.ndim - 3) + (half,)
    cos, sin = jnp.cos(ang).reshape(shape), jnp.sin(ang).reshape(shape)
    x1, x2 = x[..., :half], x[..., half:]
    return jnp.concatenate([x1 * cos - x2 * sin, x2 * cos + x1 * sin], -1).astype(x.dtype)


def causal_conv(x, w, b):
    c = x.shape[-1]
    y = lax.conv_general_dilated(x, w[:, None, :].astype(x.dtype), window_strides=(1,),
                                 padding=[(w.shape[0] - 1, 0)],
                                 dimension_numbers=('NWC', 'WIO', 'NWC'), feature_group_count=c)
    return y + b


def compress_blocks(t, pos_emb, w1, w2):
    B, S, G, HD = t.shape
    c = t.reshape(B, S // CMP_STRIDE, CMP_STRIDE, G, HD)
    blocks = jnp.concatenate([c[:, :-1], c[:, 1:]], axis=2) + pos_emb[None, None, :, None, :]
    flat = jnp.moveaxis(blocks, 3, 2).reshape(B, blocks.shape[1], G, CMP_LEN * HD)
    return jax.nn.gelu(flat @ w1) @ w2


def cmp_slc_cover(n_cmp, n_slc):
    i = np.arange(n_cmp)[:, None]
    j = np.arange(n_slc)[None, :]
    c0 = i * CMP_STRIDE
    c1 = c0 + CMP_LEN - 1
    s0 = j * SLC_LEN
    s1 = s0 + SLC_LEN - 1
    return jnp.asarray(((c1 >= s0) & (c0 <= s1)).astype(np.float32))


def selected_attention(q, k, v, top_idx, sel_valid):
    B, S, G, R, HD = q.shape
    n_sel = top_idx.shape[-1]
    kb = k.reshape(B, S // SLC_LEN, SLC_LEN, G, HD).transpose(0, 3, 1, 2, 4)
    vb = v.reshape(B, S // SLC_LEN, SLC_LEN, G, HD).transpose(0, 3, 1, 2, 4)
    qc_len = SLC_QCHUNK
    nq = S // qc_len
    bi = jnp.arange(B)[:, None, None, None]
    gi = jnp.arange(G)[None, None, :, None]
    offs = jnp.arange(SLC_LEN)
    scale = HD ** -0.5

    def to_chunks(t):
        return jnp.moveaxis(t.reshape((B, nq, qc_len) + t.shape[2:]), 1, 0)

    def attend(args):
        qc, idx, valid, c = args
        ks = kb[bi, gi, idx]
        vs = vb[bi, gi, idx]
        s = jnp.einsum('btgrd,btgnld->btgrnl', qc, ks).astype(jnp.float32) * scale
        tpos = c * qc_len + jnp.arange(qc_len)
        kpos = idx[..., None] * SLC_LEN + offs
        m = (valid[..., None] & (kpos <= tpos[None, :, None, None, None]))[:, :, :, None]
        s = jnp.where(m, s, NEG_INF).reshape(B, qc_len, G, R, n_sel * SLC_LEN)
        p = jax.nn.softmax(s, axis=-1).reshape(B, qc_len, G, R, n_sel, SLC_LEN).astype(qc.dtype)
        return jnp.einsum('btgrnl,btgnld->btgrd', p, vs)

    out = lax.map(attend, (to_chunks(q), to_chunks(top_idx), to_chunks(sel_valid), jnp.arange(nq)))
    return jnp.moveaxis(out, 0, 1).reshape(B, S, G, R, HD)


def window_attention(q, k, v):
    B, S, G, R, HD = q.shape
    qb_len = WIN_QBLOCK
    nb = S // qb_len
    n_prev = WINDOW // qb_len
    span = qb_len * (n_prev + 1)

    def band(t):
        tp = jnp.pad(t, ((0, 0), (WINDOW, 0), (0, 0), (0, 0))).reshape(B, nb + n_prev, qb_len, G, HD)
        return jnp.concatenate([tp[:, i:i + nb] for i in range(n_prev + 1)], axis=2)

    kw, vw = band(k), band(v)
    qb = q.reshape(B, nb, qb_len, G, R, HD)
    s = jnp.einsum('bjtgrd,bjugd->bjtgru', qb, kw).astype(jnp.float32) * HD ** -0.5
    blk = jnp.arange(nb)[:, None]
    qpos = (blk * qb_len + jnp.arange(qb_len)[None, :])[:, :, None]
    kpos = (blk * qb_len - WINDOW + jnp.arange(span)[None, :])[:, None, :]
    m = ((kpos <= qpos) & (kpos > qpos - WINDOW) & (kpos >= 0))[None, :, :, None, None, :]
    p = jax.nn.softmax(jnp.where(m, s, NEG_INF), axis=-1).astype(q.dtype)
    return jnp.einsum('bjtgru,bjugd->bjtgrd', p, vw).reshape(B, S, G, R, HD)


def nsa_mixer(q, k_c, v_c, k_s, v_s, k_w, v_w, g_pre,
              cmp_pos_k, cmp_k_w1, cmp_k_w2, cmp_pos_v, cmp_v_w1, cmp_v_w2):
    B, S, _ = q.shape
    G, R, HD = NSA_KV_GROUPS, NSA_REP, HEAD_DIM
    dt = q.dtype
    pos = jnp.arange(S)
    q = rope(q.reshape(B, S, G, R, HD), pos)
    n_cmp = S // CMP_STRIDE - 1
    cmp_end = jnp.arange(n_cmp) * CMP_STRIDE + CMP_LEN - 1
    kc = rope(compress_blocks(k_c.reshape(B, S, G, HD), cmp_pos_k, cmp_k_w1, cmp_k_w2), cmp_end)
    vc = compress_blocks(v_c.reshape(B, S, G, HD), cmp_pos_v, cmp_v_w1, cmp_v_w2)
    s = jnp.einsum('bsgrd,bngd->bsgrn', q, kc).astype(jnp.float32) * HD ** -0.5
    cmask = (cmp_end[None, :] <= pos[:, None])[None, :, None, None, :]
    p_cmp = jax.nn.softmax(jnp.where(cmask, s, NEG_INF), axis=-1) * cmask
    o_cmp = jnp.einsum('bsgrn,bngd->bsgrd', p_cmp.astype(dt), vc)
    n_slc = S // SLC_LEN
    n_sel = min(SLC_TOPK, n_slc)
    p_slc = jnp.einsum('bsgn,nj->bsgj', p_cmp.sum(axis=3), cmp_slc_cover(n_cmp, n_slc))
    blk = jnp.arange(n_slc)[None, :]
    t_blk = (pos // SLC_LEN)[:, None]
    forced = ((blk == 0) | (blk == t_blk) | (blk == t_blk - 1))[None, :, None, :]
    future = (blk > t_blk)[None, :, None, :]
    score = jnp.where(forced, jnp.inf, jnp.where(future, -jnp.inf, p_slc))
    top_val, top_idx = lax.top_k(score, n_sel)
    sel_valid = top_val > -jnp.inf
    o_slc = selected_attention(q, rope(k_s.reshape(B, S, G, HD), pos), v_s.reshape(B, S, G, HD),
                               top_idx, sel_valid)
    o_win = window_attention(q, rope(k_w.reshape(B, S, G, HD), pos), v_w.reshape(B, S, G, HD))
    gate = jax.nn.sigmoid(g_pre.astype(jnp.float32)).reshape(B, S, G, R, 3).astype(dt)
    o = gate[..., 0:1] * o_cmp + gate[..., 1:2] * o_slc + gate[..., 2:3] * o_win
    return o.reshape(B, S, NSA_Q_W)


def mlstm_mixer(q, k, v, o_pre, i_pre, f_pre, conv_w, conv_b, if_bias, norm_g):
    B, S, _ = q.shape
    NH, DK, DV, L = MLSTM_HEADS, MLSTM_QK_DIM, MLSTM_V_DIM, MLSTM_CHUNK
    dt = q.dtype
    qk = jax.nn.silu(causal_conv(jnp.concatenate([q, k], -1), conv_w, conv_b))
    q, k = jnp.split(qk, 2, axis=-1)
    q = q.reshape(B, S, NH, DK)
    k = k.reshape(B, S, NH, DK) * DK ** -0.5
    v = v.reshape(B, S, NH, DV)
    li = i_pre.astype(jnp.float32) + if_bias[0]
    lf = jax.nn.log_sigmoid(f_pre.astype(jnp.float32) + if_bias[1])
    nc = S // L

    def chunks(t):
        return jnp.moveaxis(t.reshape((B, nc, L) + t.shape[2:]), (1, 3), (0, 2))

    tri = jnp.tril(jnp.ones((L, L), dtype=bool))

    def step(carry, inp):
        C, n, m = carry
        qc, kc, vc, lic, lfc = inp
        b = jnp.cumsum(lfc, axis=-1)
        dmat = jnp.where(tri, b[..., :, None] - b[..., None, :] + lic[..., None, :], -jnp.inf)
        m_inter = b + m[..., None]
        m_t = jnp.maximum(m_inter, dmat.max(-1))
        wqk = jnp.exp(dmat - m_t[..., None]) * jnp.einsum('bhtd,bhsd->bhts', qc, kc)
        inter = jnp.exp(m_inter - m_t)
        num = jnp.einsum('bhts,bhsv->bhtv', wqk, vc) + inter[..., None] * jnp.einsum('bhtd,bhdv->bhtv', qc, C)
        den = wqk.sum(-1) + inter * jnp.einsum('bhtd,bhd->bht', qc, n)
        h = num / jnp.maximum(jnp.abs(den), jnp.exp(-m_t))[..., None]
        b_last = b[..., -1]
        g = b_last[..., None] - b + lic
        m_new = jnp.maximum(b_last + m, g.max(-1))
        decay = jnp.exp(b_last + m - m_new)
        w_s = jnp.exp(g - m_new[..., None])
        C = decay[..., None, None] * C + jnp.einsum('bhs,bhsd,bhsv->bhdv', w_s, kc, vc)
        n = decay[..., None] * n + jnp.einsum('bhs,bhsd->bhd', w_s, kc)
        return (C, n, m_new), h

    init = (jnp.zeros((B, NH, DK, DV), jnp.float32), jnp.zeros((B, NH, DK), jnp.float32),
            jnp.zeros((B, NH), jnp.float32))
    _, hs = lax.scan(step, init, (chunks(q), chunks(k), chunks(v), chunks(li), chunks(lf)))
    h = jnp.moveaxis(hs, (0, 2), (1, 3)).reshape(B, S, NH, DV)
    mu = h.mean(-1, keepdims=True)
    var = jnp.mean(jnp.square(h - mu), -1, keepdims=True)
    hn = ((h - mu) * lax.rsqrt(var + LN_EPS)).reshape(B, S, ML_V_W) * norm_g
    return (jax.nn.sigmoid(o_pre.astype(jnp.float32)) * hn).astype(dt)


def memory_attention(q, mem, w_kv):
    B, S, _ = q.shape
    M = mem.shape[1]
    kv = (mem @ w_kv).reshape(B, M, 2, MEM_HEADS, MEM_HEAD_DIM)
    k, v = kv[:, :, 0], kv[:, :, 1]
    q = q.reshape(B, S, MEM_HEADS, MEM_HEAD_DIM)
    s = jnp.einsum('bshd,bmhd->bhsm', q, k).astype(jnp.float32) * MEM_HEAD_DIM ** -0.5
    p = jax.nn.softmax(s, axis=-1).astype(q.dtype)
    return jnp.einsum('bhsm,bmhd->bshd', p, v).reshape(B, S, MEM_W)


def swiglu(t, wg, wu, wd):
    return (jax.nn.silu(t @ wg) * (t @ wu)) @ wd


def routed_experts(t, idx, wts, w_gate, w_up, w_down):
    T, D = t.shape
    E = w_gate.shape[0]
    M = T * TOP_K
    flat_e = idx.reshape(M)
    order = jnp.argsort(flat_e)
    e_sorted = flat_e[order]
    counts = jnp.bincount(flat_e, length=E)
    padded = (counts + MOE_BLOCK - 1) // MOE_BLOCK * MOE_BLOCK
    pad_end = jnp.cumsum(padded)
    pad_start = pad_end - padded
    start = jnp.cumsum(counts) - counts
    dest = pad_start[e_sorted] + jnp.arange(M) - start[e_sorted]
    n_blocks = -(-M // MOE_BLOCK) + E
    P = n_blocks * MOE_BLOCK
    row_tok = jnp.zeros((P,), jnp.int32).at[dest].set((order // TOP_K).astype(jnp.int32))
    row_w = jnp.zeros((P,), wts.dtype).at[dest].set(wts.reshape(M)[order])
    blk_e = jnp.minimum(jnp.searchsorted(pad_end, jnp.arange(n_blocks) * MOE_BLOCK, side='right'), E - 1)

    def block_fn(args):
        e, tok, wr = args
        xb = t[tok]
        return swiglu(xb, w_gate[e], w_up[e], w_down[e]) * wr[:, None]

    out = lax.map(block_fn, (blk_e, row_tok.reshape(n_blocks, MOE_BLOCK), row_w.reshape(n_blocks, MOE_BLOCK)))
    return jax.ops.segment_sum(out.reshape(P, D), row_tok, num_segments=T)


def moe_ffn(h, router_w, router_bias, w_e_gate, w_e_up, w_e_down, w_s_gate, w_s_up, w_s_down):
    B, S, D = h.shape
    t = h.reshape(B * S, D)
    aff = jax.nn.sigmoid((t @ router_w).astype(jnp.float32))
    _, idx = lax.top_k(aff + router_bias.astype(jnp.float32), TOP_K)
    w = jnp.take_along_axis(aff, idx, axis=-1)
    w = (w / w.sum(-1, keepdims=True) * ROUTED_SCALE).astype(h.dtype)
    routed = routed_experts(t, idx, w, w_e_gate, w_e_up, w_e_down)
    return (routed + swiglu(t, w_s_gate, w_s_up, w_s_down)).reshape(B, S, D)


def hybrid_layer(x, mem, w_in, mlstm_conv_w, mlstm_conv_b, mlstm_if_bias, mlstm_norm_g,
                 cmp_pos_k, cmp_k_w1, cmp_k_w2, cmp_pos_v, cmp_v_w1, cmp_v_w2,
                 w_mem_kv, w_branch, w_o, ln1_g, ln1_b, router_w, router_bias,
                 w_e_gate, w_e_up, w_e_down, w_s_gate, w_s_up, w_s_down, ln2_g, ln2_b):
    B, S, D = x.shape
    splits = np.cumsum(IN_SIZES)[:-1].tolist()
    (q_nsa, k_cmp, v_cmp, k_slc, v_slc, k_win, v_win, g_nsa,
     q_ml, k_ml, v_ml, o_ml, i_ml, f_ml, q_mem, g_br) = jnp.split(x @ w_in, splits, axis=-1)
    y_nsa = nsa_mixer(q_nsa, k_cmp, v_cmp, k_slc, v_slc, k_win, v_win, g_nsa,
                      cmp_pos_k, cmp_k_w1, cmp_k_w2, cmp_pos_v, cmp_v_w1, cmp_v_w2)
    y_ml = mlstm_mixer(q_ml, k_ml, v_ml, o_ml, i_ml, f_ml, mlstm_conv_w, mlstm_conv_b,
                       mlstm_if_bias, mlstm_norm_g)
    y_mem = memory_attention(q_mem, mem, w_mem_kv)
    gates = jax.nn.sigmoid(g_br.astype(jnp.float32)).reshape(B, S, N_BRANCH, D).astype(x.dtype)
    merged = (gates[:, :, 0] * (y_nsa @ w_branch[0]) + gates[:, :, 1] * (y_ml @ w_branch[1])
              + gates[:, :, 2] * (y_mem @ w_branch[2]))
    x = layer_norm(DN_ALPHA * x + merged @ w_o, ln1_g, ln1_b)
    y = moe_ffn(x, router_w, router_bias, w_e_gate, w_e_up, w_e_down, w_s_gate, w_s_up, w_s_down)
    return layer_norm(DN_ALPHA * x + y, ln2_g, ln2_b)


def setup_inputs(seed: int = 0) -> dict:
    key = jax.random.key(seed)
    ks = jax.random.split(key, 32)

    def nrm(i, shape, scale):
        return jax.random.normal(ks[i], shape, jnp.float32) * scale

    L = DEPTH
    D = D_MODEL
    i_bias = nrm(5, (L, 1, MLSTM_HEADS), 0.1)
    f_bias = jnp.linspace(3.0, 6.0, MLSTM_HEADS)[None, None, :] + nrm(6, (L, 1, MLSTM_HEADS), 0.1)
    return {
        'x': nrm(0, (BATCH, SEQ, D), 1.0),
        'mem': nrm(1, (BATCH, MEM_LEN, D), 1.0),
        'w_in': nrm(2, (L, D, IN_COLS), D ** -0.5),
        'mlstm_conv_w': nrm(3, (L, MLSTM_CONV, 2 * ML_QK_W), 0.5),
        'mlstm_conv_b': nrm(4, (L, 2 * ML_QK_W), 0.02),
        'mlstm_if_bias': jnp.concatenate([i_bias, f_bias], axis=1),
        'mlstm_norm_g': 1.0 + nrm(7, (L, ML_V_W), 0.02),
        'cmp_pos_k': nrm(8, (L, CMP_LEN, HEAD_DIM), 0.02),
        'cmp_k_w1': nrm(9, (L, CMP_LEN * HEAD_DIM, HEAD_DIM), (CMP_LEN * HEAD_DIM) ** -0.5),
        'cmp_k_w2': nrm(10, (L, HEAD_DIM, HEAD_DIM), HEAD_DIM ** -0.5),
        'cmp_pos_v': nrm(11, (L, CMP_LEN, HEAD_DIM), 0.02),
        'cmp_v_w1': nrm(12, (L, CMP_LEN * HEAD_DIM, HEAD_DIM), (CMP_LEN * HEAD_DIM) ** -0.5),
        'cmp_v_w2': nrm(13, (L, HEAD_DIM, HEAD_DIM), HEAD_DIM ** -0.5),
        'w_mem_kv': nrm(14, (L, D, 2 * MEM_W), D ** -0.5),
        'w_branch': nrm(15, (L, N_BRANCH, BRANCH_W, D), BRANCH_W ** -0.5),
        'w_o': nrm(16, (L, D, D), D ** -0.5 * DN_BETA),
        'ln1_g': 1.0 + nrm(17, (L, D), 0.02),
        'ln1_b': nrm(18, (L, D), 0.02),
        'router_w': nrm(19, (L, D, N_EXPERTS), D ** -0.5),
        'router_bias': nrm(20, (L, N_EXPERTS), 0.01),
        'w_e_gate': nrm(21, (L, N_EXPERTS, D, EXPERT_DIM), D ** -0.5),
        'w_e_up': nrm(22, (L, N_EXPERTS, D, EXPERT_DIM), D ** -0.5),
        'w_e_down': nrm(23, (L, N_EXPERTS, EXPERT_DIM, D), EXPERT_DIM ** -0.5 * DN_BETA),
        'w_s_gate': nrm(24, (L, D, SHARED_DIM), D ** -0.5),
        'w_s_up': nrm(25, (L, D, SHARED_DIM), D ** -0.5),
        'w_s_down': nrm(26, (L, SHARED_DIM, D), SHARED_DIM ** -0.5 * DN_BETA),
        'ln2_g': 1.0 + nrm(27, (L, D), 0.02),
        'ln2_b': nrm(28, (L, D), 0.02),
    }


def reference(x, mem, w_in, mlstm_conv_w, mlstm_conv_b, mlstm_if_bias, mlstm_norm_g,
              cmp_pos_k, cmp_k_w1, cmp_k_w2, cmp_pos_v, cmp_v_w1, cmp_v_w2,
              w_mem_kv, w_branch, w_o, ln1_g, ln1_b, router_w, router_bias,
              w_e_gate, w_e_up, w_e_down, w_s_gate, w_s_up, w_s_down, ln2_g, ln2_b):
    for layer in range(DEPTH):
        x = hybrid_layer(x, mem, w_in[layer], mlstm_conv_w[layer], mlstm_conv_b[layer],
                         mlstm_if_bias[layer], mlstm_norm_g[layer],
                         cmp_pos_k[layer], cmp_k_w1[layer], cmp_k_w2[layer],
                         cmp_pos_v[layer], cmp_v_w1[layer], cmp_v_w2[layer],
                         w_mem_kv[layer], w_branch[layer], w_o[layer], ln1_g[layer], ln1_b[layer],
                         router_w[layer], router_bias[layer], w_e_gate[layer], w_e_up[layer],
                         w_e_down[layer], w_s_gate[layer], w_s_up[layer], w_s_down[layer],
                         ln2_g[layer], ln2_b[layer])
    return x
```

```python
import functools

import jax
import jax.numpy as jnp
import numpy as np
from jax import lax
from jax.experimental import pallas as pl
from jax.experimental.pallas import tpu as pltpu

F32 = jnp.float32
BF16 = jnp.bfloat16

D_MODEL = 2048
HEAD_DIM = 128
NSA_HEADS = 8
NSA_KV_GROUPS = 2
NSA_REP = NSA_HEADS // NSA_KV_GROUPS
CMP_LEN = 32
CMP_STRIDE = 16
SLC_LEN = 64
SLC_TOPK = 16
WINDOW = 512
MLSTM_HEADS = 4
MLSTM_QK_DIM = 128
MLSTM_V_DIM = 256
MLSTM_CHUNK = 64
MEM_HEADS = 4
MEM_HEAD_DIM = 256
N_EXPERTS = 64
TOP_K = 6
EXPERT_DIM = 1408
ROUTED_SCALE = 2.5
ROPE_THETA = 10000.0
LN_EPS = 1e-5
NEG_INF = -1e30
DEPTH = 1
DN_ALPHA = (2 * DEPTH) ** 0.25

NSA_Q_W = NSA_HEADS * HEAD_DIM
NSA_KV_W = NSA_KV_GROUPS * HEAD_DIM
ML_QK_W = MLSTM_HEADS * MLSTM_QK_DIM
ML_V_W = MLSTM_HEADS * MLSTM_V_DIM
MEM_W = MEM_HEADS * MEM_HEAD_DIM
N_BRANCH = 3

LANES = 128
VMEM_LIMIT = 56 * 1024 * 1024

_SEGS = {}
_off = 0
for _name, _w in (("q_nsa", NSA_Q_W), ("k_cmp", NSA_KV_W), ("v_cmp", NSA_KV_W), ("k_slc", NSA_KV_W),
                  ("v_slc", NSA_KV_W), ("k_win", NSA_KV_W), ("v_win", NSA_KV_W),
                  ("g_nsa", NSA_KV_GROUPS * LANES), ("q_ml", ML_QK_W), ("k_ml", ML_QK_W),
                  ("v_ml", ML_V_W), ("o_ml", ML_V_W), ("if_ml", LANES), ("q_mem", MEM_W),
                  ("g_br", N_BRANCH * D_MODEL)):
    _SEGS[_name] = (_off, _w)
    _off += _w
IN_TN = 512
IN_PAD = -(-_off // IN_TN) * IN_TN


def _pad_in_proj(w_in):
    sizes = (NSA_Q_W,) + (NSA_KV_W,) * 6 + (NSA_HEADS * 3, ML_QK_W, ML_QK_W, ML_V_W, ML_V_W,
                                            MLSTM_HEADS, MLSTM_HEADS, MEM_W, N_BRANCH * D_MODEL)
    splits = np.cumsum(sizes)[:-1].tolist()
    (q_nsa, k_cmp, v_cmp, k_slc, v_slc, k_win, v_win, g_nsa,
     q_ml, k_ml, v_ml, o_ml, i_ml, f_ml, q_mem, g_br) = jnp.split(w_in, splits, axis=-1)
    d = w_in.shape[0]
    per_group = NSA_REP * 3
    g_blocks = [jnp.pad(g_nsa[:, g * per_group:(g + 1) * per_group], ((0, 0), (0, LANES - per_group)))
                for g in range(NSA_KV_GROUPS)]
    if_blk = jnp.pad(jnp.concatenate([i_ml, f_ml], -1), ((0, 0), (0, LANES - 2 * MLSTM_HEADS)))
    cols = [q_nsa, k_cmp, v_cmp, k_slc, v_slc, k_win, v_win] + g_blocks + [
        q_ml, k_ml, v_ml, o_ml, if_blk, q_mem, g_br]
    w = jnp.concatenate(cols, axis=-1)
    return jnp.pad(w, ((0, 0), (0, IN_PAD - w.shape[1])))


def _mm_kernel(a_ref, b_ref, o_ref):
    o_ref[...] = jnp.dot(a_ref[...].astype(BF16), b_ref[...].astype(BF16),
                         preferred_element_type=F32).astype(o_ref.dtype)


def _matmul(a, b, *, tm, tn, out_dtype=F32, name="matmul"):
    m, k = a.shape
    _, n = b.shape
    assert m % tm == 0 and n % tn == 0
    return pl.pallas_call(
        _mm_kernel,
        out_shape=jax.ShapeDtypeStruct((m, n), out_dtype),
        grid=(m // tm, n // tn),
        in_specs=[pl.BlockSpec((tm, k), lambda i, j: (i, 0)),
                  pl.BlockSpec((k, tn), lambda i, j: (0, j))],
        out_specs=pl.BlockSpec((tm, tn), lambda i, j: (i, j)),
        compiler_params=pltpu.CompilerParams(dimension_semantics=("parallel", "arbitrary"),
                                             vmem_limit_bytes=VMEM_LIMIT),
        name=name,
    )(a, b)


def _rope_tables(pos):
    half = HEAD_DIM // 2
    inv = ROPE_THETA ** (-jnp.arange(half, dtype=F32) / half)
    ang = pos.astype(F32)[:, None] * inv[None, :]
    cos, sin = jnp.cos(ang), jnp.sin(ang)
    return jnp.concatenate([cos, cos], -1), jnp.concatenate([-sin, sin], -1)


def _rope(x, cos, sin_signed):
    return x * cos + pltpu.roll(x, HEAD_DIM // 2, 1) * sin_signed


def _nsa_prep_kernel(kvc_ref, ks_ref, vs_ref, kw_ref, vw_ref, cos_ref, sin_ref,
                     kvc_o, ks_o, vs_o, kw_o, vw_o):
    cos, sin = cos_ref[...], sin_ref[...]
    kvc_o[0] = kvc_ref[0]
    for src, dst in ((ks_ref, ks_o), (kw_ref, kw_o)):
        for g in range(NSA_KV_GROUPS):
            sl = slice(g * HEAD_DIM, (g + 1) * HEAD_DIM)
            dst[0, :, sl] = _rope(src[0, :, sl], cos, sin).astype(BF16)
    vs_o[0] = vs_ref[0].astype(BF16)
    vw_o[0] = vw_ref[0].astype(BF16)


def _nsa_prep(y3, cos, sin, *, ts=512):
    b, s, _ = y3.shape
    kvw = NSA_KV_W

    def col(name, width):
        blk = _SEGS[name][0] // width
        return pl.BlockSpec((1, ts, width), lambda i, j: (i, j, blk))

    tab = pl.BlockSpec((ts, HEAD_DIM), lambda i, j: (j, 0))
    out_kv = pl.BlockSpec((1, ts, kvw), lambda i, j: (i, j, 0))
    return pl.pallas_call(
        _nsa_prep_kernel,
        out_shape=(jax.ShapeDtypeStruct((b, s, 2 * kvw), F32),) + (jax.ShapeDtypeStruct((b, s, kvw), BF16),) * 4,
        grid=(b, s // ts),
        in_specs=[col("k_cmp", 2 * kvw), col("k_slc", kvw), col("v_slc", kvw), col("k_win", kvw),
                  col("v_win", kvw), tab, tab],
        out_specs=(pl.BlockSpec((1, ts, 2 * kvw), lambda i, j: (i, j, 0)), out_kv, out_kv, out_kv, out_kv),
        compiler_params=pltpu.CompilerParams(dimension_semantics=("parallel", "parallel")),
        name="nsa_prep",
    )(y3, y3, y3, y3, y3, cos, sin)


def _cmp_kernel(u_ref, posa_ref, posb_ref, w1a_ref, w1b_ref, w2_ref, cos_ref, sin_ref, kc_o, vc_o):
    n_rows = u_ref.shape[2]
    for idx, out in ((0, kc_o), (1, vc_o)):
        u = u_ref[0, idx]
        a = jnp.dot((u + posa_ref[idx]).astype(BF16), w1a_ref[idx], preferred_element_type=F32)
        bm = jnp.dot((u + posb_ref[idx]).astype(BF16), w1b_ref[idx], preferred_element_type=F32)
        pre = a + pltpu.roll(bm, n_rows - 1, 0)
        h = jax.nn.gelu(pre).astype(BF16)
        c = jnp.dot(h, w2_ref[idx], preferred_element_type=F32)
        if idx == 0:
            for g in range(NSA_KV_GROUPS):
                sl = slice(g * HEAD_DIM, (g + 1) * HEAD_DIM)
                out[0, :, sl] = _rope(c[:, sl], cos_ref[...], sin_ref[...]).astype(BF16)
        else:
            out[0] = c.astype(BF16)


def _block_diag_groups(w):
    z = jnp.zeros_like(w)
    rows = [jnp.concatenate([w if gi == go else z for go in range(NSA_KV_GROUPS)], -1)
            for gi in range(NSA_KV_GROUPS)]
    return jnp.concatenate(rows, -2)


def _compress(kvc, cmp_pos_k, cmp_k_w1, cmp_k_w2, cmp_pos_v, cmp_v_w1, cmp_v_w2, cos_e, sin_e):
    b, s, _ = kvc.shape
    n_rows = s // CMP_STRIDE
    g, hd = NSA_KV_GROUPS, HEAD_DIM
    row_w = CMP_STRIDE * g * hd
    u = kvc.reshape(b, n_rows, CMP_STRIDE, 2, g * hd).transpose(0, 3, 1, 2, 4).reshape(b, 2, n_rows, row_w)

    def pos_rows(pos):
        p = jnp.broadcast_to(pos.reshape(2, CMP_STRIDE, 1, hd), (2, CMP_STRIDE, g, hd))
        return p.reshape(2, 1, row_w)

    def w1_halves(w1):
        w = w1.reshape(2, CMP_STRIDE, hd, hd)
        return _block_diag_groups(w).reshape(2, row_w, g * hd).astype(BF16)

    pk, pv = pos_rows(cmp_pos_k), pos_rows(cmp_pos_v)
    wk, wv = w1_halves(cmp_k_w1), w1_halves(cmp_v_w1)
    posa = jnp.stack([pk[0], pv[0]])
    posb = jnp.stack([pk[1], pv[1]])
    w1a = jnp.stack([wk[0], wv[0]])
    w1b = jnp.stack([wk[1], wv[1]])
    w2 = jnp.stack([_block_diag_groups(cmp_k_w2), _block_diag_groups(cmp_v_w2)]).astype(BF16)
    full = lambda shp: pl.BlockSpec(shp, lambda i: (0,) * len(shp))
    out_spec = pl.BlockSpec((1, n_rows, g * hd), lambda i: (i, 0, 0))
    return pl.pallas_call(
        _cmp_kernel,
        out_shape=(jax.ShapeDtypeStruct((b, n_rows, g * hd), BF16),) * 2,
        grid=(b,),
        in_specs=[pl.BlockSpec((1, 2, n_rows, row_w), lambda i: (i, 0, 0, 0)),
                  full(posa.shape), full(posb.shape), full(w1a.shape), full(w1b.shape), full(w2.shape),
                  full(cos_e.shape), full(sin_e.shape)],
        out_specs=(out_spec, out_spec),
        compiler_params=pltpu.CompilerParams(dimension_semantics=("parallel",), vmem_limit_bytes=VMEM_LIMIT),
        name="nsa_compress",
    )(u, posa, posb, w1a, w1b, w2, cos_e, sin_e)


NSA_TQ = 256
NSA_TK = 256


def _flash(q4, k_ref, v_ref, kt_lo, kt_hi, bias_fn, acc_ref):
    rows = q4.shape[0]
    tq = rows // NSA_REP
    scale = HEAD_DIM ** -0.5
    acc_ref[...] = jnp.zeros_like(acc_ref)

    def body(kt, carry):
        m, l = carry
        k0 = pl.multiple_of(kt * NSA_TK, NSA_TK)
        k = k_ref[0, pl.ds(k0, NSA_TK), :]
        v = v_ref[0, pl.ds(k0, NSA_TK), :]
        s = lax.dot_general(q4, k, (((1,), (1,)), ((), ())), preferred_element_type=F32) * scale
        s = s.reshape(NSA_REP, tq, NSA_TK) + bias_fn(k0)[None]
        m_new = jnp.maximum(m, s.max(-1, keepdims=True))
        alpha = jnp.exp(m - m_new)
        p = jnp.exp(s - m_new)
        l = alpha * l + p.sum(-1, keepdims=True)
        pv = jnp.dot(p.reshape(rows, NSA_TK).astype(BF16), v, preferred_element_type=F32)
        acc_ref[...] = acc_ref[...] * alpha.reshape(rows, 1) + pv
        return m_new, l

    init = (jnp.full((NSA_REP, tq, 1), NEG_INF, F32), jnp.zeros((NSA_REP, tq, 1), F32))
    _, l = lax.fori_loop(kt_lo, kt_hi, body, init)
    return acc_ref[...] / l.reshape(rows, 1)


def _nsa_kernel(q_ref, gate_ref, kc_ref, vc_ref, ks_ref, vs_ref, kw_ref, vw_ref, cos_ref, sin_ref,
                cover_ref, o_ref, acc_ref, *, n_slc):
    qi = pl.program_id(2)
    tq = NSA_TQ
    rows = NSA_REP * tq
    q0 = qi * tq
    cos, sin = cos_ref[...], sin_ref[...]
    q4 = jnp.concatenate(
        [_rope(q_ref[0, :, r * HEAD_DIM:(r + 1) * HEAD_DIM], cos, sin).astype(BF16) for r in range(NSA_REP)],
        axis=0)
    tpos = q0 + lax.broadcasted_iota(jnp.int32, (tq, LANES), 0)
    lane = lax.broadcasted_iota(jnp.int32, (tq, LANES), 1)

    s = lax.dot_general(q4, kc_ref[0], (((1,), (1,)), ((), ())), preferred_element_type=F32)
    s = (s * HEAD_DIM ** -0.5).reshape(NSA_REP, tq, LANES)
    cmask = (lane * CMP_STRIDE + (CMP_LEN - 1) <= tpos)[None]
    s = jnp.where(cmask, s, NEG_INF)
    e = jnp.exp(s - s.max(-1, keepdims=True))
    p = e / e.sum(-1, keepdims=True) * cmask.astype(F32)
    o_cmp = jnp.dot(p.reshape(rows, LANES).astype(BF16), vc_ref[0], preferred_element_type=F32)

    p_sum = p.sum(0)
    p_hi = p_sum.astype(BF16)
    p_lo = (p_sum - p_hi.astype(F32)).astype(BF16)
    cover = cover_ref[...]
    p_slc = (jnp.dot(p_hi, cover, preferred_element_type=F32)
             + jnp.dot(p_lo, cover, preferred_element_type=F32))
    t_blk = tpos // SLC_LEN
    forced = (lane == 0) | (lane == t_blk) | (lane == t_blk - 1)
    future = lane > t_blk
    score = jnp.where(forced, jnp.inf, jnp.where(future, -jnp.inf, p_slc))
    cnt = jnp.zeros((tq, LANES), jnp.int32)
    for i in range(n_slc):
        col = score[:, i:i + 1]
        ahead = (col > score) | ((col == score) & (lane > i))
        cnt = cnt + ahead.astype(jnp.int32)
    n_sel = min(SLC_TOPK, n_slc)
    chosen = (cnt < n_sel) & (score > -jnp.inf)
    pow2 = jnp.left_shift(1, lane & 15).astype(F32)
    lo = jnp.where(chosen & (lane < 16), pow2, 0.0).sum(-1, keepdims=True)
    hi = jnp.where(chosen & (lane >= 16) & (lane < 32), pow2, 0.0).sum(-1, keepdims=True)
    bits = lo.astype(jnp.int32) | (hi.astype(jnp.int32) << 16)

    tpos_k = q0 + lax.broadcasted_iota(jnp.int32, (tq, NSA_TK), 0)
    kiota = lax.broadcasted_iota(jnp.int32, (tq, NSA_TK), 1)
    bits_b = jnp.broadcast_to(bits, (tq, NSA_TK))

    def slc_bias(k0):
        kpos = k0 + kiota
        picked = (lax.shift_right_logical(bits_b, kpos // SLC_LEN) & 1) == 1
        return jnp.where(picked & (kpos <= tpos_k), 0.0, NEG_INF)

    o_slc = _flash(q4, ks_ref, vs_ref, 0, (q0 + tq) // NSA_TK, slc_bias, acc_ref)

    def win_bias(k0):
        kpos = k0 + kiota
        return jnp.where((kpos <= tpos_k) & (kpos > tpos_k - WINDOW), 0.0, NEG_INF)

    kt_lo = jnp.maximum(q0 - WINDOW, 0) // NSA_TK
    o_win = _flash(q4, kw_ref, vw_ref, kt_lo, (q0 + tq) // NSA_TK, win_bias, acc_ref)

    gate = jax.nn.sigmoid(gate_ref[0])
    for r in range(NSA_REP):
        rs = slice(r * tq, (r + 1) * tq)
        o = (gate[:, 3 * r:3 * r + 1] * o_cmp[rs] + gate[:, 3 * r + 1:3 * r + 2] * o_slc[rs]
             + gate[:, 3 * r + 2:3 * r + 3] * o_win[rs])
        o_ref[0, :, r * HEAD_DIM:(r + 1) * HEAD_DIM] = o.astype(o_ref.dtype)


def _cover_matrix(n_rows, n_slc):
    i = np.arange(n_rows)[:, None]
    j = np.arange(LANES)[None, :]
    c0 = i * CMP_STRIDE
    c1 = c0 + CMP_LEN - 1
    s0 = j * SLC_LEN
    s1 = s0 + SLC_LEN - 1
    ok = (c1 >= s0) & (c0 <= s1) & (i < n_rows - 1) & (j < n_slc)
    return jnp.asarray(ok.astype(np.float32), dtype=BF16)


def _nsa_attention(y3, kc, vc, ks, vs, kw, vw, cos, sin):
    b, s, _ = y3.shape
    g, hd, tq = NSA_KV_GROUPS, HEAD_DIM, NSA_TQ
    n_rows = s // CMP_STRIDE
    assert n_rows == LANES and s // SLC_LEN <= 32 and s % tq == 0
    cover = _cover_matrix(n_rows, s // SLC_LEN)
    qw = NSA_REP * hd
    gate_blk = _SEGS["g_nsa"][0] // LANES
    seq_g = pl.BlockSpec((1, s, hd), lambda i, j, t: (i, 0, j))
    cmp_g = pl.BlockSpec((1, n_rows, hd), lambda i, j, t: (i, 0, j))
    tab = pl.BlockSpec((tq, hd), lambda i, j, t: (t, 0))
    return pl.pallas_call(
        functools.partial(_nsa_kernel, n_slc=s // SLC_LEN),
        out_shape=jax.ShapeDtypeStruct((b, s, NSA_Q_W), BF16),
        grid=(b, g, s // tq),
        in_specs=[pl.BlockSpec((1, tq, qw), lambda i, j, t: (i, t, j)),
                  pl.BlockSpec((1, tq, LANES), lambda i, j, t: (i, t, gate_blk + j)),
                  cmp_g, cmp_g, seq_g, seq_g, seq_g, seq_g, tab, tab,
                  pl.BlockSpec((n_rows, LANES), lambda i, j, t: (0, 0))],
        out_specs=pl.BlockSpec((1, tq, qw), lambda i, j, t: (i, t, j)),
        scratch_shapes=[pltpu.VMEM((NSA_REP * tq, hd), F32)],
        compiler_params=pltpu.CompilerParams(dimension_semantics=("parallel", "parallel", "arbitrary"),
                                             vmem_limit_bytes=VMEM_LIMIT),
        name="nsa_attention",
    )(y3, y3, kc, vc, ks, vs, kw, vw, cos, sin, cover)


FFN_TM = 256


def _ffn_kernel(blk_e_ref, n_used_ref, x_ref, w_ref, wg_ref, wu_ref, wd_ref, o_ref):
    i = pl.program_id(0)

    @pl.when(i < n_used_ref[0])
    def _():
        x = x_ref[...]
        gate = jnp.dot(x, wg_ref[0], preferred_element_type=F32)
        up = jnp.dot(x, wu_ref[0], preferred_element_type=F32)
        h = (jax.nn.silu(gate) * up).astype(BF16)
        o_ref[...] = jnp.dot(h, wd_ref[0], preferred_element_type=F32) * w_ref[...]

    @pl.when(i >= n_used_ref[0])
    def _():
        o_ref[...] = jnp.zeros_like(o_ref)


def _grouped_ffn(xs, row_w, blk_e, n_used, wg, wu, wd):
    p, d = xs.shape
    f = wg.shape[-1]
    n_blocks = p // FFN_TM
    grid_spec = pltpu.PrefetchScalarGridSpec(
        num_scalar_prefetch=2,
        grid=(n_blocks,),
        in_specs=[pl.BlockSpec((FFN_TM, d), lambda i, be, nu: (i, 0)),
                  pl.BlockSpec((FFN_TM, 1), lambda i, be, nu: (i, 0)),
                  pl.BlockSpec((1, d, f), lambda i, be, nu: (be[i], 0, 0)),
                  pl.BlockSpec((1, d, f), lambda i, be, nu: (be[i], 0, 0)),
                  pl.BlockSpec((1, f, d), lambda i, be, nu: (be[i], 0, 0))],
        out_specs=pl.BlockSpec((FFN_TM, d), lambda i, be, nu: (i, 0)),
    )
    return pl.pallas_call(
        _ffn_kernel,
        out_shape=jax.ShapeDtypeStruct((p, d), F32),
        grid_spec=grid_spec,
        compiler_params=pltpu.CompilerParams(dimension_semantics=("arbitrary",), vmem_limit_bytes=VMEM_LIMIT),
        name="grouped_ffn",
    )(blk_e, n_used, xs, row_w, wg, wu, wd)


def _layer_norm(x, g, b):
    mu = x.mean(-1, keepdims=True)
    var = jnp.mean(jnp.square(x - mu), -1, keepdims=True)
    return (x - mu) * lax.rsqrt(var + LN_EPS) * g + b


def _causal_conv(x, w, b):
    c = x.shape[-1]
    y = lax.conv_general_dilated(x, w[:, None, :], window_strides=(1,), padding=[(w.shape[0] - 1, 0)],
                                 dimension_numbers=("NWC", "WIO", "NWC"), feature_group_count=c)
    return y + b


def _mlstm_mixer(q, k, v, o_pre, i_pre, f_pre, conv_w, conv_b, if_bias, norm_g):
    B, S, _ = q.shape
    NH, DK, DV, L = MLSTM_HEADS, MLSTM_QK_DIM, MLSTM_V_DIM, MLSTM_CHUNK
    qk = jax.nn.silu(_causal_conv(jnp.concatenate([q, k], -1), conv_w, conv_b))
    q, k = jnp.split(qk, 2, axis=-1)
    q = q.reshape(B, S, NH, DK)
    k = k.reshape(B, S, NH, DK) * DK ** -0.5
    v = v.reshape(B, S, NH, DV)
    li = i_pre + if_bias[0]
    lf = jax.nn.log_sigmoid(f_pre + if_bias[1])
    nc = S // L

    def chunks(t):
        return jnp.moveaxis(t.reshape((B, nc, L) + t.shape[2:]), (1, 3), (0, 2))

    tri = jnp.tril(jnp.ones((L, L), dtype=bool))

    def step(carry, inp):
        C, n, m = carry
        qc, kc, vc, lic, lfc = inp
        b = jnp.cumsum(lfc, axis=-1)
        dmat = jnp.where(tri, b[..., :, None] - b[..., None, :] + lic[..., None, :], -jnp.inf)
        m_inter = b + m[..., None]
        m_t = jnp.maximum(m_inter, dmat.max(-1))
        wqk = jnp.exp(dmat - m_t[..., None]) * jnp.einsum("bhtd,bhsd->bhts", qc, kc)
        inter = jnp.exp(m_inter - m_t)
        num = jnp.einsum("bhts,bhsv->bhtv", wqk, vc) + inter[..., None] * jnp.einsum("bhtd,bhdv->bhtv", qc, C)
        den = wqk.sum(-1) + inter * jnp.einsum("bhtd,bhd->bht", qc, n)
        h = num / jnp.maximum(jnp.abs(den), jnp.exp(-m_t))[..., None]
        b_last = b[..., -1]
        g = b_last[..., None] - b + lic
        m_new = jnp.maximum(b_last + m, g.max(-1))
        decay = jnp.exp(b_last + m - m_new)
        w_s = jnp.exp(g - m_new[..., None])
        C = decay[..., None, None] * C + jnp.einsum("bhs,bhsd,bhsv->bhdv", w_s, kc, vc)
        n = decay[..., None] * n + jnp.einsum("bhs,bhsd->bhd", w_s, kc)
        return (C, n, m_new), h

    init = (jnp.zeros((B, NH, DK, DV), F32), jnp.zeros((B, NH, DK), F32), jnp.zeros((B, NH), F32))
    _, hs = lax.scan(step, init, (chunks(q), chunks(k), chunks(v), chunks(li), chunks(lf)))
    h = jnp.moveaxis(hs, (0, 2), (1, 3)).reshape(B, S, NH, DV)
    mu = h.mean(-1, keepdims=True)
    var = jnp.mean(jnp.square(h - mu), -1, keepdims=True)
    hn = ((h - mu) * lax.rsqrt(var + LN_EPS)).reshape(B, S, ML_V_W) * norm_g
    return jax.nn.sigmoid(o_pre) * hn


def _memory_attention(q, kv):
    B, S, _ = q.shape
    M = kv.shape[1]
    kv = kv.reshape(B, M, 2, MEM_HEADS, MEM_HEAD_DIM)
    k, v = kv[:, :, 0], kv[:, :, 1]
    q = q.reshape(B, S, MEM_HEADS, MEM_HEAD_DIM)
    s = jnp.einsum("bshd,bmhd->bhsm", q, k) * MEM_HEAD_DIM ** -0.5
    p = jax.nn.softmax(s, axis=-1)
    return jnp.einsum("bhsm,bmhd->bshd", p, v).reshape(B, S, MEM_W)


def _moe_dispatch(idx, wts, t):
    T = idx.shape[0]
    E = N_EXPERTS
    M = T * TOP_K
    flat_e = idx.reshape(M)
    order = jnp.argsort(flat_e)
    e_sorted = flat_e[order]
    counts = jnp.bincount(flat_e, length=E)
    padded = (counts + FFN_TM - 1) // FFN_TM * FFN_TM
    pad_end = jnp.cumsum(padded)
    pad_start = pad_end - padded
    start = jnp.cumsum(counts) - counts
    dest = (pad_start[e_sorted] + jnp.arange(M) - start[e_sorted]).astype(jnp.int32)
    n_blocks = -(-M // FFN_TM) + E
    P = n_blocks * FFN_TM
    row_tok = jnp.zeros((P,), jnp.int32).at[dest].set((order // TOP_K).astype(jnp.int32))
    row_w = jnp.zeros((P,), wts.dtype).at[dest].set(wts.reshape(M)[order])
    blk_e = jnp.minimum(jnp.searchsorted(pad_end, jnp.arange(n_blocks) * FFN_TM, side="right"), E - 1)
    n_used = (pad_end[-1] // FFN_TM).astype(jnp.int32).reshape(1)
    pos = jnp.zeros((M,), jnp.int32).at[order].set(dest)
    return row_tok, row_w, blk_e.astype(jnp.int32), n_used, pos.reshape(T, TOP_K)


def _layer(x, mem, w_in, mlstm_conv_w, mlstm_conv_b, mlstm_if_bias, mlstm_norm_g,
           cmp_pos_k, cmp_k_w1, cmp_k_w2, cmp_pos_v, cmp_v_w1, cmp_v_w2,
           w_mem_kv, w_branch, w_o, ln1_g, ln1_b, router_w, router_bias,
           w_e_gate, w_e_up, w_e_down, w_s_gate, w_s_up, w_s_down, ln2_g, ln2_b):
    B, S, D = x.shape
    T = B * S
    xt = x.reshape(T, D)
    y = _matmul(xt.astype(BF16), _pad_in_proj(w_in).astype(BF16), tm=1024, tn=IN_TN, name="in_proj")
    y3 = y.reshape(B, S, IN_PAD)

    def seg(name, width=None, off=0):
        o, w = _SEGS[name]
        return y3[:, :, o + off:o + off + (width or w)]

    pos = jnp.arange(S)
    cos, sin = _rope_tables(pos)
    n_rows = S // CMP_STRIDE
    cos_e, sin_e = _rope_tables(jnp.arange(n_rows) * CMP_STRIDE + CMP_LEN - 1)
    kvc, ks, vs, kw, vw = _nsa_prep(y3, cos, sin)
    kc, vc = _compress(kvc, cmp_pos_k, cmp_k_w1, cmp_k_w2, cmp_pos_v, cmp_v_w1, cmp_v_w2, cos_e, sin_e)
    y_nsa = _nsa_attention(y3, kc, vc, ks, vs, kw, vw, cos, sin)

    y_ml = _mlstm_mixer(seg("q_ml"), seg("k_ml"), seg("v_ml"), seg("o_ml"),
                        seg("if_ml", MLSTM_HEADS), seg("if_ml", MLSTM_HEADS, MLSTM_HEADS),
                        mlstm_conv_w, mlstm_conv_b, mlstm_if_bias, mlstm_norm_g)

    kv_mem = _matmul(mem.reshape(B * mem.shape[1], D).astype(BF16), w_mem_kv.astype(BF16), tm=1024, tn=512,
                     name="mem_kv").reshape(B, mem.shape[1], 2 * MEM_W)
    y_mem = _memory_attention(seg("q_mem"), kv_mem)

    gates = jax.nn.sigmoid(seg("g_br")).reshape(B, S, N_BRANCH, D)
    wb = w_branch.astype(BF16)
    proj = [_matmul(t.reshape(T, -1).astype(BF16), wb[i], tm=1024, tn=512, name=f"branch{i}").reshape(B, S, D)
            for i, t in enumerate((y_nsa, y_ml, y_mem))]
    merged = gates[:, :, 0] * proj[0] + gates[:, :, 1] * proj[1] + gates[:, :, 2] * proj[2]
    attn_out = _matmul(merged.reshape(T, D).astype(BF16), w_o.astype(BF16), tm=1024, tn=512, name="w_o")
    x1 = _layer_norm(DN_ALPHA * xt + attn_out, ln1_g, ln1_b)

    aff = jax.nn.sigmoid(jnp.dot(x1, router_w, precision=lax.Precision.HIGHEST))
    _, idx = lax.top_k(aff + router_bias, TOP_K)
    w = jnp.take_along_axis(aff, idx, axis=-1)
    w = w / w.sum(-1, keepdims=True) * ROUTED_SCALE
    row_tok, row_w, blk_e, n_used, pos_tk = _moe_dispatch(idx, w, x1)
    x1b = x1.astype(BF16)
    out = _grouped_ffn(x1b[row_tok], row_w[:, None], blk_e, n_used,
                       w_e_gate.astype(BF16), w_e_up.astype(BF16), w_e_down.astype(BF16))
    routed = out[pos_tk].sum(1)
    n_tb = T // FFN_TM
    shared = _grouped_ffn(x1b, jnp.ones((T, 1), F32), jnp.zeros((n_tb,), jnp.int32),
                          jnp.full((1,), n_tb, jnp.int32), w_s_gate.astype(BF16)[None],
                          w_s_up.astype(BF16)[None], w_s_down.astype(BF16)[None])
    out = _layer_norm(DN_ALPHA * x1 + routed + shared, ln2_g, ln2_b)
    return out.reshape(B, S, D)


def kernel(x, mem, w_in, mlstm_conv_w, mlstm_conv_b, mlstm_if_bias, mlstm_norm_g, cmp_pos_k, cmp_k_w1, cmp_k_w2, cmp_pos_v, cmp_v_w1, cmp_v_w2, w_mem_kv, w_branch, w_o, ln1_g, ln1_b, router_w, router_bias, w_e_gate, w_e_up, w_e_down, w_s_gate, w_s_up, w_s_down, ln2_g, ln2_b):
    for layer in range(DEPTH):
        x = _layer(x, mem, w_in[layer], mlstm_conv_w[layer], mlstm_conv_b[layer],
                   mlstm_if_bias[layer], mlstm_norm_g[layer],
                   cmp_pos_k[layer], cmp_k_w1[layer], cmp_k_w2[layer],
                   cmp_pos_v[layer], cmp_v_w1[layer], cmp_v_w2[layer],
                   w_mem_kv[layer], w_branch[layer], w_o[layer], ln1_g[layer], ln1_b[layer],
                   router_w[layer], router_bias[layer], w_e_gate[layer], w_e_up[layer],
                   w_e_down[layer], w_s_gate[layer], w_s_up[layer], w_s_down[layer],
                   ln2_g[layer], ln2_b[layer])
    return x
```

```python
import functools

import jax
import jax.numpy as jnp
import numpy as np
from jax import lax
from jax.experimental import pallas as pl
from jax.experimental.pallas import tpu as pltpu

F32 = jnp.float32
BF16 = jnp.bfloat16

D_MODEL = 2048
HEAD_DIM = 128
NSA_HEADS = 8
NSA_KV_GROUPS = 2
NSA_REP = NSA_HEADS // NSA_KV_GROUPS
CMP_LEN = 32
CMP_STRIDE = 16
SLC_LEN = 64
SLC_TOPK = 16
WINDOW = 512
MLSTM_HEADS = 4
MLSTM_QK_DIM = 128
MLSTM_V_DIM = 256
MLSTM_CHUNK = 64
MEM_HEADS = 4
MEM_HEAD_DIM = 256
N_EXPERTS = 64
TOP_K = 6
EXPERT_DIM = 1408
ROUTED_SCALE = 2.5
ROPE_THETA = 10000.0
LN_EPS = 1e-5
NEG_INF = -1e30
DEPTH = 1
DN_ALPHA = (2 * DEPTH) ** 0.25

NSA_Q_W = NSA_HEADS * HEAD_DIM
NSA_KV_W = NSA_KV_GROUPS * HEAD_DIM
ML_QK_W = MLSTM_HEADS * MLSTM_QK_DIM
ML_V_W = MLSTM_HEADS * MLSTM_V_DIM
MEM_W = MEM_HEADS * MEM_HEAD_DIM
N_BRANCH = 3

LANES = 128
VMEM_LIMIT = 56 * 1024 * 1024

_SEGS = {}
_off = 0
for _name, _w, _blk in (("g_br", N_BRANCH * D_MODEL, D_MODEL), ("q_nsa", NSA_Q_W, NSA_REP * HEAD_DIM),
                        ("q_mem", MEM_W, MEM_W), ("v_ml", ML_V_W, ML_V_W), ("o_ml", ML_V_W, ML_V_W),
                        ("qk_ml", 2 * ML_QK_W, 2 * ML_QK_W), ("kv_cmp", 2 * NSA_KV_W, 2 * NSA_KV_W),
                        ("k_slc", NSA_KV_W, NSA_KV_W), ("v_slc", NSA_KV_W, NSA_KV_W),
                        ("k_win", NSA_KV_W, NSA_KV_W), ("v_win", NSA_KV_W, NSA_KV_W),
                        ("g_nsa", NSA_KV_GROUPS * LANES, LANES), ("if_ml", LANES, LANES)):
    assert _off % _blk == 0
    _SEGS[_name] = (_off, _w)
    _off += _w
IN_TN = 512
IN_PAD = -(-_off // IN_TN) * IN_TN


def _pad_in_proj(w_in):
    sizes = (NSA_Q_W,) + (NSA_KV_W,) * 6 + (NSA_HEADS * 3, ML_QK_W, ML_QK_W, ML_V_W, ML_V_W,
                                            MLSTM_HEADS, MLSTM_HEADS, MEM_W, N_BRANCH * D_MODEL)
    splits = np.cumsum(sizes)[:-1].tolist()
    (q_nsa, k_cmp, v_cmp, k_slc, v_slc, k_win, v_win, g_nsa,
     q_ml, k_ml, v_ml, o_ml, i_ml, f_ml, q_mem, g_br) = jnp.split(w_in, splits, axis=-1)
    per_group = NSA_REP * 3
    g_blocks = [jnp.pad(g_nsa[:, g * per_group:(g + 1) * per_group], ((0, 0), (0, LANES - per_group)))
                for g in range(NSA_KV_GROUPS)]
    if_blk = jnp.pad(jnp.concatenate([i_ml, f_ml], -1), ((0, 0), (0, LANES - 2 * MLSTM_HEADS)))
    cols = [g_br, q_nsa, q_mem, v_ml, o_ml, q_ml, k_ml, k_cmp, v_cmp,
            k_slc, v_slc, k_win, v_win] + g_blocks + [if_blk]
    w = jnp.concatenate(cols, axis=-1)
    return jnp.pad(w, ((0, 0), (0, IN_PAD - w.shape[1])))


def _mm_kernel(a_ref, b_ref, o_ref):
    o_ref[...] = jnp.dot(a_ref[...].astype(BF16), b_ref[...].astype(BF16),
                         preferred_element_type=F32).astype(o_ref.dtype)


def _matmul(a, b, *, tm, tn, out_dtype=F32, name="matmul"):
    m, k = a.shape
    _, n = b.shape
    assert m % tm == 0 and n % tn == 0
    return pl.pallas_call(
        _mm_kernel,
        out_shape=jax.ShapeDtypeStruct((m, n), out_dtype),
        grid=(m // tm, n // tn),
        in_specs=[pl.BlockSpec((tm, k), lambda i, j: (i, 0)),
                  pl.BlockSpec((k, tn), lambda i, j: (0, j))],
        out_specs=pl.BlockSpec((tm, tn), lambda i, j: (i, j)),
        compiler_params=pltpu.CompilerParams(dimension_semantics=("parallel", "arbitrary"),
                                             vmem_limit_bytes=VMEM_LIMIT),
        name=name,
    )(a, b)


def _rope_tables(pos):
    half = HEAD_DIM // 2
    inv = ROPE_THETA ** (-jnp.arange(half, dtype=F32) / half)
    ang = pos.astype(F32)[:, None] * inv[None, :]
    cos, sin = jnp.cos(ang), jnp.sin(ang)
    return jnp.concatenate([cos, cos], -1), jnp.concatenate([-sin, sin], -1)


def _rope(x, cos, sin_signed):
    return x * cos + pltpu.roll(x, HEAD_DIM // 2, 1) * sin_signed


def _nsa_prep_kernel(kvc_ref, ks_ref, vs_ref, kw_ref, vw_ref, cos_ref, sin_ref,
                     kvc_o, ks_o, vs_o, kw_o, vw_o):
    cos, sin = cos_ref[...], sin_ref[...]
    kvc_o[0] = kvc_ref[0]
    for src, dst in ((ks_ref, ks_o), (kw_ref, kw_o)):
        for g in range(NSA_KV_GROUPS):
            sl = slice(g * HEAD_DIM, (g + 1) * HEAD_DIM)
            dst[0, :, sl] = _rope(src[0, :, sl], cos, sin).astype(BF16)
    vs_o[0] = vs_ref[0].astype(BF16)
    vw_o[0] = vw_ref[0].astype(BF16)


def _nsa_prep(y3, cos, sin, *, ts=512):
    b, s, _ = y3.shape
    kvw = NSA_KV_W

    def col(name, width):
        blk = _SEGS[name][0] // width
        return pl.BlockSpec((1, ts, width), lambda i, j: (i, j, blk))

    tab = pl.BlockSpec((ts, HEAD_DIM), lambda i, j: (j, 0))
    out_kv = pl.BlockSpec((1, ts, kvw), lambda i, j: (i, j, 0))
    return pl.pallas_call(
        _nsa_prep_kernel,
        out_shape=(jax.ShapeDtypeStruct((b, s, 2 * kvw), F32),) + (jax.ShapeDtypeStruct((b, s, kvw), BF16),) * 4,
        grid=(b, s // ts),
        in_specs=[col("kv_cmp", 2 * kvw), col("k_slc", kvw), col("v_slc", kvw), col("k_win", kvw),
                  col("v_win", kvw), tab, tab],
        out_specs=(pl.BlockSpec((1, ts, 2 * kvw), lambda i, j: (i, j, 0)), out_kv, out_kv, out_kv, out_kv),
        compiler_params=pltpu.CompilerParams(dimension_semantics=("parallel", "parallel")),
        name="nsa_prep",
    )(y3, y3, y3, y3, y3, cos, sin)


def _cmp_kernel(u_ref, posa_ref, posb_ref, w1a_ref, w1b_ref, w2_ref, cos_ref, sin_ref, kc_o, vc_o):
    n_rows = u_ref.shape[2]
    for idx, out in ((0, kc_o), (1, vc_o)):
        u = u_ref[0, idx]
        a = jnp.dot((u + posa_ref[idx]).astype(BF16), w1a_ref[idx], preferred_element_type=F32)
        bm = jnp.dot((u + posb_ref[idx]).astype(BF16), w1b_ref[idx], preferred_element_type=F32)
        pre = a + pltpu.roll(bm, n_rows - 1, 0)
        h = jax.nn.gelu(pre).astype(BF16)
        c = jnp.dot(h, w2_ref[idx], preferred_element_type=F32)
        if idx == 0:
            for g in range(NSA_KV_GROUPS):
                sl = slice(g * HEAD_DIM, (g + 1) * HEAD_DIM)
                out[0, :, sl] = _rope(c[:, sl], cos_ref[...], sin_ref[...]).astype(BF16)
        else:
            out[0] = c.astype(BF16)


def _block_diag_groups(w):
    z = jnp.zeros_like(w)
    rows = [jnp.concatenate([w if gi == go else z for go in range(NSA_KV_GROUPS)], -1)
            for gi in range(NSA_KV_GROUPS)]
    return jnp.concatenate(rows, -2)


def _compress(kvc, cmp_pos_k, cmp_k_w1, cmp_k_w2, cmp_pos_v, cmp_v_w1, cmp_v_w2, cos_e, sin_e):
    b, s, _ = kvc.shape
    n_rows = s // CMP_STRIDE
    g, hd = NSA_KV_GROUPS, HEAD_DIM
    row_w = CMP_STRIDE * g * hd
    u = kvc.reshape(b, n_rows, CMP_STRIDE, 2, g * hd).transpose(0, 3, 1, 2, 4).reshape(b, 2, n_rows, row_w)

    def pos_rows(pos):
        p = jnp.broadcast_to(pos.reshape(2, CMP_STRIDE, 1, hd), (2, CMP_STRIDE, g, hd))
        return p.reshape(2, 1, row_w)

    def w1_halves(w1):
        w = w1.reshape(2, CMP_STRIDE, hd, hd)
        return _block_diag_groups(w).reshape(2, row_w, g * hd).astype(BF16)

    pk, pv = pos_rows(cmp_pos_k), pos_rows(cmp_pos_v)
    wk, wv = w1_halves(cmp_k_w1), w1_halves(cmp_v_w1)
    posa = jnp.stack([pk[0], pv[0]])
    posb = jnp.stack([pk[1], pv[1]])
    w1a = jnp.stack([wk[0], wv[0]])
    w1b = jnp.stack([wk[1], wv[1]])
    w2 = jnp.stack([_block_diag_groups(cmp_k_w2), _block_diag_groups(cmp_v_w2)]).astype(BF16)
    full = lambda shp: pl.BlockSpec(shp, lambda i: (0,) * len(shp))
    out_spec = pl.BlockSpec((1, n_rows, g * hd), lambda i: (i, 0, 0))
    return pl.pallas_call(
        _cmp_kernel,
        out_shape=(jax.ShapeDtypeStruct((b, n_rows, g * hd), BF16),) * 2,
        grid=(b,),
        in_specs=[pl.BlockSpec((1, 2, n_rows, row_w), lambda i: (i, 0, 0, 0)),
                  full(posa.shape), full(posb.shape), full(w1a.shape), full(w1b.shape), full(w2.shape),
                  full(cos_e.shape), full(sin_e.shape)],
        out_specs=(out_spec, out_spec),
        compiler_params=pltpu.CompilerParams(dimension_semantics=("parallel",), vmem_limit_bytes=VMEM_LIMIT),
        name="nsa_compress",
    )(u, posa, posb, w1a, w1b, w2, cos_e, sin_e)


NSA_TQ = 256
NSA_TK = 256


def _flash(q4, k_ref, v_ref, kt_lo, kt_hi, bias_fn, acc_ref):
    rows = q4.shape[0]
    tq = rows // NSA_REP
    scale = HEAD_DIM ** -0.5
    acc_ref[...] = jnp.zeros_like(acc_ref)

    def body(kt, carry):
        m, l = carry
        k0 = pl.multiple_of(kt * NSA_TK, NSA_TK)
        k = k_ref[0, pl.ds(k0, NSA_TK), :]
        v = v_ref[0, pl.ds(k0, NSA_TK), :]
        s = lax.dot_general(q4, k, (((1,), (1,)), ((), ())), preferred_element_type=F32) * scale
        s = s.reshape(NSA_REP, tq, NSA_TK) + bias_fn(k0)[None]
        m_new = jnp.maximum(m, s.max(-1, keepdims=True))
        alpha = jnp.exp(m - m_new)
        p = jnp.exp(s - m_new)
        l = alpha * l + p.sum(-1, keepdims=True)
        pv = jnp.dot(p.reshape(rows, NSA_TK).astype(BF16), v, preferred_element_type=F32)
        acc_ref[...] = acc_ref[...] * alpha.reshape(rows, 1) + pv
        return m_new, l

    init = (jnp.full((NSA_REP, tq, 1), NEG_INF, F32), jnp.zeros((NSA_REP, tq, 1), F32))
    _, l = lax.fori_loop(kt_lo, kt_hi, body, init)
    return acc_ref[...] / l.reshape(rows, 1)


def _nsa_kernel(q_ref, gate_ref, kc_ref, vc_ref, ks_ref, vs_ref, kw_ref, vw_ref, cos_ref, sin_ref,
                cover_ref, o_ref, acc_ref, *, n_slc):
    qi = pl.program_id(2)
    tq = NSA_TQ
    rows = NSA_REP * tq
    q0 = qi * tq
    cos, sin = cos_ref[...], sin_ref[...]
    q4 = jnp.concatenate(
        [_rope(q_ref[0, :, r * HEAD_DIM:(r + 1) * HEAD_DIM], cos, sin).astype(BF16) for r in range(NSA_REP)],
        axis=0)
    tpos = q0 + lax.broadcasted_iota(jnp.int32, (tq, LANES), 0)
    lane = lax.broadcasted_iota(jnp.int32, (tq, LANES), 1)

    s = lax.dot_general(q4, kc_ref[0], (((1,), (1,)), ((), ())), preferred_element_type=F32)
    s = (s * HEAD_DIM ** -0.5).reshape(NSA_REP, tq, LANES)
    cmask = (lane * CMP_STRIDE + (CMP_LEN - 1) <= tpos)[None]
    s = jnp.where(cmask, s, NEG_INF)
    e = jnp.exp(s - s.max(-1, keepdims=True))
    p = e / e.sum(-1, keepdims=True) * cmask.astype(F32)
    o_cmp = jnp.dot(p.reshape(rows, LANES).astype(BF16), vc_ref[0], preferred_element_type=F32)

    p_sum = p.sum(0)
    p_hi = p_sum.astype(BF16)
    p_lo = (p_sum - p_hi.astype(F32)).astype(BF16)
    cover = cover_ref[...]
    p_slc = (jnp.dot(p_hi, cover, preferred_element_type=F32)
             + jnp.dot(p_lo, cover, preferred_element_type=F32))
    t_blk = tpos // SLC_LEN
    forced = (lane == 0) | (lane == t_blk) | (lane == t_blk - 1)
    future = lane > t_blk
    score = jnp.where(forced, jnp.inf, jnp.where(future, -jnp.inf, p_slc))
    cnt = jnp.zeros((tq, LANES), jnp.int32)
    for i in range(n_slc):
        col = score[:, i:i + 1]
        ahead = (col > score) | ((col == score) & (lane > i))
        cnt = cnt + ahead.astype(jnp.int32)
    n_sel = min(SLC_TOPK, n_slc)
    chosen = (cnt < n_sel) & (score > -jnp.inf)
    pow2 = jnp.left_shift(1, lane & 15).astype(F32)
    lo = jnp.where(chosen & (lane < 16), pow2, 0.0).sum(-1, keepdims=True)
    hi = jnp.where(chosen & (lane >= 16) & (lane < 32), pow2, 0.0).sum(-1, keepdims=True)
    bits = lo.astype(jnp.int32) | (hi.astype(jnp.int32) << 16)

    tpos_k = q0 + lax.broadcasted_iota(jnp.int32, (tq, NSA_TK), 0)
    kiota = lax.broadcasted_iota(jnp.int32, (tq, NSA_TK), 1)
    bits_b = jnp.broadcast_to(bits, (tq, NSA_TK))

    def slc_bias(k0):
        kpos = k0 + kiota
        picked = (lax.shift_right_logical(bits_b, kpos // SLC_LEN) & 1) == 1
        return jnp.where(picked & (kpos <= tpos_k), 0.0, NEG_INF)

    o_slc = _flash(q4, ks_ref, vs_ref, 0, (q0 + tq) // NSA_TK, slc_bias, acc_ref)

    def win_bias(k0):
        kpos = k0 + kiota
        return jnp.where((kpos <= tpos_k) & (kpos > tpos_k - WINDOW), 0.0, NEG_INF)

    kt_lo = jnp.maximum(q0 - WINDOW, 0) // NSA_TK
    o_win = _flash(q4, kw_ref, vw_ref, kt_lo, (q0 + tq) // NSA_TK, win_bias, acc_ref)

    gate = jax.nn.sigmoid(gate_ref[0])
    for r in range(NSA_REP):
        rs = slice(r * tq, (r + 1) * tq)
        o = (gate[:, 3 * r:3 * r + 1] * o_cmp[rs] + gate[:, 3 * r + 1:3 * r + 2] * o_slc[rs]
             + gate[:, 3 * r + 2:3 * r + 3] * o_win[rs])
        o_ref[0, :, r * HEAD_DIM:(r + 1) * HEAD_DIM] = o.astype(o_ref.dtype)


def _cover_matrix(n_rows, n_slc):
    i = np.arange(n_rows)[:, None]
    j = np.arange(LANES)[None, :]
    c0 = i * CMP_STRIDE
    c1 = c0 + CMP_LEN - 1
    s0 = j * SLC_LEN
    s1 = s0 + SLC_LEN - 1
    ok = (c1 >= s0) & (c0 <= s1) & (i < n_rows - 1) & (j < n_slc)
    return jnp.asarray(ok.astype(np.float32), dtype=BF16)


def _nsa_attention(y3, kc, vc, ks, vs, kw, vw, cos, sin):
    b, s, _ = y3.shape
    g, hd, tq = NSA_KV_GROUPS, HEAD_DIM, NSA_TQ
    n_rows = s // CMP_STRIDE
    assert n_rows == LANES and s // SLC_LEN <= 32 and s % tq == 0
    cover = _cover_matrix(n_rows, s // SLC_LEN)
    qw = NSA_REP * hd
    gate_blk = _SEGS["g_nsa"][0] // LANES
    q_blk = _SEGS["q_nsa"][0] // qw
    seq_g = pl.BlockSpec((1, s, hd), lambda i, j, t: (i, 0, j))
    cmp_g = pl.BlockSpec((1, n_rows, hd), lambda i, j, t: (i, 0, j))
    tab = pl.BlockSpec((tq, hd), lambda i, j, t: (t, 0))
    return pl.pallas_call(
        functools.partial(_nsa_kernel, n_slc=s // SLC_LEN),
        out_shape=jax.ShapeDtypeStruct((b, s, NSA_Q_W), BF16),
        grid=(b, g, s // tq),
        in_specs=[pl.BlockSpec((1, tq, qw), lambda i, j, t: (i, t, q_blk + j)),
                  pl.BlockSpec((1, tq, LANES), lambda i, j, t: (i, t, gate_blk + j)),
                  cmp_g, cmp_g, seq_g, seq_g, seq_g, seq_g, tab, tab,
                  pl.BlockSpec((n_rows, LANES), lambda i, j, t: (0, 0))],
        out_specs=pl.BlockSpec((1, tq, qw), lambda i, j, t: (i, t, j)),
        scratch_shapes=[pltpu.VMEM((NSA_REP * tq, hd), F32)],
        compiler_params=pltpu.CompilerParams(dimension_semantics=("parallel", "parallel", "arbitrary"),
                                             vmem_limit_bytes=VMEM_LIMIT),
        name="nsa_attention",
    )(y3, y3, kc, vc, ks, vs, kw, vw, cos, sin, cover)


FFN_TM = 256


def _ffn_kernel(blk_e_ref, n_used_ref, x_ref, w_ref, wg_ref, wu_ref, wd_ref, o_ref):
    i = pl.program_id(0)

    @pl.when(i < n_used_ref[0])
    def _():
        x = x_ref[...]
        gate = jnp.dot(x, wg_ref[0], preferred_element_type=F32)
        up = jnp.dot(x, wu_ref[0], preferred_element_type=F32)
        h = (jax.nn.silu(gate) * up).astype(BF16)
        o_ref[...] = jnp.dot(h, wd_ref[0], preferred_element_type=F32) * w_ref[...]

    @pl.when(i >= n_used_ref[0])
    def _():
        o_ref[...] = jnp.zeros_like(o_ref)


def _grouped_ffn(xs, row_w, blk_e, n_used, wg, wu, wd):
    p, d = xs.shape
    f = wg.shape[-1]
    n_blocks = p // FFN_TM
    grid_spec = pltpu.PrefetchScalarGridSpec(
        num_scalar_prefetch=2,
        grid=(n_blocks,),
        in_specs=[pl.BlockSpec((FFN_TM, d), lambda i, be, nu: (i, 0)),
                  pl.BlockSpec((FFN_TM, 1), lambda i, be, nu: (i, 0)),
                  pl.BlockSpec((1, d, f), lambda i, be, nu: (be[i], 0, 0)),
                  pl.BlockSpec((1, d, f), lambda i, be, nu: (be[i], 0, 0)),
                  pl.BlockSpec((1, f, d), lambda i, be, nu: (be[i], 0, 0))],
        out_specs=pl.BlockSpec((FFN_TM, d), lambda i, be, nu: (i, 0)),
    )
    return pl.pallas_call(
        _ffn_kernel,
        out_shape=jax.ShapeDtypeStruct((p, d), F32),
        grid_spec=grid_spec,
        compiler_params=pltpu.CompilerParams(dimension_semantics=("arbitrary",), vmem_limit_bytes=VMEM_LIMIT),
        name="grouped_ffn",
    )(blk_e, n_used, xs, row_w, wg, wu, wd)


def _cast_kernel(w_ref, o_ref):
    o_ref[...] = w_ref[...].astype(o_ref.dtype)


def _cast_bf16(w, *, tr):
    e, r, c = w.shape
    assert r % tr == 0
    spec = pl.BlockSpec((1, tr, c), lambda i, j: (i, j, 0))
    return pl.pallas_call(
        _cast_kernel, out_shape=jax.ShapeDtypeStruct(w.shape, BF16), grid=(e, r // tr),
        in_specs=[spec], out_specs=spec,
        compiler_params=pltpu.CompilerParams(dimension_semantics=("parallel", "parallel")),
        name="cast_bf16",
    )(w)


ML_L = 256
ML_HALO = 8
MLSTM_CONV = 4


def _mlstm_kernel(qk_ref, v_ref, o_ref, if_ref, convw_ref, convb_ref, ifb_ref, ng_ref, tril_ref,
                  out_ref, c_ref, m_ref, halo_ref):
    L, NH, DK, DV = ML_L, MLSTM_HEADS, MLSTM_QK_DIM, MLSTM_V_DIM

    @pl.when(pl.program_id(1) == 0)
    def _():
        c_ref[...] = jnp.zeros_like(c_ref)
        m_ref[...] = jnp.zeros_like(m_ref)
        halo_ref[...] = jnp.zeros_like(halo_ref)

    x = qk_ref[0]
    cat = jnp.concatenate([halo_ref[...], x], axis=0)
    halo_ref[...] = x[L - ML_HALO:, :]
    y = convb_ref[...]
    for j in range(MLSTM_CONV):
        lo = ML_HALO - (MLSTM_CONV - 1) + j
        y = y + convw_ref[j:j + 1, :] * cat[lo:lo + L, :]
    qk = y * jax.nn.sigmoid(y)

    lane = lax.broadcasted_iota(jnp.int32, (L, LANES), 1)
    z = if_ref[0] + ifb_ref[...]
    lg = jnp.where(lane < NH, z, jnp.minimum(z, 0.0) - jnp.log1p(jnp.exp(-jnp.abs(z))))
    hi = lg.astype(BF16)
    r1 = lg - hi.astype(F32)
    mid = r1.astype(BF16)
    lo3 = (r1 - mid.astype(F32)).astype(BF16)
    tril = tril_ref[...]
    bcum = (jnp.dot(tril, hi, preferred_element_type=F32) + jnp.dot(tril, mid, preferred_element_type=F32)
            + jnp.dot(tril, lo3, preferred_element_type=F32))

    row = lax.broadcasted_iota(jnp.int32, (L, L), 0)
    colm = lax.broadcasted_iota(jnp.int32, (L, L), 1)
    tri = row >= colm
    ones_col = (lane == 0).astype(BF16)
    for h in range(NH):
        li = lg[:, h:h + 1]
        b = bcum[:, NH + h:NH + h + 1]
        m_prev = m_ref[h:h + 1, 0:1]
        a_col = li - b
        a_row = jnp.transpose(jnp.broadcast_to(a_col, (L, LANES)))[0:1, :]
        dmat = jnp.where(tri, b + a_row, -jnp.inf)
        m_inter = b + m_prev
        m_t = jnp.maximum(m_inter, dmat.max(-1, keepdims=True))
        q = qk[:, h * DK:(h + 1) * DK].astype(BF16)
        k = qk[:, NH * DK + h * DK:NH * DK + (h + 1) * DK] * DK ** -0.5
        s = lax.dot_general(q, k.astype(BF16), (((1,), (1,)), ((), ())), preferred_element_type=F32)
        wqk = (jnp.exp(dmat - m_t) * s).astype(BF16)
        inter = jnp.exp(m_inter - m_t)
        v_ext = jnp.concatenate([v_ref[0, :, h * DV:(h + 1) * DV].astype(BF16), ones_col], axis=1)
        c_old = c_ref[h]
        nd = (jnp.dot(wqk, v_ext, preferred_element_type=F32)
              + inter * jnp.dot(q, c_old.astype(BF16), preferred_element_type=F32))
        num = nd[:, :DV]
        den = nd[:, DV:DV + 1]
        hh = num / jnp.maximum(jnp.abs(den), jnp.exp(-m_t))
        b_last = b[L - 1:L, :]
        g = b_last - b + li
        m_new = jnp.maximum(b_last + m_prev, g.max(0, keepdims=True))
        decay = jnp.exp(b_last + m_prev - m_new)
        w_s = jnp.exp(g - m_new)
        kw_t = jnp.transpose(k * w_s).astype(BF16)
        c_ref[h] = decay * c_old + jnp.dot(kw_t, v_ext, preferred_element_type=F32)
        m_ref[h:h + 1, :] = jnp.broadcast_to(m_new, (1, LANES))
        mu = hh.mean(-1, keepdims=True)
        var = jnp.mean(jnp.square(hh - mu), -1, keepdims=True)
        hn = (hh - mu) * lax.rsqrt(var + LN_EPS) * ng_ref[:, h * DV:(h + 1) * DV]
        out_ref[0, :, h * DV:(h + 1) * DV] = (
            jax.nn.sigmoid(o_ref[0, :, h * DV:(h + 1) * DV]) * hn).astype(out_ref.dtype)


def _mlstm(y3, conv_w, conv_b, if_bias, norm_g):
    b, s, _ = y3.shape
    L, NH = ML_L, MLSTM_HEADS
    assert s % L == 0

    def col(name, width):
        blk = _SEGS[name][0] // width
        return pl.BlockSpec((1, L, width), lambda i, j: (i, j, blk))

    full = lambda shp: pl.BlockSpec(shp, lambda i, j: (0,) * len(shp))
    ifb = jnp.pad(if_bias.reshape(1, 2 * NH), ((0, 0), (0, LANES - 2 * NH)))
    tril = jnp.asarray(np.tril(np.ones((L, L), np.float32)), dtype=BF16)
    conv_b = conv_b.reshape(1, -1)
    norm_g = norm_g.reshape(1, -1)
    return pl.pallas_call(
        _mlstm_kernel,
        out_shape=jax.ShapeDtypeStruct((b, s, ML_V_W), BF16),
        grid=(b, s // L),
        in_specs=[col("qk_ml", 2 * ML_QK_W), col("v_ml", ML_V_W), col("o_ml", ML_V_W), col("if_ml", LANES),
                  full(conv_w.shape), full(conv_b.shape), full(ifb.shape), full(norm_g.shape), full(tril.shape)],
        out_specs=pl.BlockSpec((1, L, ML_V_W), lambda i, j: (i, j, 0)),
        scratch_shapes=[pltpu.VMEM((NH, MLSTM_QK_DIM, MLSTM_V_DIM + LANES), F32),
                        pltpu.VMEM((8, LANES), F32),
                        pltpu.VMEM((ML_HALO, 2 * ML_QK_W), F32)],
        compiler_params=pltpu.CompilerParams(dimension_semantics=("parallel", "arbitrary"),
                                             vmem_limit_bytes=VMEM_LIMIT),
        name="mlstm",
    )(y3, y3, y3, y3, conv_w, conv_b, ifb, norm_g, tril)


MEM_TQ = 512


def _mem_attn_kernel(q_ref, kv_ref, o_ref):
    hd = MEM_HEAD_DIM
    for h in range(MEM_HEADS):
        q = q_ref[0, :, h * hd:(h + 1) * hd].astype(BF16)
        k = kv_ref[0, :, h * hd:(h + 1) * hd]
        v = kv_ref[0, :, MEM_W + h * hd:MEM_W + (h + 1) * hd]
        s = lax.dot_general(q, k, (((1,), (1,)), ((), ())), preferred_element_type=F32) * hd ** -0.5
        e = jnp.exp(s - s.max(-1, keepdims=True))
        p = (e / e.sum(-1, keepdims=True)).astype(BF16)
        o_ref[0, :, h * hd:(h + 1) * hd] = jnp.dot(p, v, preferred_element_type=F32).astype(o_ref.dtype)


def _mem_attention(y3, kv_mem):
    b, s, _ = y3.shape
    m = kv_mem.shape[1]
    q_blk = _SEGS["q_mem"][0] // MEM_W
    return pl.pallas_call(
        _mem_attn_kernel,
        out_shape=jax.ShapeDtypeStruct((b, s, MEM_W), BF16),
        grid=(b, s // MEM_TQ),
        in_specs=[pl.BlockSpec((1, MEM_TQ, MEM_W), lambda i, j: (i, j, q_blk)),
                  pl.BlockSpec((1, m, 2 * MEM_W), lambda i, j: (i, 0, 0))],
        out_specs=pl.BlockSpec((1, MEM_TQ, MEM_W), lambda i, j: (i, j, 0)),
        compiler_params=pltpu.CompilerParams(dimension_semantics=("parallel", "parallel"),
                                             vmem_limit_bytes=VMEM_LIMIT),
        name="mem_attention",
    )(y3, kv_mem)


MERGE_TM = 256


def _merge_kernel(yn_ref, ym_ref, yc_ref, g0_ref, g1_ref, g2_ref, x_ref, wb_ref, wo_ref, lg_ref, lb_ref,
                  x1_ref, x1b_ref):
    merged = None
    for i, (y_ref, g_ref) in enumerate(((yn_ref, g0_ref), (ym_ref, g1_ref), (yc_ref, g2_ref))):
        t = jax.nn.sigmoid(g_ref[...]) * jnp.dot(y_ref[...], wb_ref[i], preferred_element_type=F32)
        merged = t if merged is None else merged + t
    z = DN_ALPHA * x_ref[...] + jnp.dot(merged.astype(BF16), wo_ref[...], preferred_element_type=F32)
    mu = z.mean(-1, keepdims=True)
    var = jnp.mean(jnp.square(z - mu), -1, keepdims=True)
    x1 = (z - mu) * lax.rsqrt(var + LN_EPS) * lg_ref[...] + lb_ref[...]
    x1_ref[...] = x1
    x1b_ref[...] = x1.astype(BF16)


def _merge_ln1(y_nsa, y_ml, y_mem, y2, xt, wb, wo, ln_g, ln_b):
    t, d = xt.shape
    tm = MERGE_TM
    bw = y_nsa.shape[1]
    resident = pl.Buffered(1)
    row = lambda w: pl.BlockSpec((tm, w), lambda i: (i, 0))
    gate = lambda k: pl.BlockSpec((tm, d), lambda i: (i, k))
    return pl.pallas_call(
        _merge_kernel,
        out_shape=(jax.ShapeDtypeStruct((t, d), F32), jax.ShapeDtypeStruct((t, d), BF16)),
        grid=(t // tm,),
        in_specs=[row(bw), row(bw), row(bw), gate(0), gate(1), gate(2), row(d),
                  pl.BlockSpec((N_BRANCH, bw, d), lambda i: (0, 0, 0), pipeline_mode=resident),
                  pl.BlockSpec((d, d), lambda i: (0, 0), pipeline_mode=resident),
                  pl.BlockSpec((1, d), lambda i: (0, 0)), pl.BlockSpec((1, d), lambda i: (0, 0))],
        out_specs=(row(d), row(d)),
        compiler_params=pltpu.CompilerParams(dimension_semantics=("parallel",), vmem_limit_bytes=VMEM_LIMIT),
        name="merge_ln1",
    )(y_nsa, y_ml, y_mem, y2, y2, y2, xt, wb, wo, ln_g.reshape(1, d), ln_b.reshape(1, d))


ROUTER_TM = 256


def _router_kernel(x_ref, w_ref, bias_ref, tril_ref, idx_ref, wt_ref, rank_ref, cnt_ref, run_ref):
    tm = ROUTER_TM

    @pl.when(pl.program_id(0) == 0)
    def _():
        run_ref[...] = jnp.zeros_like(run_ref)

    x = x_ref[...]
    w = w_ref[...]
    x_hi = x.astype(BF16)
    x_lo = (x - x_hi.astype(F32)).astype(BF16)
    w_hi = w.astype(BF16)
    w_lo = (w - w_hi.astype(F32)).astype(BF16)
    logits = (jnp.dot(x_hi, w_hi, preferred_element_type=F32) + jnp.dot(x_lo, w_hi, preferred_element_type=F32)
              + jnp.dot(x_hi, w_lo, preferred_element_type=F32))
    lane = lax.broadcasted_iota(jnp.int32, (tm, LANES), 1)
    real = lane < N_EXPERTS
    aff = jax.nn.sigmoid(logits)
    score = jnp.where(real, aff + bias_ref[...], -jnp.inf)
    cnt = jnp.zeros((tm, LANES), jnp.int32)
    for i in range(N_EXPERTS):
        col = score[:, i:i + 1]
        cnt = cnt + ((col > score) | ((col == score) & (lane > i))).astype(jnp.int32)
    chosen = (cnt < TOP_K) & real
    aff_c = jnp.where(chosen, aff, 0.0)
    wd = aff_c / aff_c.sum(-1, keepdims=True) * ROUTED_SCALE
    chosen_b = chosen.astype(BF16)
    run = run_ref[0:1, :]
    excl = jnp.dot(tril_ref[...], chosen_b, preferred_element_type=F32) + run
    run_new = run + chosen.astype(F32).sum(0, keepdims=True)
    run_ref[...] = jnp.broadcast_to(run_new, run_ref.shape)
    cnt_ref[...] = jnp.broadcast_to(run_new, cnt_ref.shape)

    idx_out = jnp.zeros((tm, LANES), jnp.int32)
    wt_out = jnp.zeros((tm, LANES), F32)
    rank_out = jnp.zeros((tm, LANES), F32)
    prev = jnp.full((tm, 1), -1, jnp.int32)
    for k in range(TOP_K):
        cur = jnp.where(chosen & (lane > prev), lane, LANES).min(-1, keepdims=True)
        here = lane == cur
        idx_out = jnp.where(lane == k, cur, idx_out)
        wt_out = jnp.where(lane == k, jnp.where(here, wd, 0.0).sum(-1, keepdims=True), wt_out)
        rank_out = jnp.where(lane == k, jnp.where(here, excl, 0.0).sum(-1, keepdims=True), rank_out)
        prev = cur
    idx_ref[...] = idx_out
    wt_ref[...] = wt_out
    rank_ref[...] = rank_out.astype(jnp.int32)


def _router(x1, router_w, router_bias):
    t, d = x1.shape
    tm = ROUTER_TM
    e = router_w.shape[1]
    w = jnp.pad(router_w, ((0, 0), (0, LANES - e)))
    bias = jnp.pad(router_bias.reshape(1, e), ((0, 0), (0, LANES - e)))
    tril = jnp.asarray(np.tril(np.ones((tm, tm), np.float32), -1), dtype=BF16)
    out = pl.BlockSpec((tm, LANES), lambda i: (i, 0))
    const = lambda shp: pl.BlockSpec(shp, lambda i: (0, 0))
    idx, wt, rank, cnt = pl.pallas_call(
        _router_kernel,
        out_shape=(jax.ShapeDtypeStruct((t, LANES), jnp.int32), jax.ShapeDtypeStruct((t, LANES), F32),
                   jax.ShapeDtypeStruct((t, LANES), jnp.int32), jax.ShapeDtypeStruct((8, LANES), F32)),
        grid=(t // tm,),
        in_specs=[pl.BlockSpec((tm, d), lambda i: (i, 0)), const(w.shape), const(bias.shape), const(tril.shape)],
        out_specs=(out, out, out, const((8, LANES))),
        scratch_shapes=[pltpu.VMEM((8, LANES), F32)],
        compiler_params=pltpu.CompilerParams(dimension_semantics=("arbitrary",), vmem_limit_bytes=VMEM_LIMIT),
        name="router",
    )(x1, w, bias, tril)
    return idx[:, :TOP_K], wt[:, :TOP_K], rank[:, :TOP_K], cnt[0, :e].astype(jnp.int32)


def _moe_layout(idx, wts, rank, counts):
    t = idx.shape[0]
    e = N_EXPERTS
    m = t * TOP_K
    padded = (counts + FFN_TM - 1) // FFN_TM * FFN_TM
    pad_end = jnp.cumsum(padded)
    pad_start = pad_end - padded
    start = jnp.cumsum(counts) - counts
    n_blocks = m // FFN_TM + e
    blk_e = jnp.minimum(jnp.searchsorted(pad_end, jnp.arange(n_blocks) * FFN_TM, side="right"), e - 1)
    blk_e = blk_e.astype(jnp.int32)
    n_used = (pad_end[-1] // FFN_TM).astype(jnp.int32).reshape(1)
    pos = pad_start[idx] + rank
    order = jnp.argsort(idx.reshape(m))
    row_e = jnp.repeat(blk_e, FFN_TM)
    j = jnp.arange(n_blocks * FFN_TM) - pad_start[row_e]
    valid = j < counts[row_e]
    flat = order[jnp.clip(start[row_e] + j, 0, m - 1)]
    row_tok = jnp.where(valid, flat // TOP_K, 0).astype(jnp.int32)
    row_w = jnp.where(valid, wts.reshape(m)[flat], 0.0)
    return row_tok, row_w, blk_e, n_used, pos


def _combine_kernel(x_ref, r_ref, s_ref, g_ref, b_ref, o_ref):
    z = DN_ALPHA * x_ref[...] + (r_ref[...] + s_ref[...])
    mu = z.mean(-1, keepdims=True)
    var = jnp.mean(jnp.square(z - mu), -1, keepdims=True)
    o_ref[...] = (z - mu) * lax.rsqrt(var + LN_EPS) * g_ref[...] + b_ref[...]


def _combine_ln2(x1, routed, shared, ln_g, ln_b, *, tm=512):
    t, d = x1.shape
    row = pl.BlockSpec((tm, d), lambda i: (i, 0))
    vec = pl.BlockSpec((1, d), lambda i: (0, 0))
    return pl.pallas_call(
        _combine_kernel, out_shape=jax.ShapeDtypeStruct((t, d), F32), grid=(t // tm,),
        in_specs=[row, row, row, vec, vec], out_specs=row,
        compiler_params=pltpu.CompilerParams(dimension_semantics=("parallel",), vmem_limit_bytes=VMEM_LIMIT),
        name="combine_ln2",
    )(x1, routed, shared, ln_g.reshape(1, d), ln_b.reshape(1, d))


def _layer(x, mem, w_in, mlstm_conv_w, mlstm_conv_b, mlstm_if_bias, mlstm_norm_g,
           cmp_pos_k, cmp_k_w1, cmp_k_w2, cmp_pos_v, cmp_v_w1, cmp_v_w2,
           w_mem_kv, w_branch, w_o, ln1_g, ln1_b, router_w, router_bias,
           w_e_gate, w_e_up, w_e_down, w_s_gate, w_s_up, w_s_down, ln2_g, ln2_b):
    B, S, D = x.shape
    T = B * S
    xt = x.reshape(T, D)
    y2 = _matmul(xt.astype(BF16), _pad_in_proj(w_in).astype(BF16), tm=1024, tn=IN_TN, name="in_proj")
    y3 = y2.reshape(B, S, IN_PAD)

    cos, sin = _rope_tables(jnp.arange(S))
    cos_e, sin_e = _rope_tables(jnp.arange(S // CMP_STRIDE) * CMP_STRIDE + CMP_LEN - 1)
    kvc, ks, vs, kw, vw = _nsa_prep(y3, cos, sin)
    kc, vc = _compress(kvc, cmp_pos_k, cmp_k_w1, cmp_k_w2, cmp_pos_v, cmp_v_w1, cmp_v_w2, cos_e, sin_e)
    y_nsa = _nsa_attention(y3, kc, vc, ks, vs, kw, vw, cos, sin)

    y_ml = _mlstm(y3, mlstm_conv_w, mlstm_conv_b, mlstm_if_bias, mlstm_norm_g)

    m_len = mem.shape[1]
    kv_mem = _matmul(mem.reshape(B * m_len, D).astype(BF16), w_mem_kv.astype(BF16), tm=1024, tn=512,
                     out_dtype=BF16, name="mem_kv").reshape(B, m_len, 2 * MEM_W)
    y_mem = _mem_attention(y3, kv_mem)

    x1, x1b = _merge_ln1(y_nsa.reshape(T, -1), y_ml.reshape(T, -1), y_mem.reshape(T, -1), y2, xt,
                         w_branch.astype(BF16), w_o.astype(BF16), ln1_g, ln1_b)

    idx, wts, rank, counts = _router(x1, router_w, router_bias)
    row_tok, row_w, blk_e, n_used, pos = _moe_layout(idx, wts, rank, counts)
    out = _grouped_ffn(x1b[row_tok], row_w[:, None], blk_e, n_used,
                       _cast_bf16(w_e_gate, tr=512), _cast_bf16(w_e_up, tr=512), _cast_bf16(w_e_down, tr=352))
    routed = out[pos].sum(1)
    n_tb = T // FFN_TM
    shared = _grouped_ffn(x1b, jnp.ones((T, 1), F32), jnp.zeros((n_tb,), jnp.int32),
                          jnp.full((1,), n_tb, jnp.int32), w_s_gate.astype(BF16)[None],
                          w_s_up.astype(BF16)[None], w_s_down.astype(BF16)[None])
    return _combine_ln2(x1, routed, shared, ln2_g, ln2_b).reshape(B, S, D)


def kernel(x, mem, w_in, mlstm_conv_w, mlstm_conv_b, mlstm_if_bias, mlstm_norm_g, cmp_pos_k, cmp_k_w1, cmp_k_w2, cmp_pos_v, cmp_v_w1, cmp_v_w2, w_mem_kv, w_branch, w_o, ln1_g, ln1_b, router_w, router_bias, w_e_gate, w_e_up, w_e_down, w_s_gate, w_s_up, w_s_down, ln2_g, ln2_b):
    for layer in range(DEPTH):
        x = _layer(x, mem, w_in[layer], mlstm_conv_w[layer], mlstm_conv_b[layer],
                   mlstm_if_bias[layer], mlstm_norm_g[layer],
                   cmp_pos_k[layer], cmp_k_w1[layer], cmp_k_w2[layer],
                   cmp_pos_v[layer], cmp_v_w1[layer], cmp_v_w2[layer],
                   w_mem_kv[layer], w_branch[layer], w_o[layer], ln1_g[layer], ln1_b[layer],
                   router_w[layer], router_bias[layer], w_e_gate[layer], w_e_up[layer],
                   w_e_down[layer], w_s_gate[layer], w_s_up[layer], w_s_down[layer],
                   ln2_g[layer], ln2_b[layer])
    return x
```

```python
import functools

import jax
import jax.numpy as jnp
import numpy as np
from jax import lax
from jax.experimental import pallas as pl
from jax.experimental.pallas import tpu as pltpu

F32 = jnp.float32
BF16 = jnp.bfloat16

D_MODEL = 2048
HEAD_DIM = 128
NSA_HEADS = 8
NSA_KV_GROUPS = 2
NSA_REP = NSA_HEADS // NSA_KV_GROUPS
CMP_LEN = 32
CMP_STRIDE = 16
SLC_LEN = 64
SLC_TOPK = 16
WINDOW = 512
MLSTM_HEADS = 4
MLSTM_QK_DIM = 128
MLSTM_V_DIM = 256
MLSTM_CHUNK = 64
MEM_HEADS = 4
MEM_HEAD_DIM = 256
N_EXPERTS = 64
TOP_K = 6
EXPERT_DIM = 1408
ROUTED_SCALE = 2.5
ROPE_THETA = 10000.0
LN_EPS = 1e-5
NEG_INF = -1e30
DEPTH = 1
DN_ALPHA = (2 * DEPTH) ** 0.25

NSA_Q_W = NSA_HEADS * HEAD_DIM
NSA_KV_W = NSA_KV_GROUPS * HEAD_DIM
ML_QK_W = MLSTM_HEADS * MLSTM_QK_DIM
ML_V_W = MLSTM_HEADS * MLSTM_V_DIM
MEM_W = MEM_HEADS * MEM_HEAD_DIM
N_BRANCH = 3

LANES = 128
VMEM_LIMIT = 56 * 1024 * 1024

_SEGS = {}
_off = 0
for _name, _w, _blk in (("g_br", N_BRANCH * D_MODEL, D_MODEL), ("q_nsa", NSA_Q_W, NSA_REP * HEAD_DIM),
                        ("q_mem", MEM_W, MEM_W), ("v_ml", ML_V_W, ML_V_W), ("o_ml", ML_V_W, ML_V_W),
                        ("qk_ml", 2 * ML_QK_W, 2 * ML_QK_W), ("kv_cmp", 2 * NSA_KV_W, 2 * NSA_KV_W),
                        ("k_slc", NSA_KV_W, NSA_KV_W), ("v_slc", NSA_KV_W, NSA_KV_W),
                        ("k_win", NSA_KV_W, NSA_KV_W), ("v_win", NSA_KV_W, NSA_KV_W),
                        ("g_nsa", NSA_KV_GROUPS * LANES, LANES), ("if_ml", LANES, LANES)):
    assert _off % _blk == 0
    _SEGS[_name] = (_off, _w)
    _off += _w
IN_TN = 512
IN_PAD = -(-_off // IN_TN) * IN_TN


def _pad_in_proj(w_in):
    sizes = (NSA_Q_W,) + (NSA_KV_W,) * 6 + (NSA_HEADS * 3, ML_QK_W, ML_QK_W, ML_V_W, ML_V_W,
                                            MLSTM_HEADS, MLSTM_HEADS, MEM_W, N_BRANCH * D_MODEL)
    splits = np.cumsum(sizes)[:-1].tolist()
    (q_nsa, k_cmp, v_cmp, k_slc, v_slc, k_win, v_win, g_nsa,
     q_ml, k_ml, v_ml, o_ml, i_ml, f_ml, q_mem, g_br) = jnp.split(w_in, splits, axis=-1)
    per_group = NSA_REP * 3
    g_blocks = [jnp.pad(g_nsa[:, g * per_group:(g + 1) * per_group], ((0, 0), (0, LANES - per_group)))
                for g in range(NSA_KV_GROUPS)]
    if_blk = jnp.pad(jnp.concatenate([i_ml, f_ml], -1), ((0, 0), (0, LANES - 2 * MLSTM_HEADS)))
    cols = [g_br, q_nsa, q_mem, v_ml, o_ml, q_ml, k_ml, k_cmp, v_cmp,
            k_slc, v_slc, k_win, v_win] + g_blocks + [if_blk]
    w = jnp.concatenate(cols, axis=-1)
    return jnp.pad(w, ((0, 0), (0, IN_PAD - w.shape[1])))


def _mm_kernel(a_ref, b_ref, o_ref):
    o_ref[...] = jnp.dot(a_ref[...].astype(BF16), b_ref[...].astype(BF16),
                         preferred_element_type=F32).astype(o_ref.dtype)


def _matmul(a, b, *, tm, tn, out_dtype=F32, name="matmul"):
    m, k = a.shape
    _, n = b.shape
    assert m % tm == 0 and n % tn == 0
    return pl.pallas_call(
        _mm_kernel,
        out_shape=jax.ShapeDtypeStruct((m, n), out_dtype),
        grid=(m // tm, n // tn),
        in_specs=[pl.BlockSpec((tm, k), lambda i, j: (i, 0)),
                  pl.BlockSpec((k, tn), lambda i, j: (0, j))],
        out_specs=pl.BlockSpec((tm, tn), lambda i, j: (i, j)),
        compiler_params=pltpu.CompilerParams(dimension_semantics=("parallel", "arbitrary"),
                                             vmem_limit_bytes=VMEM_LIMIT),
        name=name,
    )(a, b)


def _rope_tables(pos):
    half = HEAD_DIM // 2
    inv = ROPE_THETA ** (-jnp.arange(half, dtype=F32) / half)
    ang = pos.astype(F32)[:, None] * inv[None, :]
    cos, sin = jnp.cos(ang), jnp.sin(ang)
    return jnp.concatenate([cos, cos], -1), jnp.concatenate([-sin, sin], -1)


def _rope(x, cos, sin_signed):
    return x * cos + pltpu.roll(x, HEAD_DIM // 2, 1) * sin_signed


def _nsa_prep_kernel(kvc_ref, ks_ref, vs_ref, kw_ref, vw_ref, cos_ref, sin_ref,
                     kvc_o, ks_o, vs_o, kw_o, vw_o):
    cos, sin = cos_ref[...], sin_ref[...]
    kvc_o[0] = kvc_ref[0]
    for src, dst in ((ks_ref, ks_o), (kw_ref, kw_o)):
        for g in range(NSA_KV_GROUPS):
            sl = slice(g * HEAD_DIM, (g + 1) * HEAD_DIM)
            dst[0, :, sl] = _rope(src[0, :, sl], cos, sin).astype(BF16)
    vs_o[0] = vs_ref[0].astype(BF16)
    vw_o[0] = vw_ref[0].astype(BF16)


def _nsa_prep(y3, cos, sin, *, ts=512):
    b, s, _ = y3.shape
    kvw = NSA_KV_W

    def col(name, width):
        blk = _SEGS[name][0] // width
        return pl.BlockSpec((1, ts, width), lambda i, j: (i, j, blk))

    tab = pl.BlockSpec((ts, HEAD_DIM), lambda i, j: (j, 0))
    out_kv = pl.BlockSpec((1, ts, kvw), lambda i, j: (i, j, 0))
    return pl.pallas_call(
        _nsa_prep_kernel,
        out_shape=(jax.ShapeDtypeStruct((b, s, 2 * kvw), F32),) + (jax.ShapeDtypeStruct((b, s, kvw), BF16),) * 4,
        grid=(b, s // ts),
        in_specs=[col("kv_cmp", 2 * kvw), col("k_slc", kvw), col("v_slc", kvw), col("k_win", kvw),
                  col("v_win", kvw), tab, tab],
        out_specs=(pl.BlockSpec((1, ts, 2 * kvw), lambda i, j: (i, j, 0)), out_kv, out_kv, out_kv, out_kv),
        compiler_params=pltpu.CompilerParams(dimension_semantics=("parallel", "parallel")),
        name="nsa_prep",
    )(y3, y3, y3, y3, y3, cos, sin)


def _cmp_kernel(u_ref, posa_ref, posb_ref, w1a_ref, w1b_ref, w2_ref, cos_ref, sin_ref, kc_o, vc_o):
    n_rows = u_ref.shape[2]
    for idx, out in ((0, kc_o), (1, vc_o)):
        u = u_ref[0, idx]
        a = jnp.dot((u + posa_ref[idx]).astype(BF16), w1a_ref[idx], preferred_element_type=F32)
        bm = jnp.dot((u + posb_ref[idx]).astype(BF16), w1b_ref[idx], preferred_element_type=F32)
        pre = a + pltpu.roll(bm, n_rows - 1, 0)
        h = jax.nn.gelu(pre).astype(BF16)
        c = jnp.dot(h, w2_ref[idx], preferred_element_type=F32)
        if idx == 0:
            for g in range(NSA_KV_GROUPS):
                sl = slice(g * HEAD_DIM, (g + 1) * HEAD_DIM)
                out[0, :, sl] = _rope(c[:, sl], cos_ref[...], sin_ref[...]).astype(BF16)
        else:
            out[0] = c.astype(BF16)


def _block_diag_groups(w):
    z = jnp.zeros_like(w)
    rows = [jnp.concatenate([w if gi == go else z for go in range(NSA_KV_GROUPS)], -1)
            for gi in range(NSA_KV_GROUPS)]
    return jnp.concatenate(rows, -2)


def _compress(kvc, cmp_pos_k, cmp_k_w1, cmp_k_w2, cmp_pos_v, cmp_v_w1, cmp_v_w2, cos_e, sin_e):
    b, s, _ = kvc.shape
    n_rows = s // CMP_STRIDE
    g, hd = NSA_KV_GROUPS, HEAD_DIM
    row_w = CMP_STRIDE * g * hd
    u = kvc.reshape(b, n_rows, CMP_STRIDE, 2, g * hd).transpose(0, 3, 1, 2, 4).reshape(b, 2, n_rows, row_w)

    def pos_rows(pos):
        p = jnp.broadcast_to(pos.reshape(2, CMP_STRIDE, 1, hd), (2, CMP_STRIDE, g, hd))
        return p.reshape(2, 1, row_w)

    def w1_halves(w1):
        w = w1.reshape(2, CMP_STRIDE, hd, hd)
        return _block_diag_groups(w).reshape(2, row_w, g * hd).astype(BF16)

    pk, pv = pos_rows(cmp_pos_k), pos_rows(cmp_pos_v)
    wk, wv = w1_halves(cmp_k_w1), w1_halves(cmp_v_w1)
    posa = jnp.stack([pk[0], pv[0]])
    posb = jnp.stack([pk[1], pv[1]])
    w1a = jnp.stack([wk[0], wv[0]])
    w1b = jnp.stack([wk[1], wv[1]])
    w2 = jnp.stack([_block_diag_groups(cmp_k_w2), _block_diag_groups(cmp_v_w2)]).astype(BF16)
    full = lambda shp: pl.BlockSpec(shp, lambda i: (0,) * len(shp))
    out_spec = pl.BlockSpec((1, n_rows, g * hd), lambda i: (i, 0, 0))
    return pl.pallas_call(
        _cmp_kernel,
        out_shape=(jax.ShapeDtypeStruct((b, n_rows, g * hd), BF16),) * 2,
        grid=(b,),
        in_specs=[pl.BlockSpec((1, 2, n_rows, row_w), lambda i: (i, 0, 0, 0)),
                  full(posa.shape), full(posb.shape), full(w1a.shape), full(w1b.shape), full(w2.shape),
                  full(cos_e.shape), full(sin_e.shape)],
        out_specs=(out_spec, out_spec),
        compiler_params=pltpu.CompilerParams(dimension_semantics=("parallel",), vmem_limit_bytes=VMEM_LIMIT),
        name="nsa_compress",
    )(u, posa, posb, w1a, w1b, w2, cos_e, sin_e)


NSA_TQ = 256
NSA_TK = 256


def _flash(q4, k_ref, v_ref, kt_lo, kt_hi, bias_fn, acc_ref):
    rows = q4.shape[0]
    tq = rows // NSA_REP
    scale = HEAD_DIM ** -0.5
    acc_ref[...] = jnp.zeros_like(acc_ref)

    def body(kt, carry):
        m, l = carry
        k0 = pl.multiple_of(kt * NSA_TK, NSA_TK)
        k = k_ref[0, pl.ds(k0, NSA_TK), :]
        v = v_ref[0, pl.ds(k0, NSA_TK), :]
        s = lax.dot_general(q4, k, (((1,), (1,)), ((), ())), preferred_element_type=F32) * scale
        s = s.reshape(NSA_REP, tq, NSA_TK) + bias_fn(k0)[None]
        m_new = jnp.maximum(m, s.max(-1, keepdims=True))
        alpha = jnp.exp(m - m_new)
        p = jnp.exp(s - m_new)
        l = alpha * l + p.sum(-1, keepdims=True)
        pv = jnp.dot(p.reshape(rows, NSA_TK).astype(BF16), v, preferred_element_type=F32)
        acc_ref[...] = acc_ref[...] * alpha.reshape(rows, 1) + pv
        return m_new, l

    init = (jnp.full((NSA_REP, tq, 1), NEG_INF, F32), jnp.zeros((NSA_REP, tq, 1), F32))
    _, l = lax.fori_loop(kt_lo, kt_hi, body, init)
    return acc_ref[...] / l.reshape(rows, 1)


def _nsa_kernel(q_ref, gate_ref, kc_ref, vc_ref, ks_ref, vs_ref, kw_ref, vw_ref, cos_ref, sin_ref,
                cover_ref, o_ref, acc_ref, *, n_slc):
    qi = pl.program_id(2)
    tq = NSA_TQ
    rows = NSA_REP * tq
    q0 = qi * tq
    cos, sin = cos_ref[...], sin_ref[...]
    q4 = jnp.concatenate(
        [_rope(q_ref[0, :, r * HEAD_DIM:(r + 1) * HEAD_DIM], cos, sin).astype(BF16) for r in range(NSA_REP)],
        axis=0)
    tpos = q0 + lax.broadcasted_iota(jnp.int32, (tq, LANES), 0)
    lane = lax.broadcasted_iota(jnp.int32, (tq, LANES), 1)

    s = lax.dot_general(q4, kc_ref[0], (((1,), (1,)), ((), ())), preferred_element_type=F32)
    s = (s * HEAD_DIM ** -0.5).reshape(NSA_REP, tq, LANES)
    cmask = (lane * CMP_STRIDE + (CMP_LEN - 1) <= tpos)[None]
    s = jnp.where(cmask, s, NEG_INF)
    e = jnp.exp(s - s.max(-1, keepdims=True))
    p = e / e.sum(-1, keepdims=True) * cmask.astype(F32)
    o_cmp = jnp.dot(p.reshape(rows, LANES).astype(BF16), vc_ref[0], preferred_element_type=F32)

    p_sum = p.sum(0)
    p_hi = p_sum.astype(BF16)
    p_lo = (p_sum - p_hi.astype(F32)).astype(BF16)
    cover = cover_ref[...]
    p_slc = (jnp.dot(p_hi, cover, preferred_element_type=F32)
             + jnp.dot(p_lo, cover, preferred_element_type=F32))
    t_blk = tpos // SLC_LEN
    forced = (lane == 0) | (lane == t_blk) | (lane == t_blk - 1)
    future = lane > t_blk
    score = jnp.where(forced, jnp.inf, jnp.where(future, -jnp.inf, p_slc))
    cnt = jnp.zeros((tq, LANES), jnp.int32)
    for i in range(n_slc):
        col = score[:, i:i + 1]
        ahead = (col > score) | ((col == score) & (lane > i))
        cnt = cnt + ahead.astype(jnp.int32)
    n_sel = min(SLC_TOPK, n_slc)
    chosen = (cnt < n_sel) & (score > -jnp.inf)
    pow2 = jnp.left_shift(1, lane & 15).astype(F32)
    lo = jnp.where(chosen & (lane < 16), pow2, 0.0).sum(-1, keepdims=True)
    hi = jnp.where(chosen & (lane >= 16) & (lane < 32), pow2, 0.0).sum(-1, keepdims=True)
    bits = lo.astype(jnp.int32) | (hi.astype(jnp.int32) << 16)

    tpos_k = q0 + lax.broadcasted_iota(jnp.int32, (tq, NSA_TK), 0)
    kiota = lax.broadcasted_iota(jnp.int32, (tq, NSA_TK), 1)
    bits_b = jnp.broadcast_to(bits, (tq, NSA_TK))

    def slc_bias(k0):
        kpos = k0 + kiota
        picked = (lax.shift_right_logical(bits_b, kpos // SLC_LEN) & 1) == 1
        return jnp.where(picked & (kpos <= tpos_k), 0.0, NEG_INF)

    o_slc = _flash(q4, ks_ref, vs_ref, 0, (q0 + tq) // NSA_TK, slc_bias, acc_ref)

    def win_bias(k0):
        kpos = k0 + kiota
        return jnp.where((kpos <= tpos_k) & (kpos > tpos_k - WINDOW), 0.0, NEG_INF)

    kt_lo = jnp.maximum(q0 - WINDOW, 0) // NSA_TK
    o_win = _flash(q4, kw_ref, vw_ref, kt_lo, (q0 + tq) // NSA_TK, win_bias, acc_ref)

    gate = jax.nn.sigmoid(gate_ref[0])
    for r in range(NSA_REP):
        rs = slice(r * tq, (r + 1) * tq)
        o = (gate[:, 3 * r:3 * r + 1] * o_cmp[rs] + gate[:, 3 * r + 1:3 * r + 2] * o_slc[rs]
             + gate[:, 3 * r + 2:3 * r + 3] * o_win[rs])
        o_ref[0, :, r * HEAD_DIM:(r + 1) * HEAD_DIM] = o.astype(o_ref.dtype)


def _cover_matrix(n_rows, n_slc):
    i = np.arange(n_rows)[:, None]
    j = np.arange(LANES)[None, :]
    c0 = i * CMP_STRIDE
    c1 = c0 + CMP_LEN - 1
    s0 = j * SLC_LEN
    s1 = s0 + SLC_LEN - 1
    ok = (c1 >= s0) & (c0 <= s1) & (i < n_rows - 1) & (j < n_slc)
    return jnp.asarray(ok.astype(np.float32), dtype=BF16)


def _nsa_attention(y3, kc, vc, ks, vs, kw, vw, cos, sin):
    b, s, _ = y3.shape
    g, hd, tq = NSA_KV_GROUPS, HEAD_DIM, NSA_TQ
    n_rows = s // CMP_STRIDE
    assert n_rows == LANES and s // SLC_LEN <= 32 and s % tq == 0
    cover = _cover_matrix(n_rows, s // SLC_LEN)
    qw = NSA_REP * hd
    gate_blk = _SEGS["g_nsa"][0] // LANES
    q_blk = _SEGS["q_nsa"][0] // qw
    seq_g = pl.BlockSpec((1, s, hd), lambda i, j, t: (i, 0, j))
    cmp_g = pl.BlockSpec((1, n_rows, hd), lambda i, j, t: (i, 0, j))
    tab = pl.BlockSpec((tq, hd), lambda i, j, t: (t, 0))
    return pl.pallas_call(
        functools.partial(_nsa_kernel, n_slc=s // SLC_LEN),
        out_shape=jax.ShapeDtypeStruct((b, s, NSA_Q_W), BF16),
        grid=(b, g, s // tq),
        in_specs=[pl.BlockSpec((1, tq, qw), lambda i, j, t: (i, t, q_blk + j)),
                  pl.BlockSpec((1, tq, LANES), lambda i, j, t: (i, t, gate_blk + j)),
                  cmp_g, cmp_g, seq_g, seq_g, seq_g, seq_g, tab, tab,
                  pl.BlockSpec((n_rows, LANES), lambda i, j, t: (0, 0))],
        out_specs=pl.BlockSpec((1, tq, qw), lambda i, j, t: (i, t, j)),
        scratch_shapes=[pltpu.VMEM((NSA_REP * tq, hd), F32)],
        compiler_params=pltpu.CompilerParams(dimension_semantics=("parallel", "parallel", "arbitrary"),
                                             vmem_limit_bytes=VMEM_LIMIT),
        name="nsa_attention",
    )(y3, y3, kc, vc, ks, vs, kw, vw, cos, sin, cover)


FFN_TM = 256
FFN_NCH = 8
_STAGE, _COMPUTE, _ZERO = 0, 1, 2


def _ffn_kernel(kind_ref, e_ref, c_ref, blk_ref, x_ref, w_ref, wg_ref, wu_ref, wd_ref, o_ref,
                wg_s, wu_s, wd_s):
    s = pl.program_id(0)
    kind = kind_ref[s]
    rg = wg_ref.shape[1]
    rd = wd_ref.shape[1]

    @pl.when(kind == _STAGE)
    def _():
        c = c_ref[s]
        r0 = pl.multiple_of(c * rg, rg)
        wg_s[pl.ds(r0, rg), :] = wg_ref[0].astype(BF16)
        wu_s[pl.ds(r0, rg), :] = wu_ref[0].astype(BF16)
        r1 = pl.multiple_of(c * rd, rd)
        wd_s[pl.ds(r1, rd), :] = wd_ref[0].astype(BF16)

    @pl.when(kind == _COMPUTE)
    def _():
        x = x_ref[...]
        gate = jnp.dot(x, wg_s[...], preferred_element_type=F32)
        up = jnp.dot(x, wu_s[...], preferred_element_type=F32)
        h = (jax.nn.silu(gate) * up).astype(BF16)
        o_ref[...] = jnp.dot(h, wd_s[...], preferred_element_type=F32) * w_ref[...]

    @pl.when(kind == _ZERO)
    def _():
        o_ref[...] = jnp.zeros_like(o_ref)


def _ffn_items(nb_e, n_blocks):
    e = nb_e.shape[0]
    blk_start = jnp.cumsum(nb_e) - nb_e
    n_used = blk_start[-1] + nb_e[-1]
    base = FFN_NCH * jnp.arange(e) + blk_start
    s = jnp.arange(FFN_NCH * e + n_blocks)
    e_s = (s[:, None] >= base[None, :]).sum(1) - 1
    r = s - base[e_s]
    end = FFN_NCH * e + n_used
    kind = jnp.where(s >= end, _ZERO, jnp.where(r < FFN_NCH, _STAGE, _COMPUTE))
    c_s = jnp.minimum(r, FFN_NCH - 1)
    blk = jnp.where(kind == _ZERO, n_used + s - end,
                    jnp.where(kind == _STAGE, blk_start[e_s], blk_start[e_s] + r - FFN_NCH))
    blk = jnp.minimum(blk, n_blocks - 1)
    i32 = lambda a: a.astype(jnp.int32)
    return i32(kind), i32(e_s), i32(c_s), i32(blk)


def _grouped_ffn(xs, row_w, nb_e, wg, wu, wd):
    p, d = xs.shape
    e, _, f = wg.shape
    n_blocks = p // FFN_TM
    assert d % FFN_NCH == 0 and f % (FFN_NCH * 16) == 0
    rg, rd = d // FFN_NCH, f // FFN_NCH
    kind, e_s, c_s, blk = _ffn_items(nb_e, n_blocks)
    row = lambda w: pl.BlockSpec((FFN_TM, w), lambda s, k, es, cs, bs: (bs[s], 0))
    chunk = lambda r, w: pl.BlockSpec((1, r, w), lambda s, k, es, cs, bs: (es[s], cs[s], 0))
    grid_spec = pltpu.PrefetchScalarGridSpec(
        num_scalar_prefetch=4,
        grid=(FFN_NCH * e + n_blocks,),
        in_specs=[row(d), row(1), chunk(rg, f), chunk(rg, f), chunk(rd, d)],
        out_specs=row(d),
        scratch_shapes=[pltpu.VMEM((d, f), BF16), pltpu.VMEM((d, f), BF16), pltpu.VMEM((f, d), BF16)],
    )
    return pl.pallas_call(
        _ffn_kernel,
        out_shape=jax.ShapeDtypeStruct((p, d), F32),
        grid_spec=grid_spec,
        compiler_params=pltpu.CompilerParams(dimension_semantics=("arbitrary",), vmem_limit_bytes=VMEM_LIMIT),
        name="grouped_ffn",
    )(kind, e_s, c_s, blk, xs, row_w, wg, wu, wd)


ML_L = 256
ML_HALO = 8
MLSTM_CONV = 4


def _mlstm_kernel(qk_ref, v_ref, o_ref, if_ref, convw_ref, convb_ref, ifb_ref, ng_ref, tril_ref,
                  out_ref, c_ref, m_ref, halo_ref):
    L, NH, DK, DV = ML_L, MLSTM_HEADS, MLSTM_QK_DIM, MLSTM_V_DIM

    @pl.when(pl.program_id(1) == 0)
    def _():
        c_ref[...] = jnp.zeros_like(c_ref)
        m_ref[...] = jnp.zeros_like(m_ref)
        halo_ref[...] = jnp.zeros_like(halo_ref)

    x = qk_ref[0]
    cat = jnp.concatenate([halo_ref[...], x], axis=0)
    halo_ref[...] = x[L - ML_HALO:, :]
    y = convb_ref[...]
    for j in range(MLSTM_CONV):
        lo = ML_HALO - (MLSTM_CONV - 1) + j
        y = y + convw_ref[j:j + 1, :] * cat[lo:lo + L, :]
    qk = y * jax.nn.sigmoid(y)

    lane = lax.broadcasted_iota(jnp.int32, (L, LANES), 1)
    z = if_ref[0] + ifb_ref[...]
    lg = jnp.where(lane < NH, z, jnp.minimum(z, 0.0) - jnp.log1p(jnp.exp(-jnp.abs(z))))
    hi = lg.astype(BF16)
    r1 = lg - hi.astype(F32)
    mid = r1.astype(BF16)
    lo3 = (r1 - mid.astype(F32)).astype(BF16)
    tril = tril_ref[...]
    bcum = (jnp.dot(tril, hi, preferred_element_type=F32) + jnp.dot(tril, mid, preferred_element_type=F32)
            + jnp.dot(tril, lo3, preferred_element_type=F32))

    row = lax.broadcasted_iota(jnp.int32, (L, L), 0)
    colm = lax.broadcasted_iota(jnp.int32, (L, L), 1)
    tri = row >= colm
    ones_col = (lane == 0).astype(BF16)
    for h in range(NH):
        li = lg[:, h:h + 1]
        b = bcum[:, NH + h:NH + h + 1]
        m_prev = m_ref[h:h + 1, 0:1]
        a_col = li - b
        a_row = jnp.transpose(jnp.broadcast_to(a_col, (L, LANES)))[0:1, :]
        dmat = jnp.where(tri, b + a_row, -jnp.inf)
        m_inter = b + m_prev
        m_t = jnp.maximum(m_inter, dmat.max(-1, keepdims=True))
        q = qk[:, h * DK:(h + 1) * DK].astype(BF16)
        k = qk[:, NH * DK + h * DK:NH * DK + (h + 1) * DK] * DK ** -0.5
        s = lax.dot_general(q, k.astype(BF16), (((1,), (1,)), ((), ())), preferred_element_type=F32)
        wqk = (jnp.exp(dmat - m_t) * s).astype(BF16)
        inter = jnp.exp(m_inter - m_t)
        v_ext = jnp.concatenate([v_ref[0, :, h * DV:(h + 1) * DV].astype(BF16), ones_col], axis=1)
        c_old = c_ref[h]
        nd = (jnp.dot(wqk, v_ext, preferred_element_type=F32)
              + inter * jnp.dot(q, c_old.astype(BF16), preferred_element_type=F32))
        num = nd[:, :DV]
        den = nd[:, DV:DV + 1]
        hh = num / jnp.maximum(jnp.abs(den), jnp.exp(-m_t))
        b_last = b[L - 1:L, :]
        g = b_last - b + li
        m_new = jnp.maximum(b_last + m_prev, g.max(0, keepdims=True))
        decay = jnp.exp(b_last + m_prev - m_new)
        w_s = jnp.exp(g - m_new)
        kw_t = jnp.transpose(k * w_s).astype(BF16)
        c_ref[h] = decay * c_old + jnp.dot(kw_t, v_ext, preferred_element_type=F32)
        m_ref[h:h + 1, :] = jnp.broadcast_to(m_new, (1, LANES))
        mu = hh.mean(-1, keepdims=True)
        var = jnp.mean(jnp.square(hh - mu), -1, keepdims=True)
        hn = (hh - mu) * lax.rsqrt(var + LN_EPS) * ng_ref[:, h * DV:(h + 1) * DV]
        out_ref[0, :, h * DV:(h + 1) * DV] = (
            jax.nn.sigmoid(o_ref[0, :, h * DV:(h + 1) * DV]) * hn).astype(out_ref.dtype)


def _mlstm(y3, conv_w, conv_b, if_bias, norm_g):
    b, s, _ = y3.shape
    L, NH = ML_L, MLSTM_HEADS
    assert s % L == 0

    def col(name, width):
        blk = _SEGS[name][0] // width
        return pl.BlockSpec((1, L, width), lambda i, j: (i, j, blk))

    full = lambda shp: pl.BlockSpec(shp, lambda i, j: (0,) * len(shp))
    ifb = jnp.pad(if_bias.reshape(1, 2 * NH), ((0, 0), (0, LANES - 2 * NH)))
    tril = jnp.asarray(np.tril(np.ones((L, L), np.float32)), dtype=BF16)
    conv_b = conv_b.reshape(1, -1)
    norm_g = norm_g.reshape(1, -1)
    return pl.pallas_call(
        _mlstm_kernel,
        out_shape=jax.ShapeDtypeStruct((b, s, ML_V_W), BF16),
        grid=(b, s // L),
        in_specs=[col("qk_ml", 2 * ML_QK_W), col("v_ml", ML_V_W), col("o_ml", ML_V_W), col("if_ml", LANES),
                  full(conv_w.shape), full(conv_b.shape), full(ifb.shape), full(norm_g.shape), full(tril.shape)],
        out_specs=pl.BlockSpec((1, L, ML_V_W), lambda i, j: (i, j, 0)),
        scratch_shapes=[pltpu.VMEM((NH, MLSTM_QK_DIM, MLSTM_V_DIM + LANES), F32),
                        pltpu.VMEM((8, LANES), F32),
                        pltpu.VMEM((ML_HALO, 2 * ML_QK_W), F32)],
        compiler_params=pltpu.CompilerParams(dimension_semantics=("parallel", "arbitrary"),
                                             vmem_limit_bytes=VMEM_LIMIT),
        name="mlstm",
    )(y3, y3, y3, y3, conv_w, conv_b, ifb, norm_g, tril)


MEM_TQ = 512


def _mem_attn_kernel(q_ref, kv_ref, o_ref):
    hd = MEM_HEAD_DIM
    for h in range(MEM_HEADS):
        q = q_ref[0, :, h * hd:(h + 1) * hd].astype(BF16)
        k = kv_ref[0, :, h * hd:(h + 1) * hd]
        v = kv_ref[0, :, MEM_W + h * hd:MEM_W + (h + 1) * hd]
        s = lax.dot_general(q, k, (((1,), (1,)), ((), ())), preferred_element_type=F32) * hd ** -0.5
        e = jnp.exp(s - s.max(-1, keepdims=True))
        p = (e / e.sum(-1, keepdims=True)).astype(BF16)
        o_ref[0, :, h * hd:(h + 1) * hd] = jnp.dot(p, v, preferred_element_type=F32).astype(o_ref.dtype)


def _mem_attention(y3, kv_mem):
    b, s, _ = y3.shape
    m = kv_mem.shape[1]
    q_blk = _SEGS["q_mem"][0] // MEM_W
    return pl.pallas_call(
        _mem_attn_kernel,
        out_shape=jax.ShapeDtypeStruct((b, s, MEM_W), BF16),
        grid=(b, s // MEM_TQ),
        in_specs=[pl.BlockSpec((1, MEM_TQ, MEM_W), lambda i, j: (i, j, q_blk)),
                  pl.BlockSpec((1, m, 2 * MEM_W), lambda i, j: (i, 0, 0))],
        out_specs=pl.BlockSpec((1, MEM_TQ, MEM_W), lambda i, j: (i, j, 0)),
        compiler_params=pltpu.CompilerParams(dimension_semantics=("parallel", "parallel"),
                                             vmem_limit_bytes=VMEM_LIMIT),
        name="mem_attention",
    )(y3, kv_mem)


MERGE_TM = 256


def _merge_kernel(yn_ref, ym_ref, yc_ref, g0_ref, g1_ref, g2_ref, x_ref, wb_ref, wo_ref, lg_ref, lb_ref,
                  x1_ref, x1b_ref):
    merged = None
    for i, (y_ref, g_ref) in enumerate(((yn_ref, g0_ref), (ym_ref, g1_ref), (yc_ref, g2_ref))):
        t = jax.nn.sigmoid(g_ref[...]) * jnp.dot(y_ref[...], wb_ref[i], preferred_element_type=F32)
        merged = t if merged is None else merged + t
    z = DN_ALPHA * x_ref[...] + jnp.dot(merged.astype(BF16), wo_ref[...], preferred_element_type=F32)
    mu = z.mean(-1, keepdims=True)
    var = jnp.mean(jnp.square(z - mu), -1, keepdims=True)
    x1 = (z - mu) * lax.rsqrt(var + LN_EPS) * lg_ref[...] + lb_ref[...]
    x1_ref[...] = x1
    x1b_ref[...] = x1.astype(BF16)


def _merge_ln1(y_nsa, y_ml, y_mem, y2, xt, wb, wo, ln_g, ln_b):
    t, d = xt.shape
    tm = MERGE_TM
    bw = y_nsa.shape[1]
    resident = pl.Buffered(1)
    row = lambda w: pl.BlockSpec((tm, w), lambda i: (i, 0))
    gate = lambda k: pl.BlockSpec((tm, d), lambda i: (i, k))
    return pl.pallas_call(
        _merge_kernel,
        out_shape=(jax.ShapeDtypeStruct((t, d), F32), jax.ShapeDtypeStruct((t, d), BF16)),
        grid=(t // tm,),
        in_specs=[row(bw), row(bw), row(bw), gate(0), gate(1), gate(2), row(d),
                  pl.BlockSpec((N_BRANCH, bw, d), lambda i: (0, 0, 0), pipeline_mode=resident),
                  pl.BlockSpec((d, d), lambda i: (0, 0), pipeline_mode=resident),
                  pl.BlockSpec((1, d), lambda i: (0, 0)), pl.BlockSpec((1, d), lambda i: (0, 0))],
        out_specs=(row(d), row(d)),
        compiler_params=pltpu.CompilerParams(dimension_semantics=("parallel",), vmem_limit_bytes=VMEM_LIMIT),
        name="merge_ln1",
    )(y_nsa, y_ml, y_mem, y2, y2, y2, xt, wb, wo, ln_g.reshape(1, d), ln_b.reshape(1, d))


ROUTER_TM = 256


def _router_kernel(x_ref, w_ref, bias_ref, tril_ref, idx_ref, wt_ref, rank_ref, cnt_ref, run_ref):
    tm = ROUTER_TM

    @pl.when(pl.program_id(0) == 0)
    def _():
        run_ref[...] = jnp.zeros_like(run_ref)

    x = x_ref[...]
    w = w_ref[...]
    x_hi = x.astype(BF16)
    x_lo = (x - x_hi.astype(F32)).astype(BF16)
    w_hi = w.astype(BF16)
    w_lo = (w - w_hi.astype(F32)).astype(BF16)
    logits = (jnp.dot(x_hi, w_hi, preferred_element_type=F32) + jnp.dot(x_lo, w_hi, preferred_element_type=F32)
              + jnp.dot(x_hi, w_lo, preferred_element_type=F32))
    lane = lax.broadcasted_iota(jnp.int32, (tm, LANES), 1)
    real = lane < N_EXPERTS
    aff = jax.nn.sigmoid(logits)
    score = jnp.where(real, aff + bias_ref[...], -jnp.inf)
    picks = []
    chosen = jnp.zeros((tm, LANES), jnp.bool_)
    for _ in range(TOP_K):
        best = score.max(-1, keepdims=True)
        cur = jnp.where(score == best, lane, LANES).min(-1, keepdims=True)
        here = lane == cur
        picks.append((cur, here))
        chosen = chosen | here
        score = jnp.where(here, -jnp.inf, score)
    run = run_ref[0:1, :]
    excl = jnp.dot(tril_ref[...], chosen.astype(BF16), preferred_element_type=F32) + run
    run_new = run + chosen.astype(F32).sum(0, keepdims=True)
    run_ref[...] = jnp.broadcast_to(run_new, run_ref.shape)
    cnt_ref[...] = jnp.broadcast_to(run_new, cnt_ref.shape)

    idx_out = jnp.zeros((tm, LANES), jnp.int32)
    aff_out = jnp.zeros((tm, LANES), F32)
    rank_out = jnp.zeros((tm, LANES), F32)
    for k, (cur, here) in enumerate(picks):
        idx_out = jnp.where(lane == k, cur, idx_out)
        aff_out = jnp.where(lane == k, jnp.where(here, aff, 0.0).sum(-1, keepdims=True), aff_out)
        rank_out = jnp.where(lane == k, jnp.where(here, excl, 0.0).sum(-1, keepdims=True), rank_out)
    idx_ref[...] = idx_out
    wt_ref[...] = aff_out / aff_out.sum(-1, keepdims=True) * ROUTED_SCALE
    rank_ref[...] = rank_out.astype(jnp.int32)


def _router(x1, router_w, router_bias):
    t, d = x1.shape
    tm = ROUTER_TM
    e = router_w.shape[1]
    w = jnp.pad(router_w, ((0, 0), (0, LANES - e)))
    bias = jnp.pad(router_bias.reshape(1, e), ((0, 0), (0, LANES - e)))
    tril = jnp.asarray(np.tril(np.ones((tm, tm), np.float32), -1), dtype=BF16)
    out = pl.BlockSpec((tm, LANES), lambda i: (i, 0))
    const = lambda shp: pl.BlockSpec(shp, lambda i: (0, 0))
    idx, wt, rank, cnt = pl.pallas_call(
        _router_kernel,
        out_shape=(jax.ShapeDtypeStruct((t, LANES), jnp.int32), jax.ShapeDtypeStruct((t, LANES), F32),
                   jax.ShapeDtypeStruct((t, LANES), jnp.int32), jax.ShapeDtypeStruct((8, LANES), F32)),
        grid=(t // tm,),
        in_specs=[pl.BlockSpec((tm, d), lambda i: (i, 0)), const(w.shape), const(bias.shape), const(tril.shape)],
        out_specs=(out, out, out, const((8, LANES))),
        scratch_shapes=[pltpu.VMEM((8, LANES), F32)],
        compiler_params=pltpu.CompilerParams(dimension_semantics=("arbitrary",), vmem_limit_bytes=VMEM_LIMIT),
        name="router",
    )(x1, w, bias, tril)
    return idx[:, :TOP_K], wt[:, :TOP_K], rank[:, :TOP_K], cnt[0, :e].astype(jnp.int32)


def _moe_layout(idx, wts, rank, counts):
    t = idx.shape[0]
    e = N_EXPERTS
    m = t * TOP_K
    nb_e = (counts + FFN_TM - 1) // FFN_TM
    blk_start = jnp.cumsum(nb_e) - nb_e
    start = jnp.cumsum(counts) - counts
    n_blocks = m // FFN_TM + e
    pos = (blk_start * FFN_TM)[idx] + rank
    order = jnp.argsort(idx.reshape(m))
    bi = jnp.arange(n_blocks)
    blk_e = jnp.clip((bi[:, None] >= blk_start[None, :]).sum(1) - 1, 0, e - 1)
    j = ((bi - blk_start[blk_e]) * FFN_TM)[:, None] + jnp.arange(FFN_TM)[None, :]
    valid = (j < counts[blk_e][:, None]).reshape(-1)
    flat = order[jnp.clip(start[blk_e][:, None] + j, 0, m - 1).reshape(-1)]
    row_tok = jnp.where(valid, flat // TOP_K, 0).astype(jnp.int32)
    row_w = jnp.where(valid, wts.reshape(m)[flat], 0.0)
    return row_tok, row_w, nb_e, pos


def _combine_kernel(x_ref, r_ref, s_ref, g_ref, b_ref, o_ref):
    d = x_ref.shape[1]
    routed = r_ref[:, 0:d]
    for k in range(1, TOP_K):
        routed = routed + r_ref[:, k * d:(k + 1) * d]
    z = DN_ALPHA * x_ref[...] + (routed + s_ref[...])
    mu = z.mean(-1, keepdims=True)
    var = jnp.mean(jnp.square(z - mu), -1, keepdims=True)
    o_ref[...] = (z - mu) * lax.rsqrt(var + LN_EPS) * g_ref[...] + b_ref[...]


def _combine_ln2(x1, routed_rows, shared, ln_g, ln_b, *, tm=128):
    t, d = x1.shape
    row = pl.BlockSpec((tm, d), lambda i: (i, 0))
    vec = pl.BlockSpec((1, d), lambda i: (0, 0))
    return pl.pallas_call(
        _combine_kernel, out_shape=jax.ShapeDtypeStruct((t, d), F32), grid=(t // tm,),
        in_specs=[row, pl.BlockSpec((tm, TOP_K * d), lambda i: (i, 0)), row, vec, vec], out_specs=row,
        compiler_params=pltpu.CompilerParams(dimension_semantics=("parallel",), vmem_limit_bytes=VMEM_LIMIT),
        name="combine_ln2",
    )(x1, routed_rows, shared, ln_g.reshape(1, d), ln_b.reshape(1, d))


def _layer(x, mem, w_in, mlstm_conv_w, mlstm_conv_b, mlstm_if_bias, mlstm_norm_g,
           cmp_pos_k, cmp_k_w1, cmp_k_w2, cmp_pos_v, cmp_v_w1, cmp_v_w2,
           w_mem_kv, w_branch, w_o, ln1_g, ln1_b, router_w, router_bias,
           w_e_gate, w_e_up, w_e_down, w_s_gate, w_s_up, w_s_down, ln2_g, ln2_b):
    B, S, D = x.shape
    T = B * S
    xt = x.reshape(T, D)
    y2 = _matmul(xt.astype(BF16), _pad_in_proj(w_in).astype(BF16), tm=1024, tn=IN_TN, name="in_proj")
    y3 = y2.reshape(B, S, IN_PAD)

    cos, sin = _rope_tables(jnp.arange(S))
    cos_e, sin_e = _rope_tables(jnp.arange(S // CMP_STRIDE) * CMP_STRIDE + CMP_LEN - 1)
    kvc, ks, vs, kw, vw = _nsa_prep(y3, cos, sin)
    kc, vc = _compress(kvc, cmp_pos_k, cmp_k_w1, cmp_k_w2, cmp_pos_v, cmp_v_w1, cmp_v_w2, cos_e, sin_e)
    y_nsa = _nsa_attention(y3, kc, vc, ks, vs, kw, vw, cos, sin)

    y_ml = _mlstm(y3, mlstm_conv_w, mlstm_conv_b, mlstm_if_bias, mlstm_norm_g)

    m_len = mem.shape[1]
    kv_mem = _matmul(mem.reshape(B * m_len, D).astype(BF16), w_mem_kv.astype(BF16), tm=1024, tn=512,
                     out_dtype=BF16, name="mem_kv").reshape(B, m_len, 2 * MEM_W)
    y_mem = _mem_attention(y3, kv_mem)

    x1, x1b = _merge_ln1(y_nsa.reshape(T, -1), y_ml.reshape(T, -1), y_mem.reshape(T, -1), y2, xt,
                         w_branch.astype(BF16), w_o.astype(BF16), ln1_g, ln1_b)

    idx, wts, rank, counts = _router(x1, router_w, router_bias)
    row_tok, row_w, nb_e, pos = _moe_layout(idx, wts, rank, counts)
    out = _grouped_ffn(x1b[row_tok], row_w[:, None], nb_e, w_e_gate, w_e_up, w_e_down)
    routed_rows = out[pos.reshape(-1)].reshape(T, TOP_K * D)
    shared = _grouped_ffn(x1b, jnp.ones((T, 1), F32), jnp.full((1,), T // FFN_TM, jnp.int32),
                          w_s_gate[None], w_s_up[None], w_s_down[None])
    return _combine_ln2(x1, routed_rows, shared, ln2_g, ln2_b).reshape(B, S, D)


def kernel(x, mem, w_in, mlstm_conv_w, mlstm_conv_b, mlstm_if_bias, mlstm_norm_g, cmp_pos_k, cmp_k_w1, cmp_k_w2, cmp_pos_v, cmp_v_w1, cmp_v_w2, w_mem_kv, w_branch, w_o, ln1_g, ln1_b, router_w, router_bias, w_e_gate, w_e_up, w_e_down, w_s_gate, w_s_up, w_s_down, ln2_g, ln2_b):
    for layer in range(DEPTH):
        x = _layer(x, mem, w_in[layer], mlstm_conv_w[layer], mlstm_conv_b[layer],
                   mlstm_if_bias[layer], mlstm_norm_g[layer],
                   cmp_pos_k[layer], cmp_k_w1[layer], cmp_k_w2[layer],
                   cmp_pos_v[layer], cmp_v_w1[layer], cmp_v_w2[layer],
                   w_mem_kv[layer], w_branch[layer], w_o[layer], ln1_g[layer], ln1_b[layer],
                   router_w[layer], router_bias[layer], w_e_gate[layer], w_e_up[layer],
                   w_e_down[layer], w_s_gate[layer], w_s_up[layer], w_s_down[layer],
                   ln2_g[layer], ln2_b[layer])
    return x
```

```python
import functools

import jax
import jax.numpy as jnp
import numpy as np
from jax import lax
from jax.experimental import pallas as pl
from jax.experimental.pallas import tpu as pltpu

F32 = jnp.float32
BF16 = jnp.bfloat16

D_MODEL = 2048
HEAD_DIM = 128
NSA_HEADS = 8
NSA_KV_GROUPS = 2
NSA_REP = NSA_HEADS // NSA_KV_GROUPS
CMP_LEN = 32
CMP_STRIDE = 16
SLC_LEN = 64
SLC_TOPK = 16
WINDOW = 512
MLSTM_HEADS = 4
MLSTM_QK_DIM = 128
MLSTM_V_DIM = 256
MLSTM_CHUNK = 64
MEM_HEADS = 4
MEM_HEAD_DIM = 256
N_EXPERTS = 64
TOP_K = 6
EXPERT_DIM = 1408
ROUTED_SCALE = 2.5
ROPE_THETA = 10000.0
LN_EPS = 1e-5
NEG_INF = -1e30
DEPTH = 1
DN_ALPHA = (2 * DEPTH) ** 0.25

NSA_Q_W = NSA_HEADS * HEAD_DIM
NSA_KV_W = NSA_KV_GROUPS * HEAD_DIM
ML_QK_W = MLSTM_HEADS * MLSTM_QK_DIM
ML_V_W = MLSTM_HEADS * MLSTM_V_DIM
MEM_W = MEM_HEADS * MEM_HEAD_DIM
N_BRANCH = 3

LANES = 128
VMEM_LIMIT = 56 * 1024 * 1024

_SEGS = {}
_off = 0
for _name, _w, _blk in (("g_br", N_BRANCH * D_MODEL, D_MODEL), ("q_nsa", NSA_Q_W, NSA_REP * HEAD_DIM),
                        ("q_mem", MEM_W, MEM_W), ("v_ml", ML_V_W, ML_V_W), ("o_ml", ML_V_W, ML_V_W),
                        ("qk_ml", 2 * ML_QK_W, 2 * ML_QK_W), ("kv_cmp", 2 * NSA_KV_W, 2 * NSA_KV_W),
                        ("k_slc", NSA_KV_W, NSA_KV_W), ("v_slc", NSA_KV_W, NSA_KV_W),
                        ("k_win", NSA_KV_W, NSA_KV_W), ("v_win", NSA_KV_W, NSA_KV_W),
                        ("g_nsa", NSA_KV_GROUPS * LANES, LANES), ("if_ml", LANES, LANES)):
    assert _off % _blk == 0
    _SEGS[_name] = (_off, _w)
    _off += _w
IN_TN = 512
IN_PAD = -(-_off // IN_TN) * IN_TN


def _pad_in_proj(w_in):
    sizes = (NSA_Q_W,) + (NSA_KV_W,) * 6 + (NSA_HEADS * 3, ML_QK_W, ML_QK_W, ML_V_W, ML_V_W,
                                            MLSTM_HEADS, MLSTM_HEADS, MEM_W, N_BRANCH * D_MODEL)
    splits = np.cumsum(sizes)[:-1].tolist()
    (q_nsa, k_cmp, v_cmp, k_slc, v_slc, k_win, v_win, g_nsa,
     q_ml, k_ml, v_ml, o_ml, i_ml, f_ml, q_mem, g_br) = jnp.split(w_in, splits, axis=-1)
    per_group = NSA_REP * 3
    g_blocks = [jnp.pad(g_nsa[:, g * per_group:(g + 1) * per_group], ((0, 0), (0, LANES - per_group)))
                for g in range(NSA_KV_GROUPS)]
    if_blk = jnp.pad(jnp.concatenate([i_ml, f_ml], -1), ((0, 0), (0, LANES - 2 * MLSTM_HEADS)))
    cols = [g_br, q_nsa, q_mem, v_ml, o_ml, q_ml, k_ml, k_cmp, v_cmp,
            k_slc, v_slc, k_win, v_win] + g_blocks + [if_blk]
    w = jnp.concatenate(cols, axis=-1)
    return jnp.pad(w, ((0, 0), (0, IN_PAD - w.shape[1])))


def _mm_kernel(a_ref, b_ref, o_ref):
    o_ref[...] = jnp.dot(a_ref[...].astype(BF16), b_ref[...].astype(BF16),
                         preferred_element_type=F32).astype(o_ref.dtype)


def _matmul(a, b, *, tm, tn, out_dtype=F32, name="matmul"):
    m, k = a.shape
    _, n = b.shape
    assert m % tm == 0 and n % tn == 0
    return pl.pallas_call(
        _mm_kernel,
        out_shape=jax.ShapeDtypeStruct((m, n), out_dtype),
        grid=(m // tm, n // tn),
        in_specs=[pl.BlockSpec((tm, k), lambda i, j: (i, 0)),
                  pl.BlockSpec((k, tn), lambda i, j: (0, j))],
        out_specs=pl.BlockSpec((tm, tn), lambda i, j: (i, j)),
        compiler_params=pltpu.CompilerParams(dimension_semantics=("parallel", "arbitrary"),
                                             vmem_limit_bytes=VMEM_LIMIT),
        name=name,
    )(a, b)


def _rope_tables(pos):
    half = HEAD_DIM // 2
    inv = ROPE_THETA ** (-jnp.arange(half, dtype=F32) / half)
    ang = pos.astype(F32)[:, None] * inv[None, :]
    cos, sin = jnp.cos(ang), jnp.sin(ang)
    return jnp.concatenate([cos, cos], -1), jnp.concatenate([-sin, sin], -1)


def _rope(x, cos, sin_signed):
    return x * cos + pltpu.roll(x, HEAD_DIM // 2, 1) * sin_signed


def _nsa_prep_kernel(kvc_ref, ks_ref, vs_ref, kw_ref, vw_ref, cos_ref, sin_ref,
                     kvc_o, ks_o, vs_o, kw_o, vw_o):
    cos, sin = cos_ref[...], sin_ref[...]
    kvc_o[0] = kvc_ref[0]
    for src, dst in ((ks_ref, ks_o), (kw_ref, kw_o)):
        for g in range(NSA_KV_GROUPS):
            sl = slice(g * HEAD_DIM, (g + 1) * HEAD_DIM)
            dst[0, :, sl] = _rope(src[0, :, sl], cos, sin).astype(BF16)
    vs_o[0] = vs_ref[0].astype(BF16)
    vw_o[0] = vw_ref[0].astype(BF16)


def _nsa_prep(y3, cos, sin, *, ts=512):
    b, s, _ = y3.shape
    kvw = NSA_KV_W

    def col(name, width):
        blk = _SEGS[name][0] // width
        return pl.BlockSpec((1, ts, width), lambda i, j: (i, j, blk))

    tab = pl.BlockSpec((ts, HEAD_DIM), lambda i, j: (j, 0))
    out_kv = pl.BlockSpec((1, ts, kvw), lambda i, j: (i, j, 0))
    return pl.pallas_call(
        _nsa_prep_kernel,
        out_shape=(jax.ShapeDtypeStruct((b, s, 2 * kvw), F32),) + (jax.ShapeDtypeStruct((b, s, kvw), BF16),) * 4,
        grid=(b, s // ts),
        in_specs=[col("kv_cmp", 2 * kvw), col("k_slc", kvw), col("v_slc", kvw), col("k_win", kvw),
                  col("v_win", kvw), tab, tab],
        out_specs=(pl.BlockSpec((1, ts, 2 * kvw), lambda i, j: (i, j, 0)), out_kv, out_kv, out_kv, out_kv),
        compiler_params=pltpu.CompilerParams(dimension_semantics=("parallel", "parallel")),
        name="nsa_prep",
    )(y3, y3, y3, y3, y3, cos, sin)


def _cmp_kernel(u_ref, posa_ref, posb_ref, w1a_ref, w1b_ref, w2_ref, cos_ref, sin_ref, kc_o, vc_o):
    n_rows = u_ref.shape[2]
    for idx, out in ((0, kc_o), (1, vc_o)):
        u = u_ref[0, idx]
        a = jnp.dot((u + posa_ref[idx]).astype(BF16), w1a_ref[idx], preferred_element_type=F32)
        bm = jnp.dot((u + posb_ref[idx]).astype(BF16), w1b_ref[idx], preferred_element_type=F32)
        pre = a + pltpu.roll(bm, n_rows - 1, 0)
        h = jax.nn.gelu(pre).astype(BF16)
        c = jnp.dot(h, w2_ref[idx], preferred_element_type=F32)
        if idx == 0:
            for g in range(NSA_KV_GROUPS):
                sl = slice(g * HEAD_DIM, (g + 1) * HEAD_DIM)
                out[0, :, sl] = _rope(c[:, sl], cos_ref[...], sin_ref[...]).astype(BF16)
        else:
            out[0] = c.astype(BF16)


def _block_diag_groups(w):
    z = jnp.zeros_like(w)
    rows = [jnp.concatenate([w if gi == go else z for go in range(NSA_KV_GROUPS)], -1)
            for gi in range(NSA_KV_GROUPS)]
    return jnp.concatenate(rows, -2)


def _compress(kvc, cmp_pos_k, cmp_k_w1, cmp_k_w2, cmp_pos_v, cmp_v_w1, cmp_v_w2, cos_e, sin_e):
    b, s, _ = kvc.shape
    n_rows = s // CMP_STRIDE
    g, hd = NSA_KV_GROUPS, HEAD_DIM
    row_w = CMP_STRIDE * g * hd
    u = kvc.reshape(b, n_rows, CMP_STRIDE, 2, g * hd).transpose(0, 3, 1, 2, 4).reshape(b, 2, n_rows, row_w)

    def pos_rows(pos):
        p = jnp.broadcast_to(pos.reshape(2, CMP_STRIDE, 1, hd), (2, CMP_STRIDE, g, hd))
        return p.reshape(2, 1, row_w)

    def w1_halves(w1):
        w = w1.reshape(2, CMP_STRIDE, hd, hd)
        return _block_diag_groups(w).reshape(2, row_w, g * hd).astype(BF16)

    pk, pv = pos_rows(cmp_pos_k), pos_rows(cmp_pos_v)
    wk, wv = w1_halves(cmp_k_w1), w1_halves(cmp_v_w1)
    posa = jnp.stack([pk[0], pv[0]])
    posb = jnp.stack([pk[1], pv[1]])
    w1a = jnp.stack([wk[0], wv[0]])
    w1b = jnp.stack([wk[1], wv[1]])
    w2 = jnp.stack([_block_diag_groups(cmp_k_w2), _block_diag_groups(cmp_v_w2)]).astype(BF16)
    full = lambda shp: pl.BlockSpec(shp, lambda i: (0,) * len(shp))
    out_spec = pl.BlockSpec((1, n_rows, g * hd), lambda i: (i, 0, 0))
    return pl.pallas_call(
        _cmp_kernel,
        out_shape=(jax.ShapeDtypeStruct((b, n_rows, g * hd), BF16),) * 2,
        grid=(b,),
        in_specs=[pl.BlockSpec((1, 2, n_rows, row_w), lambda i: (i, 0, 0, 0)),
                  full(posa.shape), full(posb.shape), full(w1a.shape), full(w1b.shape), full(w2.shape),
                  full(cos_e.shape), full(sin_e.shape)],
        out_specs=(out_spec, out_spec),
        compiler_params=pltpu.CompilerParams(dimension_semantics=("parallel",), vmem_limit_bytes=VMEM_LIMIT),
        name="nsa_compress",
    )(u, posa, posb, w1a, w1b, w2, cos_e, sin_e)


NSA_TQ = 256
NSA_TK = 256


def _flash(q4, k_ref, v_ref, kt_lo, kt_hi, bias_fn, acc_ref):
    rows = q4.shape[0]
    tq = rows // NSA_REP
    scale = HEAD_DIM ** -0.5
    acc_ref[...] = jnp.zeros_like(acc_ref)

    def body(kt, carry):
        m, l = carry
        k0 = pl.multiple_of(kt * NSA_TK, NSA_TK)
        k = k_ref[0, pl.ds(k0, NSA_TK), :]
        v = v_ref[0, pl.ds(k0, NSA_TK), :]
        s = lax.dot_general(q4, k, (((1,), (1,)), ((), ())), preferred_element_type=F32) * scale
        s = s.reshape(NSA_REP, tq, NSA_TK) + bias_fn(k0)[None]
        m_new = jnp.maximum(m, s.max(-1, keepdims=True))
        alpha = jnp.exp(m - m_new)
        p = jnp.exp(s - m_new)
        l = alpha * l + p.sum(-1, keepdims=True)
        pv = jnp.dot(p.reshape(rows, NSA_TK).astype(BF16), v, preferred_element_type=F32)
        acc_ref[...] = acc_ref[...] * alpha.reshape(rows, 1) + pv
        return m_new, l

    init = (jnp.full((NSA_REP, tq, 1), NEG_INF, F32), jnp.zeros((NSA_REP, tq, 1), F32))
    _, l = lax.fori_loop(kt_lo, kt_hi, body, init)
    return acc_ref[...] / l.reshape(rows, 1)


def _nsa_kernel(q_ref, gate_ref, kc_ref, vc_ref, ks_ref, vs_ref, kw_ref, vw_ref, cos_ref, sin_ref,
                cover_ref, o_ref, acc_ref, *, n_slc):
    qi = pl.program_id(2)
    tq = NSA_TQ
    rows = NSA_REP * tq
    q0 = qi * tq
    cos, sin = cos_ref[...], sin_ref[...]
    q4 = jnp.concatenate(
        [_rope(q_ref[0, :, r * HEAD_DIM:(r + 1) * HEAD_DIM], cos, sin).astype(BF16) for r in range(NSA_REP)],
        axis=0)
    tpos = q0 + lax.broadcasted_iota(jnp.int32, (tq, LANES), 0)
    lane = lax.broadcasted_iota(jnp.int32, (tq, LANES), 1)

    s = lax.dot_general(q4, kc_ref[0], (((1,), (1,)), ((), ())), preferred_element_type=F32)
    s = (s * HEAD_DIM ** -0.5).reshape(NSA_REP, tq, LANES)
    cmask = (lane * CMP_STRIDE + (CMP_LEN - 1) <= tpos)[None]
    s = jnp.where(cmask, s, NEG_INF)
    e = jnp.exp(s - s.max(-1, keepdims=True))
    p = e / e.sum(-1, keepdims=True) * cmask.astype(F32)
    o_cmp = jnp.dot(p.reshape(rows, LANES).astype(BF16), vc_ref[0], preferred_element_type=F32)

    p_sum = p.sum(0)
    p_hi = p_sum.astype(BF16)
    p_lo = (p_sum - p_hi.astype(F32)).astype(BF16)
    cover = cover_ref[...]
    p_slc = (jnp.dot(p_hi, cover, preferred_element_type=F32)
             + jnp.dot(p_lo, cover, preferred_element_type=F32))
    t_blk = tpos // SLC_LEN
    forced = (lane == 0) | (lane == t_blk) | (lane == t_blk - 1)
    future = lane > t_blk
    score = jnp.where(forced, jnp.inf, jnp.where(future, -jnp.inf, p_slc))
    cnt = jnp.zeros((tq, LANES), jnp.int32)
    for i in range(n_slc):
        col = score[:, i:i + 1]
        ahead = (col > score) | ((col == score) & (lane > i))
        cnt = cnt + ahead.astype(jnp.int32)
    n_sel = min(SLC_TOPK, n_slc)
    chosen = (cnt < n_sel) & (score > -jnp.inf)
    pow2 = jnp.left_shift(1, lane & 15).astype(F32)
    lo = jnp.where(chosen & (lane < 16), pow2, 0.0).sum(-1, keepdims=True)
    hi = jnp.where(chosen & (lane >= 16) & (lane < 32), pow2, 0.0).sum(-1, keepdims=True)
    bits = lo.astype(jnp.int32) | (hi.astype(jnp.int32) << 16)

    tpos_k = q0 + lax.broadcasted_iota(jnp.int32, (tq, NSA_TK), 0)
    kiota = lax.broadcasted_iota(jnp.int32, (tq, NSA_TK), 1)
    bits_b = jnp.broadcast_to(bits, (tq, NSA_TK))

    def slc_bias(k0):
        kpos = k0 + kiota
        picked = (lax.shift_right_logical(bits_b, kpos // SLC_LEN) & 1) == 1
        return jnp.where(picked & (kpos <= tpos_k), 0.0, NEG_INF)

    o_slc = _flash(q4, ks_ref, vs_ref, 0, (q0 + tq) // NSA_TK, slc_bias, acc_ref)

    def win_bias(k0):
        kpos = k0 + kiota
        return jnp.where((kpos <= tpos_k) & (kpos > tpos_k - WINDOW), 0.0, NEG_INF)

    kt_lo = jnp.maximum(q0 - WINDOW, 0) // NSA_TK
    o_win = _flash(q4, kw_ref, vw_ref, kt_lo, (q0 + tq) // NSA_TK, win_bias, acc_ref)

    gate = jax.nn.sigmoid(gate_ref[0])
    for r in range(NSA_REP):
        rs = slice(r * tq, (r + 1) * tq)
        o = (gate[:, 3 * r:3 * r + 1] * o_cmp[rs] + gate[:, 3 * r + 1:3 * r + 2] * o_slc[rs]
             + gate[:, 3 * r + 2:3 * r + 3] * o_win[rs])
        o_ref[0, :, r * HEAD_DIM:(r + 1) * HEAD_DIM] = o.astype(o_ref.dtype)


def _cover_matrix(n_rows, n_slc):
    i = np.arange(n_rows)[:, None]
    j = np.arange(LANES)[None, :]
    c0 = i * CMP_STRIDE
    c1 = c0 + CMP_LEN - 1
    s0 = j * SLC_LEN
    s1 = s0 + SLC_LEN - 1
    ok = (c1 >= s0) & (c0 <= s1) & (i < n_rows - 1) & (j < n_slc)
    return jnp.asarray(ok.astype(np.float32), dtype=BF16)


def _nsa_attention(y3, kc, vc, ks, vs, kw, vw, cos, sin):
    b, s, _ = y3.shape
    g, hd, tq = NSA_KV_GROUPS, HEAD_DIM, NSA_TQ
    n_rows = s // CMP_STRIDE
    assert n_rows == LANES and s // SLC_LEN <= 32 and s % tq == 0
    cover = _cover_matrix(n_rows, s // SLC_LEN)
    qw = NSA_REP * hd
    gate_blk = _SEGS["g_nsa"][0] // LANES
    q_blk = _SEGS["q_nsa"][0] // qw
    seq_g = pl.BlockSpec((1, s, hd), lambda i, j, t: (i, 0, j))
    cmp_g = pl.BlockSpec((1, n_rows, hd), lambda i, j, t: (i, 0, j))
    tab = pl.BlockSpec((tq, hd), lambda i, j, t: (t, 0))
    return pl.pallas_call(
        functools.partial(_nsa_kernel, n_slc=s // SLC_LEN),
        out_shape=jax.ShapeDtypeStruct((b, s, NSA_Q_W), BF16),
        grid=(b, g, s // tq),
        in_specs=[pl.BlockSpec((1, tq, qw), lambda i, j, t: (i, t, q_blk + j)),
                  pl.BlockSpec((1, tq, LANES), lambda i, j, t: (i, t, gate_blk + j)),
                  cmp_g, cmp_g, seq_g, seq_g, seq_g, seq_g, tab, tab,
                  pl.BlockSpec((n_rows, LANES), lambda i, j, t: (0, 0))],
        out_specs=pl.BlockSpec((1, tq, qw), lambda i, j, t: (i, t, j)),
        scratch_shapes=[pltpu.VMEM((NSA_REP * tq, hd), F32)],
        compiler_params=pltpu.CompilerParams(dimension_semantics=("parallel", "parallel", "arbitrary"),
                                             vmem_limit_bytes=VMEM_LIMIT),
        name="nsa_attention",
    )(y3, y3, kc, vc, ks, vs, kw, vw, cos, sin, cover)


FFN_TM = 256
FFN_NCH = 8
_STAGE, _COMPUTE, _ZERO = 0, 1, 2


def _ffn_kernel(kind_ref, e_ref, c_ref, blk_ref, x_ref, w_ref, wg_ref, wu_ref, wd_ref, o_ref,
                wg_s, wu_s, wd_s):
    s = pl.program_id(0)
    kind = kind_ref[s]
    rg = wg_ref.shape[1]
    rd = wd_ref.shape[1]

    @pl.when(kind == _STAGE)
    def _():
        c = c_ref[s]
        r0 = pl.multiple_of(c * rg, rg)
        wg_s[pl.ds(r0, rg), :] = wg_ref[0].astype(BF16)
        wu_s[pl.ds(r0, rg), :] = wu_ref[0].astype(BF16)
        r1 = pl.multiple_of(c * rd, rd)
        wd_s[pl.ds(r1, rd), :] = wd_ref[0].astype(BF16)

    @pl.when(kind == _COMPUTE)
    def _():
        x = x_ref[...]
        gate = jnp.dot(x, wg_s[...], preferred_element_type=F32)
        up = jnp.dot(x, wu_s[...], preferred_element_type=F32)
        h = (jax.nn.silu(gate) * up).astype(BF16)
        o_ref[...] = jnp.dot(h, wd_s[...], preferred_element_type=F32) * w_ref[...]

    @pl.when(kind == _ZERO)
    def _():
        o_ref[...] = jnp.zeros_like(o_ref)


def _ffn_items(nb_e, n_blocks):
    e = nb_e.shape[0]
    blk_start = jnp.cumsum(nb_e) - nb_e
    n_used = blk_start[-1] + nb_e[-1]
    base = FFN_NCH * jnp.arange(e) + blk_start
    s = jnp.arange(FFN_NCH * e + n_blocks)
    e_s = (s[:, None] >= base[None, :]).sum(1) - 1
    r = s - base[e_s]
    end = FFN_NCH * e + n_used
    kind = jnp.where(s >= end, _ZERO, jnp.where(r < FFN_NCH, _STAGE, _COMPUTE))
    c_s = jnp.minimum(r, FFN_NCH - 1)
    blk = jnp.where(kind == _ZERO, n_used + s - end,
                    jnp.where(kind == _STAGE, blk_start[e_s], blk_start[e_s] + r - FFN_NCH))
    blk = jnp.minimum(blk, n_blocks - 1)
    i32 = lambda a: a.astype(jnp.int32)
    return i32(kind), i32(e_s), i32(c_s), i32(blk)


def _grouped_ffn(xs, row_w, nb_e, wg, wu, wd):
    p, d = xs.shape
    e, _, f = wg.shape
    n_blocks = p // FFN_TM
    assert d % FFN_NCH == 0 and f % (FFN_NCH * 16) == 0
    rg, rd = d // FFN_NCH, f // FFN_NCH
    kind, e_s, c_s, blk = _ffn_items(nb_e, n_blocks)
    row = lambda w: pl.BlockSpec((FFN_TM, w), lambda s, k, es, cs, bs: (bs[s], 0))
    chunk = lambda r, w: pl.BlockSpec((1, r, w), lambda s, k, es, cs, bs: (es[s], cs[s], 0))
    grid_spec = pltpu.PrefetchScalarGridSpec(
        num_scalar_prefetch=4,
        grid=(FFN_NCH * e + n_blocks,),
        in_specs=[row(d), row(1), chunk(rg, f), chunk(rg, f), chunk(rd, d)],
        out_specs=row(d),
        scratch_shapes=[pltpu.VMEM((d, f), BF16), pltpu.VMEM((d, f), BF16), pltpu.VMEM((f, d), BF16)],
    )
    return pl.pallas_call(
        _ffn_kernel,
        out_shape=jax.ShapeDtypeStruct((p, d), F32),
        grid_spec=grid_spec,
        compiler_params=pltpu.CompilerParams(dimension_semantics=("arbitrary",), vmem_limit_bytes=VMEM_LIMIT),
        name="grouped_ffn",
    )(kind, e_s, c_s, blk, xs, row_w, wg, wu, wd)


FFN_IDX_SLOTS = 4


def _routed_ffn_kernel(kind_ref, e_ref, c_ref, blk_ref, nblk_ref, x_hbm, tab_hbm, w_ref, wg_ref, wu_ref,
                       wd_ref, y_hbm, wg_s, wu_s, wd_s, xbuf0, xbuf1, obuf0, obuf1, zbuf, idx_s,
                       isem, gsem, ssem, zsem):
    s = pl.program_id(0)
    kind = kind_ref[s]
    rg = wg_ref.shape[1]
    rd = wd_ref.shape[1]
    tm = FFN_TM
    last_blk = tab_hbm.shape[0] - 1
    xbufs, obufs = (xbuf0, xbuf1), (obuf0, obuf1)

    def idx_copy(b):
        return pltpu.make_async_copy(tab_hbm.at[jnp.minimum(b, last_blk)], idx_s.at[b % FFN_IDX_SLOTS],
                                     isem.at[b % FFN_IDX_SLOTS])

    def gather_copy(b, par, r):
        tok = idx_s[b % FFN_IDX_SLOTS, r]
        return pltpu.make_async_copy(x_hbm.at[pl.ds(tok, 1), :], xbufs[par].at[pl.ds(r, 1), :], gsem.at[par])

    def scatter_copy(b, par, r):
        dst = idx_s[b % FFN_IDX_SLOTS, tm + r]
        return pltpu.make_async_copy(obufs[par].at[pl.ds(r, 1), :], y_hbm.at[pl.ds(dst, 1), :], ssem.at[par])

    @pl.when(s == 0)
    def _():
        zbuf[...] = jnp.zeros_like(zbuf)
        idx_copy(0).start()
        idx_copy(1).start()
        idx_copy(0).wait()
        for r in range(tm):
            gather_copy(0, 0, r).start()

    @pl.when(kind == _STAGE)
    def _():
        c = c_ref[s]
        r0 = pl.multiple_of(c * rg, rg)
        wg_s[pl.ds(r0, rg), :] = wg_ref[0].astype(BF16)
        wu_s[pl.ds(r0, rg), :] = wu_ref[0].astype(BF16)
        r1 = pl.multiple_of(c * rd, rd)
        wd_s[pl.ds(r1, rd), :] = wd_ref[0].astype(BF16)

    def compute(b, par, first):
        idx_copy(b + 1).wait()
        for r in range(tm):
            gather_copy(b, par, r).wait()
        if not first:
            @pl.when(b >= 2)
            def _():
                for r in range(tm):
                    scatter_copy(b - 2, par, r).wait()

            for r in range(tm):
                scatter_copy(b - 1, 1 - par, r).start()
        for r in range(tm):
            gather_copy(b + 1, 1 - par, r).start()
        idx_copy(b + 2).start()
        x = xbufs[par][...].astype(BF16)
        gate = jnp.dot(x, wg_s[...], preferred_element_type=F32)
        up = jnp.dot(x, wu_s[...], preferred_element_type=F32)
        h = (jax.nn.silu(gate) * up).astype(BF16)
        obufs[par][...] = jnp.dot(h, wd_s[...], preferred_element_type=F32) * w_ref[...]

    is_compute = kind == _COMPUTE
    blk = blk_ref[s]
    pl.when(is_compute & (blk == 0))(lambda: compute(blk, 0, True))
    pl.when(is_compute & (blk > 0) & (blk % 2 == 0))(lambda: compute(blk, 0, False))
    pl.when(is_compute & (blk % 2 == 1))(lambda: compute(blk, 1, False))

    @pl.when(kind == _ZERO)
    def _():
        row0 = pl.multiple_of(blk * tm, tm)
        tail = pltpu.make_async_copy(zbuf, y_hbm.at[pl.ds(row0, tm), :], zsem.at[0])
        tail.start()
        tail.wait()

    def drain(n_used, par):
        idx_copy(n_used + 1).wait()
        for r in range(tm):
            gather_copy(n_used, par, r).wait()
        for r in range(tm):
            scatter_copy(n_used - 1, 1 - par, r).start()
        for r in range(tm):
            scatter_copy(n_used - 2, par, r).wait()
        for r in range(tm):
            scatter_copy(n_used - 1, 1 - par, r).wait()

    is_last = s == pl.num_programs(0) - 1
    n_used = nblk_ref[0]
    pl.when(is_last & (n_used % 2 == 0))(lambda: drain(n_used, 0))
    pl.when(is_last & (n_used % 2 == 1))(lambda: drain(n_used, 1))


def _routed_ffn(x, tab, row_w, nb_e, wg, wu, wd):
    t, d = x.shape
    e, _, f = wg.shape
    n_blocks = tab.shape[0]
    row_buf = pltpu.VMEM((FFN_TM, d), F32)
    assert d % FFN_NCH == 0 and f % (FFN_NCH * 16) == 0 and tab.shape[1] == 2 * FFN_TM
    rg, rd = d // FFN_NCH, f // FFN_NCH
    kind, e_s, c_s, blk = _ffn_items(nb_e, n_blocks)
    n_used = jnp.sum(nb_e).astype(jnp.int32).reshape(1)
    chunk = lambda r, w: pl.BlockSpec((1, r, w), lambda s, k, es, cs, bs, nu: (es[s], cs[s], 0))
    grid_spec = pltpu.PrefetchScalarGridSpec(
        num_scalar_prefetch=5,
        grid=(FFN_NCH * e + n_blocks,),
        in_specs=[pl.BlockSpec(memory_space=pl.ANY), pl.BlockSpec(memory_space=pl.ANY),
                  pl.BlockSpec((FFN_TM, 1), lambda s, k, es, cs, bs, nu: (bs[s], 0)),
                  chunk(rg, f), chunk(rg, f), chunk(rd, d)],
        out_specs=pl.BlockSpec(memory_space=pl.ANY),
        scratch_shapes=[pltpu.VMEM((d, f), BF16), pltpu.VMEM((d, f), BF16), pltpu.VMEM((f, d), BF16),
                        row_buf, row_buf, row_buf, row_buf, row_buf,
                        pltpu.SMEM((FFN_IDX_SLOTS, 2 * FFN_TM), jnp.int32),
                        pltpu.SemaphoreType.DMA((FFN_IDX_SLOTS,)), pltpu.SemaphoreType.DMA((2,)),
                        pltpu.SemaphoreType.DMA((2,)), pltpu.SemaphoreType.DMA((1,))],
    )
    return pl.pallas_call(
        _routed_ffn_kernel,
        out_shape=jax.ShapeDtypeStruct((n_blocks * FFN_TM, d), F32),
        grid_spec=grid_spec,
        compiler_params=pltpu.CompilerParams(dimension_semantics=("arbitrary",), vmem_limit_bytes=VMEM_LIMIT),
        name="routed_ffn",
    )(kind, e_s, c_s, blk, n_used, x, tab, row_w, wg, wu, wd)


ML_L = 256
ML_HALO = 8
MLSTM_CONV = 4


def _mlstm_kernel(qk_ref, v_ref, o_ref, if_ref, convw_ref, convb_ref, ifb_ref, ng_ref, tril_ref,
                  out_ref, c_ref, m_ref, halo_ref):
    L, NH, DK, DV = ML_L, MLSTM_HEADS, MLSTM_QK_DIM, MLSTM_V_DIM

    @pl.when(pl.program_id(1) == 0)
    def _():
        c_ref[...] = jnp.zeros_like(c_ref)
        m_ref[...] = jnp.zeros_like(m_ref)
        halo_ref[...] = jnp.zeros_like(halo_ref)

    x = qk_ref[0]
    cat = jnp.concatenate([halo_ref[...], x], axis=0)
    halo_ref[...] = x[L - ML_HALO:, :]
    y = convb_ref[...]
    for j in range(MLSTM_CONV):
        lo = ML_HALO - (MLSTM_CONV - 1) + j
        y = y + convw_ref[j:j + 1, :] * cat[lo:lo + L, :]
    qk = y * jax.nn.sigmoid(y)

    lane = lax.broadcasted_iota(jnp.int32, (L, LANES), 1)
    z = if_ref[0] + ifb_ref[...]
    lg = jnp.where(lane < NH, z, jnp.minimum(z, 0.0) - jnp.log1p(jnp.exp(-jnp.abs(z))))
    hi = lg.astype(BF16)
    r1 = lg - hi.astype(F32)
    mid = r1.astype(BF16)
    lo3 = (r1 - mid.astype(F32)).astype(BF16)
    tril = tril_ref[...]
    bcum = (jnp.dot(tril, hi, preferred_element_type=F32) + jnp.dot(tril, mid, preferred_element_type=F32)
            + jnp.dot(tril, lo3, preferred_element_type=F32))

    row = lax.broadcasted_iota(jnp.int32, (L, L), 0)
    colm = lax.broadcasted_iota(jnp.int32, (L, L), 1)
    tri = row >= colm
    ones_col = (lane == 0).astype(BF16)
    for h in range(NH):
        li = lg[:, h:h + 1]
        b = bcum[:, NH + h:NH + h + 1]
        m_prev = m_ref[h:h + 1, 0:1]
        a_col = li - b
        a_row = jnp.transpose(jnp.broadcast_to(a_col, (L, LANES)))[0:1, :]
        dmat = jnp.where(tri, b + a_row, -jnp.inf)
        m_inter = b + m_prev
        m_t = jnp.maximum(m_inter, dmat.max(-1, keepdims=True))
        q = qk[:, h * DK:(h + 1) * DK].astype(BF16)
        k = qk[:, NH * DK + h * DK:NH * DK + (h + 1) * DK] * DK ** -0.5
        s = lax.dot_general(q, k.astype(BF16), (((1,), (1,)), ((), ())), preferred_element_type=F32)
        wqk = (jnp.exp(dmat - m_t) * s).astype(BF16)
        inter = jnp.exp(m_inter - m_t)
        v_ext = jnp.concatenate([v_ref[0, :, h * DV:(h + 1) * DV].astype(BF16), ones_col], axis=1)
        c_old = c_ref[h]
        nd = (jnp.dot(wqk, v_ext, preferred_element_type=F32)
              + inter * jnp.dot(q, c_old.astype(BF16), preferred_element_type=F32))
        num = nd[:, :DV]
        den = nd[:, DV:DV + 1]
        hh = num / jnp.maximum(jnp.abs(den), jnp.exp(-m_t))
        b_last = b[L - 1:L, :]
        g = b_last - b + li
        m_new = jnp.maximum(b_last + m_prev, g.max(0, keepdims=True))
        decay = jnp.exp(b_last + m_prev - m_new)
        w_s = jnp.exp(g - m_new)
        kw_t = jnp.transpose(k * w_s).astype(BF16)
        c_ref[h] = decay * c_old + jnp.dot(kw_t, v_ext, preferred_element_type=F32)
        m_ref[h:h + 1, :] = jnp.broadcast_to(m_new, (1, LANES))
        mu = hh.mean(-1, keepdims=True)
        var = jnp.mean(jnp.square(hh - mu), -1, keepdims=True)
        hn = (hh - mu) * lax.rsqrt(var + LN_EPS) * ng_ref[:, h * DV:(h + 1) * DV]
        out_ref[0, :, h * DV:(h + 1) * DV] = (
            jax.nn.sigmoid(o_ref[0, :, h * DV:(h + 1) * DV]) * hn).astype(out_ref.dtype)


def _mlstm(y3, conv_w, conv_b, if_bias, norm_g):
    b, s, _ = y3.shape
    L, NH = ML_L, MLSTM_HEADS
    assert s % L == 0

    def col(name, width):
        blk = _SEGS[name][0] // width
        return pl.BlockSpec((1, L, width), lambda i, j: (i, j, blk))

    full = lambda shp: pl.BlockSpec(shp, lambda i, j: (0,) * len(shp))
    ifb = jnp.pad(if_bias.reshape(1, 2 * NH), ((0, 0), (0, LANES - 2 * NH)))
    tril = jnp.asarray(np.tril(np.ones((L, L), np.float32)), dtype=BF16)
    conv_b = conv_b.reshape(1, -1)
    norm_g = norm_g.reshape(1, -1)
    return pl.pallas_call(
        _mlstm_kernel,
        out_shape=jax.ShapeDtypeStruct((b, s, ML_V_W), BF16),
        grid=(b, s // L),
        in_specs=[col("qk_ml", 2 * ML_QK_W), col("v_ml", ML_V_W), col("o_ml", ML_V_W), col("if_ml", LANES),
                  full(conv_w.shape), full(conv_b.shape), full(ifb.shape), full(norm_g.shape), full(tril.shape)],
        out_specs=pl.BlockSpec((1, L, ML_V_W), lambda i, j: (i, j, 0)),
        scratch_shapes=[pltpu.VMEM((NH, MLSTM_QK_DIM, MLSTM_V_DIM + LANES), F32),
                        pltpu.VMEM((8, LANES), F32),
                        pltpu.VMEM((ML_HALO, 2 * ML_QK_W), F32)],
        compiler_params=pltpu.CompilerParams(dimension_semantics=("parallel", "arbitrary"),
                                             vmem_limit_bytes=VMEM_LIMIT),
        name="mlstm",
    )(y3, y3, y3, y3, conv_w, conv_b, ifb, norm_g, tril)


MEM_TQ = 512


def _mem_attn_kernel(q_ref, kv_ref, o_ref):
    hd = MEM_HEAD_DIM
    for h in range(MEM_HEADS):
        q = q_ref[0, :, h * hd:(h + 1) * hd].astype(BF16)
        k = kv_ref[0, :, h * hd:(h + 1) * hd]
        v = kv_ref[0, :, MEM_W + h * hd:MEM_W + (h + 1) * hd]
        s = lax.dot_general(q, k, (((1,), (1,)), ((), ())), preferred_element_type=F32) * hd ** -0.5
        e = jnp.exp(s - s.max(-1, keepdims=True))
        p = (e / e.sum(-1, keepdims=True)).astype(BF16)
        o_ref[0, :, h * hd:(h + 1) * hd] = jnp.dot(p, v, preferred_element_type=F32).astype(o_ref.dtype)


def _mem_attention(y3, kv_mem):
    b, s, _ = y3.shape
    m = kv_mem.shape[1]
    q_blk = _SEGS["q_mem"][0] // MEM_W
    return pl.pallas_call(
        _mem_attn_kernel,
        out_shape=jax.ShapeDtypeStruct((b, s, MEM_W), BF16),
        grid=(b, s // MEM_TQ),
        in_specs=[pl.BlockSpec((1, MEM_TQ, MEM_W), lambda i, j: (i, j, q_blk)),
                  pl.BlockSpec((1, m, 2 * MEM_W), lambda i, j: (i, 0, 0))],
        out_specs=pl.BlockSpec((1, MEM_TQ, MEM_W), lambda i, j: (i, j, 0)),
        compiler_params=pltpu.CompilerParams(dimension_semantics=("parallel", "parallel"),
                                             vmem_limit_bytes=VMEM_LIMIT),
        name="mem_attention",
    )(y3, kv_mem)


MERGE_TM = 256


def _merge_kernel(yn_ref, ym_ref, yc_ref, g0_ref, g1_ref, g2_ref, x_ref, wb_ref, wo_ref, lg_ref, lb_ref,
                  x1_ref, x1b_ref):
    merged = None
    for i, (y_ref, g_ref) in enumerate(((yn_ref, g0_ref), (ym_ref, g1_ref), (yc_ref, g2_ref))):
        t = jax.nn.sigmoid(g_ref[...]) * jnp.dot(y_ref[...], wb_ref[i], preferred_element_type=F32)
        merged = t if merged is None else merged + t
    z = DN_ALPHA * x_ref[...] + jnp.dot(merged.astype(BF16), wo_ref[...], preferred_element_type=F32)
    mu = z.mean(-1, keepdims=True)
    var = jnp.mean(jnp.square(z - mu), -1, keepdims=True)
    x1 = (z - mu) * lax.rsqrt(var + LN_EPS) * lg_ref[...] + lb_ref[...]
    x1_ref[...] = x1
    x1b_ref[...] = x1.astype(BF16)


def _merge_ln1(y_nsa, y_ml, y_mem, y2, xt, wb, wo, ln_g, ln_b):
    t, d = xt.shape
    tm = MERGE_TM
    bw = y_nsa.shape[1]
    resident = pl.Buffered(1)
    row = lambda w: pl.BlockSpec((tm, w), lambda i: (i, 0))
    gate = lambda k: pl.BlockSpec((tm, d), lambda i: (i, k))
    return pl.pallas_call(
        _merge_kernel,
        out_shape=(jax.ShapeDtypeStruct((t, d), F32), jax.ShapeDtypeStruct((t, d), BF16)),
        grid=(t // tm,),
        in_specs=[row(bw), row(bw), row(bw), gate(0), gate(1), gate(2), row(d),
                  pl.BlockSpec((N_BRANCH, bw, d), lambda i: (0, 0, 0), pipeline_mode=resident),
                  pl.BlockSpec((d, d), lambda i: (0, 0), pipeline_mode=resident),
                  pl.BlockSpec((1, d), lambda i: (0, 0)), pl.BlockSpec((1, d), lambda i: (0, 0))],
        out_specs=(row(d), row(d)),
        compiler_params=pltpu.CompilerParams(dimension_semantics=("parallel",), vmem_limit_bytes=VMEM_LIMIT),
        name="merge_ln1",
    )(y_nsa, y_ml, y_mem, y2, y2, y2, xt, wb, wo, ln_g.reshape(1, d), ln_b.reshape(1, d))


ROUTER_TM = 256


def _router_kernel(x_ref, w_ref, bias_ref, idx_ref, wt_ref, cnt_ref, run_ref):
    tm = ROUTER_TM

    @pl.when(pl.program_id(0) == 0)
    def _():
        run_ref[...] = jnp.zeros_like(run_ref)

    x = x_ref[...]
    w = w_ref[...]
    x_hi = x.astype(BF16)
    x_lo = (x - x_hi.astype(F32)).astype(BF16)
    w_hi = w.astype(BF16)
    w_lo = (w - w_hi.astype(F32)).astype(BF16)
    logits = (jnp.dot(x_hi, w_hi, preferred_element_type=F32) + jnp.dot(x_lo, w_hi, preferred_element_type=F32)
              + jnp.dot(x_hi, w_lo, preferred_element_type=F32))
    lane = lax.broadcasted_iota(jnp.int32, (tm, LANES), 1)
    aff = jax.nn.sigmoid(logits)
    score = jnp.where(lane < N_EXPERTS, aff + bias_ref[...], -jnp.inf)
    idx_out = jnp.zeros((tm, LANES), jnp.int32)
    aff_out = jnp.zeros((tm, LANES), F32)
    chosen = jnp.zeros((tm, LANES), jnp.bool_)
    for k in range(TOP_K):
        best = score.max(-1, keepdims=True)
        cur = jnp.where(score == best, lane, LANES).min(-1, keepdims=True)
        here = lane == cur
        idx_out = jnp.where(lane == k, cur, idx_out)
        aff_out = jnp.where(lane == k, jnp.where(here, aff, 0.0).sum(-1, keepdims=True), aff_out)
        chosen = chosen | here
        score = jnp.where(here, -jnp.inf, score)
    run_new = run_ref[0:1, :] + chosen.astype(F32).sum(0, keepdims=True)
    run_ref[...] = jnp.broadcast_to(run_new, run_ref.shape)
    cnt_ref[...] = jnp.broadcast_to(run_new, cnt_ref.shape)
    idx_ref[...] = idx_out
    wt_ref[...] = aff_out / aff_out.sum(-1, keepdims=True) * ROUTED_SCALE


def _router(x1, router_w, router_bias):
    t, d = x1.shape
    tm = ROUTER_TM
    e = router_w.shape[1]
    w = jnp.pad(router_w, ((0, 0), (0, LANES - e)))
    bias = jnp.pad(router_bias.reshape(1, e), ((0, 0), (0, LANES - e)))
    out = pl.BlockSpec((tm, LANES), lambda i: (i, 0))
    const = lambda shp: pl.BlockSpec(shp, lambda i: (0, 0))
    idx, wt, cnt = pl.pallas_call(
        _router_kernel,
        out_shape=(jax.ShapeDtypeStruct((t, LANES), jnp.int32), jax.ShapeDtypeStruct((t, LANES), F32),
                   jax.ShapeDtypeStruct((8, LANES), F32)),
        grid=(t // tm,),
        in_specs=[pl.BlockSpec((tm, d), lambda i: (i, 0)), const(w.shape), const(bias.shape)],
        out_specs=(out, out, const((8, LANES))),
        scratch_shapes=[pltpu.VMEM((8, LANES), F32)],
        compiler_params=pltpu.CompilerParams(dimension_semantics=("arbitrary",), vmem_limit_bytes=VMEM_LIMIT),
        name="router",
    )(x1, w, bias)
    return idx[:, :TOP_K], wt[:, :TOP_K], cnt[0, :e].astype(jnp.int32)


def _moe_layout(idx, wts, counts):
    t = idx.shape[0]
    e = N_EXPERTS
    m = t * TOP_K
    assert m // FFN_TM >= 2
    nb_e = (counts + FFN_TM - 1) // FFN_TM
    blk_start = jnp.cumsum(nb_e) - nb_e
    start = jnp.cumsum(counts) - counts
    n_blocks = m // FFN_TM + e
    order = jnp.argsort(idx.reshape(m))
    bi = jnp.arange(n_blocks)
    blk_e = jnp.clip((bi[:, None] >= blk_start[None, :]).sum(1) - 1, 0, e - 1)
    j = ((bi - blk_start[blk_e]) * FFN_TM)[:, None] + jnp.arange(FFN_TM)[None, :]
    valid = j < counts[blk_e][:, None]
    flat = order[jnp.clip(start[blk_e][:, None] + j, 0, m - 1)]
    tok, slot = flat // TOP_K, flat % TOP_K
    pad_before = (blk_start * FFN_TM - start)[blk_e] - counts[blk_e]
    row_tok = jnp.where(valid, tok, 0)
    dest = jnp.where(valid, slot * t + tok, m + pad_before[:, None] + j)
    row_w = jnp.where(valid, wts.reshape(m)[flat], 0.0).reshape(n_blocks * FFN_TM, 1)
    tab = jnp.concatenate([row_tok, dest], axis=1).astype(jnp.int32)
    return tab, row_w, nb_e


def _combine_kernel(x_ref, *refs):
    slot_refs, (s_ref, g_ref, b_ref, o_ref) = refs[:TOP_K], refs[TOP_K:]
    routed = slot_refs[0][...]
    for r_ref in slot_refs[1:]:
        routed = routed + r_ref[...]
    z = DN_ALPHA * x_ref[...] + (routed + s_ref[...])
    mu = z.mean(-1, keepdims=True)
    var = jnp.mean(jnp.square(z - mu), -1, keepdims=True)
    o_ref[...] = (z - mu) * lax.rsqrt(var + LN_EPS) * g_ref[...] + b_ref[...]


def _combine_ln2(x1, y_slots, shared, ln_g, ln_b, *, tm=256):
    t, d = x1.shape
    nt = t // tm
    row = pl.BlockSpec((tm, d), lambda i: (i, 0))
    vec = pl.BlockSpec((1, d), lambda i: (0, 0))
    slots = [pl.BlockSpec((tm, d), functools.partial(lambda i, k: (k * nt + i, 0), k=k)) for k in range(TOP_K)]
    return pl.pallas_call(
        _combine_kernel, out_shape=jax.ShapeDtypeStruct((t, d), F32), grid=(nt,),
        in_specs=[row] + slots + [row, vec, vec], out_specs=row,
        compiler_params=pltpu.CompilerParams(dimension_semantics=("parallel",), vmem_limit_bytes=VMEM_LIMIT),
        name="combine_ln2",
    )(x1, *([y_slots] * TOP_K), shared, ln_g.reshape(1, d), ln_b.reshape(1, d))


def _layer(x, mem, w_in, mlstm_conv_w, mlstm_conv_b, mlstm_if_bias, mlstm_norm_g,
           cmp_pos_k, cmp_k_w1, cmp_k_w2, cmp_pos_v, cmp_v_w1, cmp_v_w2,
           w_mem_kv, w_branch, w_o, ln1_g, ln1_b, router_w, router_bias,
           w_e_gate, w_e_up, w_e_down, w_s_gate, w_s_up, w_s_down, ln2_g, ln2_b):
    B, S, D = x.shape
    T = B * S
    xt = x.reshape(T, D)
    y2 = _matmul(xt.astype(BF16), _pad_in_proj(w_in).astype(BF16), tm=1024, tn=IN_TN, name="in_proj")
    y3 = y2.reshape(B, S, IN_PAD)

    cos, sin = _rope_tables(jnp.arange(S))
    cos_e, sin_e = _rope_tables(jnp.arange(S // CMP_STRIDE) * CMP_STRIDE + CMP_LEN - 1)
    kvc, ks, vs, kw, vw = _nsa_prep(y3, cos, sin)
    kc, vc = _compress(kvc, cmp_pos_k, cmp_k_w1, cmp_k_w2, cmp_pos_v, cmp_v_w1, cmp_v_w2, cos_e, sin_e)
    y_nsa = _nsa_attention(y3, kc, vc, ks, vs, kw, vw, cos, sin)

    y_ml = _mlstm(y3, mlstm_conv_w, mlstm_conv_b, mlstm_if_bias, mlstm_norm_g)

    m_len = mem.shape[1]
    kv_mem = _matmul(mem.reshape(B * m_len, D).astype(BF16), w_mem_kv.astype(BF16), tm=1024, tn=512,
                     out_dtype=BF16, name="mem_kv").reshape(B, m_len, 2 * MEM_W)
    y_mem = _mem_attention(y3, kv_mem)

    x1, x1b = _merge_ln1(y_nsa.reshape(T, -1), y_ml.reshape(T, -1), y_mem.reshape(T, -1), y2, xt,
                         w_branch.astype(BF16), w_o.astype(BF16), ln1_g, ln1_b)

    idx, wts, counts = _router(x1, router_w, router_bias)
    tab, row_w, nb_e = _moe_layout(idx, wts, counts)
    y_slots = _routed_ffn(x1, tab, row_w, nb_e, w_e_gate, w_e_up, w_e_down)
    shared = _grouped_ffn(x1b, jnp.ones((T, 1), F32), jnp.full((1,), T // FFN_TM, jnp.int32),
                          w_s_gate[None], w_s_up[None], w_s_down[None])
    return _combine_ln2(x1, y_slots, shared, ln2_g, ln2_b).reshape(B, S, D)


def kernel(x, mem, w_in, mlstm_conv_w, mlstm_conv_b, mlstm_if_bias, mlstm_norm_g, cmp_pos_k, cmp_k_w1, cmp_k_w2, cmp_pos_v, cmp_v_w1, cmp_v_w2, w_mem_kv, w_branch, w_o, ln1_g, ln1_b, router_w, router_bias, w_e_gate, w_e_up, w_e_down, w_s_gate, w_s_up, w_s_down, ln2_g, ln2_b):
    for layer in range(DEPTH):
        x = _layer(x, mem, w_in[layer], mlstm_conv_w[layer], mlstm_conv_b[layer],
                   mlstm_if_bias[layer], mlstm_norm_g[layer],
                   cmp_pos_k[layer], cmp_k_w1[layer], cmp_k_w2[layer],
                   cmp_pos_v[layer], cmp_v_w1[layer], cmp_v_w2[layer],
                   w_mem_kv[layer], w_branch[layer], w_o[layer], ln1_g[layer], ln1_b[layer],
                   router_w[layer], router_bias[layer], w_e_gate[layer], w_e_up[layer],
                   w_e_down[layer], w_s_gate[layer], w_s_up[layer], w_s_down[layer],
                   ln2_g[layer], ln2_b[layer])
    return x
```

```python
import functools

import jax
import jax.numpy as jnp
import numpy as np
from jax import lax
from jax.experimental import pallas as pl
from jax.experimental.pallas import tpu as pltpu

F32 = jnp.float32
BF16 = jnp.bfloat16

D_MODEL = 2048
HEAD_DIM = 128
NSA_HEADS = 8
NSA_KV_GROUPS = 2
NSA_REP = NSA_HEADS // NSA_KV_GROUPS
CMP_LEN = 32
CMP_STRIDE = 16
SLC_LEN = 64
SLC_TOPK = 16
WINDOW = 512
MLSTM_HEADS = 4
MLSTM_QK_DIM = 128
MLSTM_V_DIM = 256
MLSTM_CHUNK = 64
MEM_HEADS = 4
MEM_HEAD_DIM = 256
N_EXPERTS = 64
TOP_K = 6
EXPERT_DIM = 1408
ROUTED_SCALE = 2.5
ROPE_THETA = 10000.0
LN_EPS = 1e-5
NEG_INF = -1e30
DEPTH = 1
DN_ALPHA = (2 * DEPTH) ** 0.25

NSA_Q_W = NSA_HEADS * HEAD_DIM
NSA_KV_W = NSA_KV_GROUPS * HEAD_DIM
ML_QK_W = MLSTM_HEADS * MLSTM_QK_DIM
ML_V_W = MLSTM_HEADS * MLSTM_V_DIM
MEM_W = MEM_HEADS * MEM_HEAD_DIM
N_BRANCH = 3

LANES = 128
VMEM_LIMIT = 56 * 1024 * 1024

_SEGS = {}
_off = 0
for _name, _w, _blk in (("g_br", N_BRANCH * D_MODEL, D_MODEL), ("q_nsa", NSA_Q_W, NSA_REP * HEAD_DIM),
                        ("q_mem", MEM_W, MEM_W), ("v_ml", ML_V_W, ML_V_W), ("o_ml", ML_V_W, ML_V_W),
                        ("qk_ml", 2 * ML_QK_W, 2 * ML_QK_W), ("kv_cmp", 2 * NSA_KV_W, 2 * NSA_KV_W),
                        ("k_slc", NSA_KV_W, NSA_KV_W), ("v_slc", NSA_KV_W, NSA_KV_W),
                        ("k_win", NSA_KV_W, NSA_KV_W), ("v_win", NSA_KV_W, NSA_KV_W),
                        ("g_nsa", NSA_KV_GROUPS * LANES, LANES), ("if_ml", LANES, LANES)):
    assert _off % _blk == 0
    _SEGS[_name] = (_off, _w)
    _off += _w
IN_TN = 512
IN_PAD = -(-_off // IN_TN) * IN_TN


def _pad_in_proj(w_in):
    sizes = (NSA_Q_W,) + (NSA_KV_W,) * 6 + (NSA_HEADS * 3, ML_QK_W, ML_QK_W, ML_V_W, ML_V_W,
                                            MLSTM_HEADS, MLSTM_HEADS, MEM_W, N_BRANCH * D_MODEL)
    splits = np.cumsum(sizes)[:-1].tolist()
    (q_nsa, k_cmp, v_cmp, k_slc, v_slc, k_win, v_win, g_nsa,
     q_ml, k_ml, v_ml, o_ml, i_ml, f_ml, q_mem, g_br) = jnp.split(w_in, splits, axis=-1)
    per_group = NSA_REP * 3
    g_blocks = [jnp.pad(g_nsa[:, g * per_group:(g + 1) * per_group], ((0, 0), (0, LANES - per_group)))
                for g in range(NSA_KV_GROUPS)]
    if_blk = jnp.pad(jnp.concatenate([i_ml, f_ml], -1), ((0, 0), (0, LANES - 2 * MLSTM_HEADS)))
    cols = [g_br, q_nsa, q_mem, v_ml, o_ml, q_ml, k_ml, k_cmp, v_cmp,
            k_slc, v_slc, k_win, v_win] + g_blocks + [if_blk]
    w = jnp.concatenate(cols, axis=-1)
    return jnp.pad(w, ((0, 0), (0, IN_PAD - w.shape[1])))


def _mm_kernel(a_ref, b_ref, o_ref):
    o_ref[...] = jnp.dot(a_ref[...].astype(BF16), b_ref[...].astype(BF16),
                         preferred_element_type=F32).astype(o_ref.dtype)


def _matmul(a, b, *, tm, tn, out_dtype=F32, name="matmul"):
    m, k = a.shape
    _, n = b.shape
    assert m % tm == 0 and n % tn == 0
    return pl.pallas_call(
        _mm_kernel,
        out_shape=jax.ShapeDtypeStruct((m, n), out_dtype),
        grid=(m // tm, n // tn),
        in_specs=[pl.BlockSpec((tm, k), lambda i, j: (i, 0)),
                  pl.BlockSpec((k, tn), lambda i, j: (0, j))],
        out_specs=pl.BlockSpec((tm, tn), lambda i, j: (i, j)),
        compiler_params=pltpu.CompilerParams(dimension_semantics=("parallel", "arbitrary"),
                                             vmem_limit_bytes=VMEM_LIMIT),
        name=name,
    )(a, b)


def _rope_tables(pos):
    half = HEAD_DIM // 2
    inv = ROPE_THETA ** (-jnp.arange(half, dtype=F32) / half)
    ang = pos.astype(F32)[:, None] * inv[None, :]
    cos, sin = jnp.cos(ang), jnp.sin(ang)
    return jnp.concatenate([cos, cos], -1), jnp.concatenate([-sin, sin], -1)


def _rope(x, cos, sin_signed):
    return x * cos + pltpu.roll(x, HEAD_DIM // 2, 1) * sin_signed


def _nsa_prep_kernel(kvc_ref, ks_ref, vs_ref, kw_ref, vw_ref, cos_ref, sin_ref,
                     kvc_o, ks_o, vs_o, kw_o, vw_o):
    cos, sin = cos_ref[...], sin_ref[...]
    kvc_o[0] = kvc_ref[0]
    for src, dst in ((ks_ref, ks_o), (kw_ref, kw_o)):
        for g in range(NSA_KV_GROUPS):
            sl = slice(g * HEAD_DIM, (g + 1) * HEAD_DIM)
            dst[0, :, sl] = _rope(src[0, :, sl], cos, sin).astype(BF16)
    vs_o[0] = vs_ref[0].astype(BF16)
    vw_o[0] = vw_ref[0].astype(BF16)


def _nsa_prep(y3, cos, sin, *, ts=512):
    b, s, _ = y3.shape
    kvw = NSA_KV_W

    def col(name, width):
        blk = _SEGS[name][0] // width
        return pl.BlockSpec((1, ts, width), lambda i, j: (i, j, blk))

    tab = pl.BlockSpec((ts, HEAD_DIM), lambda i, j: (j, 0))
    out_kv = pl.BlockSpec((1, ts, kvw), lambda i, j: (i, j, 0))
    return pl.pallas_call(
        _nsa_prep_kernel,
        out_shape=(jax.ShapeDtypeStruct((b, s, 2 * kvw), F32),) + (jax.ShapeDtypeStruct((b, s, kvw), BF16),) * 4,
        grid=(b, s // ts),
        in_specs=[col("kv_cmp", 2 * kvw), col("k_slc", kvw), col("v_slc", kvw), col("k_win", kvw),
                  col("v_win", kvw), tab, tab],
        out_specs=(pl.BlockSpec((1, ts, 2 * kvw), lambda i, j: (i, j, 0)), out_kv, out_kv, out_kv, out_kv),
        compiler_params=pltpu.CompilerParams(dimension_semantics=("parallel", "parallel")),
        name="nsa_prep",
    )(y3, y3, y3, y3, y3, cos, sin)


def _cmp_kernel(u_ref, posa_ref, posb_ref, w1a_ref, w1b_ref, w2_ref, cos_ref, sin_ref, kc_o, vc_o):
    n_rows = u_ref.shape[2]
    for idx, out in ((0, kc_o), (1, vc_o)):
        u = u_ref[0, idx]
        a = jnp.dot((u + posa_ref[idx]).astype(BF16), w1a_ref[idx], preferred_element_type=F32)
        bm = jnp.dot((u + posb_ref[idx]).astype(BF16), w1b_ref[idx], preferred_element_type=F32)
        pre = a + pltpu.roll(bm, n_rows - 1, 0)
        h = jax.nn.gelu(pre).astype(BF16)
        c = jnp.dot(h, w2_ref[idx], preferred_element_type=F32)
        if idx == 0:
            for g in range(NSA_KV_GROUPS):
                sl = slice(g * HEAD_DIM, (g + 1) * HEAD_DIM)
                out[0, :, sl] = _rope(c[:, sl], cos_ref[...], sin_ref[...]).astype(BF16)
        else:
            out[0] = c.astype(BF16)


def _block_diag_groups(w):
    z = jnp.zeros_like(w)
    rows = [jnp.concatenate([w if gi == go else z for go in range(NSA_KV_GROUPS)], -1)
            for gi in range(NSA_KV_GROUPS)]
    return jnp.concatenate(rows, -2)


def _compress(kvc, cmp_pos_k, cmp_k_w1, cmp_k_w2, cmp_pos_v, cmp_v_w1, cmp_v_w2, cos_e, sin_e):
    b, s, _ = kvc.shape
    n_rows = s // CMP_STRIDE
    g, hd = NSA_KV_GROUPS, HEAD_DIM
    row_w = CMP_STRIDE * g * hd
    u = kvc.reshape(b, n_rows, CMP_STRIDE, 2, g * hd).transpose(0, 3, 1, 2, 4).reshape(b, 2, n_rows, row_w)

    def pos_rows(pos):
        p = jnp.broadcast_to(pos.reshape(2, CMP_STRIDE, 1, hd), (2, CMP_STRIDE, g, hd))
        return p.reshape(2, 1, row_w)

    def w1_halves(w1):
        w = w1.reshape(2, CMP_STRIDE, hd, hd)
        return _block_diag_groups(w).reshape(2, row_w, g * hd).astype(BF16)

    pk, pv = pos_rows(cmp_pos_k), pos_rows(cmp_pos_v)
    wk, wv = w1_halves(cmp_k_w1), w1_halves(cmp_v_w1)
    posa = jnp.stack([pk[0], pv[0]])
    posb = jnp.stack([pk[1], pv[1]])
    w1a = jnp.stack([wk[0], wv[0]])
    w1b = jnp.stack([wk[1], wv[1]])
    w2 = jnp.stack([_block_diag_groups(cmp_k_w2), _block_diag_groups(cmp_v_w2)]).astype(BF16)
    full = lambda shp: pl.BlockSpec(shp, lambda i: (0,) * len(shp))
    out_spec = pl.BlockSpec((1, n_rows, g * hd), lambda i: (i, 0, 0))
    return pl.pallas_call(
        _cmp_kernel,
        out_shape=(jax.ShapeDtypeStruct((b, n_rows, g * hd), BF16),) * 2,
        grid=(b,),
        in_specs=[pl.BlockSpec((1, 2, n_rows, row_w), lambda i: (i, 0, 0, 0)),
                  full(posa.shape), full(posb.shape), full(w1a.shape), full(w1b.shape), full(w2.shape),
                  full(cos_e.shape), full(sin_e.shape)],
        out_specs=(out_spec, out_spec),
        compiler_params=pltpu.CompilerParams(dimension_semantics=("parallel",), vmem_limit_bytes=VMEM_LIMIT),
        name="nsa_compress",
    )(u, posa, posb, w1a, w1b, w2, cos_e, sin_e)


NSA_TQ = 256
NSA_TK = 256


def _flash(q4, k_ref, v_ref, kt_lo, kt_hi, bias_fn, acc_ref):
    rows = q4.shape[0]
    tq = rows // NSA_REP
    scale = HEAD_DIM ** -0.5
    acc_ref[...] = jnp.zeros_like(acc_ref)

    def body(kt, carry):
        m, l = carry
        k0 = pl.multiple_of(kt * NSA_TK, NSA_TK)
        k = k_ref[0, pl.ds(k0, NSA_TK), :]
        v = v_ref[0, pl.ds(k0, NSA_TK), :]
        s = lax.dot_general(q4, k, (((1,), (1,)), ((), ())), preferred_element_type=F32) * scale
        s = s.reshape(NSA_REP, tq, NSA_TK) + bias_fn(k0)[None]
        m_new = jnp.maximum(m, s.max(-1, keepdims=True))
        alpha = jnp.exp(m - m_new)
        p = jnp.exp(s - m_new)
        l = alpha * l + p.sum(-1, keepdims=True)
        pv = jnp.dot(p.reshape(rows, NSA_TK).astype(BF16), v, preferred_element_type=F32)
        acc_ref[...] = acc_ref[...] * alpha.reshape(rows, 1) + pv
        return m_new, l

    init = (jnp.full((NSA_REP, tq, 1), NEG_INF, F32), jnp.zeros((NSA_REP, tq, 1), F32))
    _, l = lax.fori_loop(kt_lo, kt_hi, body, init)
    return acc_ref[...] / l.reshape(rows, 1)


def _nsa_kernel(q_ref, gate_ref, kc_ref, vc_ref, ks_ref, vs_ref, kw_ref, vw_ref, cos_ref, sin_ref,
                cover_ref, o_ref, acc_ref, *, n_slc):
    qi = pl.program_id(2)
    tq = NSA_TQ
    rows = NSA_REP * tq
    q0 = qi * tq
    cos, sin = cos_ref[...], sin_ref[...]
    q4 = jnp.concatenate(
        [_rope(q_ref[0, :, r * HEAD_DIM:(r + 1) * HEAD_DIM], cos, sin).astype(BF16) for r in range(NSA_REP)],
        axis=0)
    tpos = q0 + lax.broadcasted_iota(jnp.int32, (tq, LANES), 0)
    lane = lax.broadcasted_iota(jnp.int32, (tq, LANES), 1)

    s = lax.dot_general(q4, kc_ref[0], (((1,), (1,)), ((), ())), preferred_element_type=F32)
    s = (s * HEAD_DIM ** -0.5).reshape(NSA_REP, tq, LANES)
    cmask = (lane * CMP_STRIDE + (CMP_LEN - 1) <= tpos)[None]
    s = jnp.where(cmask, s, NEG_INF)
    e = jnp.exp(s - s.max(-1, keepdims=True))
    p = e / e.sum(-1, keepdims=True) * cmask.astype(F32)
    o_cmp = jnp.dot(p.reshape(rows, LANES).astype(BF16), vc_ref[0], preferred_element_type=F32)

    p_sum_t = jnp.transpose(p.sum(0))
    p_hi = p_sum_t.astype(BF16)
    p_lo = (p_sum_t - p_hi.astype(F32)).astype(BF16)
    cover_t = cover_ref[...]
    p_slc = (jnp.dot(cover_t, p_hi, preferred_element_type=F32)
             + jnp.dot(cover_t, p_lo, preferred_element_type=F32))
    blk = lax.broadcasted_iota(jnp.int32, (n_slc, tq), 0)
    t_blk = (q0 + lax.broadcasted_iota(jnp.int32, (n_slc, tq), 1)) // SLC_LEN
    forced = (blk == 0) | (blk == t_blk) | (blk == t_blk - 1)
    score = jnp.where(forced, jnp.inf, jnp.where(blk > t_blk, -jnp.inf, p_slc))
    cnt = jnp.zeros((n_slc, tq), jnp.int32)
    for i in range(n_slc):
        other = score[i:i + 1, :]
        ahead = (other > score) | ((other == score) & (blk > i))
        cnt = cnt + ahead.astype(jnp.int32)
    chosen = (cnt < min(SLC_TOPK, n_slc)) & (score > -jnp.inf)
    pow2 = jnp.left_shift(1, blk & 15).astype(F32)
    halves = []
    for lo_blk in (0, 16):
        half = jnp.where(chosen & (blk >= lo_blk) & (blk < lo_blk + 16), pow2, 0.0).sum(0, keepdims=True)
        halves.append(jnp.transpose(jnp.broadcast_to(half, (LANES, tq)))[:, 0:1].astype(jnp.int32))
    bits = halves[0] | (halves[1] << 16)

    tpos_k = q0 + lax.broadcasted_iota(jnp.int32, (tq, NSA_TK), 0)
    kiota = lax.broadcasted_iota(jnp.int32, (tq, NSA_TK), 1)
    bits_b = jnp.broadcast_to(bits, (tq, NSA_TK))

    def slc_bias(k0):
        kpos = k0 + kiota
        picked = (lax.shift_right_logical(bits_b, kpos // SLC_LEN) & 1) == 1
        return jnp.where(picked & (kpos <= tpos_k), 0.0, NEG_INF)

    o_slc = _flash(q4, ks_ref, vs_ref, 0, (q0 + tq) // NSA_TK, slc_bias, acc_ref)

    def win_bias(k0):
        kpos = k0 + kiota
        return jnp.where((kpos <= tpos_k) & (kpos > tpos_k - WINDOW), 0.0, NEG_INF)

    kt_lo = jnp.maximum(q0 - WINDOW, 0) // NSA_TK
    o_win = _flash(q4, kw_ref, vw_ref, kt_lo, (q0 + tq) // NSA_TK, win_bias, acc_ref)

    gate = jax.nn.sigmoid(gate_ref[0])
    for r in range(NSA_REP):
        rs = slice(r * tq, (r + 1) * tq)
        o = (gate[:, 3 * r:3 * r + 1] * o_cmp[rs] + gate[:, 3 * r + 1:3 * r + 2] * o_slc[rs]
             + gate[:, 3 * r + 2:3 * r + 3] * o_win[rs])
        o_ref[0, :, r * HEAD_DIM:(r + 1) * HEAD_DIM] = o.astype(o_ref.dtype)


def _cover_matrix(n_rows, n_slc):
    i = np.arange(n_rows)[None, :]
    j = np.arange(n_slc)[:, None]
    c0 = i * CMP_STRIDE
    c1 = c0 + CMP_LEN - 1
    s0 = j * SLC_LEN
    s1 = s0 + SLC_LEN - 1
    ok = (c1 >= s0) & (c0 <= s1) & (i < n_rows - 1)
    return jnp.asarray(ok.astype(np.float32), dtype=BF16)


def _nsa_attention(y3, kc, vc, ks, vs, kw, vw, cos, sin):
    b, s, _ = y3.shape
    g, hd, tq = NSA_KV_GROUPS, HEAD_DIM, NSA_TQ
    n_rows = s // CMP_STRIDE
    assert n_rows == LANES and s // SLC_LEN <= 32 and s % tq == 0
    cover = _cover_matrix(n_rows, s // SLC_LEN)
    qw = NSA_REP * hd
    gate_blk = _SEGS["g_nsa"][0] // LANES
    q_blk = _SEGS["q_nsa"][0] // qw
    seq_g = pl.BlockSpec((1, s, hd), lambda i, j, t: (i, 0, j))
    cmp_g = pl.BlockSpec((1, n_rows, hd), lambda i, j, t: (i, 0, j))
    tab = pl.BlockSpec((tq, hd), lambda i, j, t: (t, 0))
    return pl.pallas_call(
        functools.partial(_nsa_kernel, n_slc=s // SLC_LEN),
        out_shape=jax.ShapeDtypeStruct((b, s, NSA_Q_W), BF16),
        grid=(b, g, s // tq),
        in_specs=[pl.BlockSpec((1, tq, qw), lambda i, j, t: (i, t, q_blk + j)),
                  pl.BlockSpec((1, tq, LANES), lambda i, j, t: (i, t, gate_blk + j)),
                  cmp_g, cmp_g, seq_g, seq_g, seq_g, seq_g, tab, tab,
                  pl.BlockSpec(cover.shape, lambda i, j, t: (0, 0))],
        out_specs=pl.BlockSpec((1, tq, qw), lambda i, j, t: (i, t, j)),
        scratch_shapes=[pltpu.VMEM((NSA_REP * tq, hd), F32)],
        compiler_params=pltpu.CompilerParams(dimension_semantics=("parallel", "parallel", "arbitrary"),
                                             vmem_limit_bytes=VMEM_LIMIT),
        name="nsa_attention",
    )(y3, y3, kc, vc, ks, vs, kw, vw, cos, sin, cover)


FFN_TM = 256
FFN_NCH = 8
_STAGE, _COMPUTE, _ZERO = 0, 1, 2


def _ffn_kernel(kind_ref, e_ref, c_ref, blk_ref, x_ref, w_ref, wg_ref, wu_ref, wd_ref, o_ref,
                wg_s, wu_s, wd_s):
    s = pl.program_id(0)
    kind = kind_ref[s]
    rg = wg_ref.shape[1]
    rd = wd_ref.shape[1]

    @pl.when(kind == _STAGE)
    def _():
        c = c_ref[s]
        r0 = pl.multiple_of(c * rg, rg)
        wg_s[pl.ds(r0, rg), :] = wg_ref[0].astype(BF16)
        wu_s[pl.ds(r0, rg), :] = wu_ref[0].astype(BF16)
        r1 = pl.multiple_of(c * rd, rd)
        wd_s[pl.ds(r1, rd), :] = wd_ref[0].astype(BF16)

    @pl.when(kind == _COMPUTE)
    def _():
        x = x_ref[...]
        gate = jnp.dot(x, wg_s[...], preferred_element_type=F32)
        up = jnp.dot(x, wu_s[...], preferred_element_type=F32)
        h = (jax.nn.silu(gate) * up).astype(BF16)
        o_ref[...] = jnp.dot(h, wd_s[...], preferred_element_type=F32) * w_ref[...]

    @pl.when(kind == _ZERO)
    def _():
        o_ref[...] = jnp.zeros_like(o_ref)


def _ffn_items(nb_e, n_blocks):
    e = nb_e.shape[0]
    blk_start = jnp.cumsum(nb_e) - nb_e
    n_used = blk_start[-1] + nb_e[-1]
    base = FFN_NCH * jnp.arange(e) + blk_start
    s = jnp.arange(FFN_NCH * e + n_blocks)
    e_s = (s[:, None] >= base[None, :]).sum(1) - 1
    r = s - base[e_s]
    end = FFN_NCH * e + n_used
    kind = jnp.where(s >= end, _ZERO, jnp.where(r < FFN_NCH, _STAGE, _COMPUTE))
    c_s = jnp.minimum(r, FFN_NCH - 1)
    blk = jnp.where(kind == _ZERO, n_used + s - end,
                    jnp.where(kind == _STAGE, blk_start[e_s], blk_start[e_s] + r - FFN_NCH))
    blk = jnp.minimum(blk, n_blocks - 1)
    i32 = lambda a: a.astype(jnp.int32)
    return i32(kind), i32(e_s), i32(c_s), i32(blk)


def _grouped_ffn(xs, row_w, nb_e, wg, wu, wd):
    p, d = xs.shape
    e, _, f = wg.shape
    n_blocks = p // FFN_TM
    assert d % FFN_NCH == 0 and f % (FFN_NCH * 16) == 0
    rg, rd = d // FFN_NCH, f // FFN_NCH
    kind, e_s, c_s, blk = _ffn_items(nb_e, n_blocks)
    row = lambda w: pl.BlockSpec((FFN_TM, w), lambda s, k, es, cs, bs: (bs[s], 0))
    chunk = lambda r, w: pl.BlockSpec((1, r, w), lambda s, k, es, cs, bs: (es[s], cs[s], 0))
    grid_spec = pltpu.PrefetchScalarGridSpec(
        num_scalar_prefetch=4,
        grid=(FFN_NCH * e + n_blocks,),
        in_specs=[row(d), row(1), chunk(rg, f), chunk(rg, f), chunk(rd, d)],
        out_specs=row(d),
        scratch_shapes=[pltpu.VMEM((d, f), BF16), pltpu.VMEM((d, f), BF16), pltpu.VMEM((f, d), BF16)],
    )
    return pl.pallas_call(
        _ffn_kernel,
        out_shape=jax.ShapeDtypeStruct((p, d), F32),
        grid_spec=grid_spec,
        compiler_params=pltpu.CompilerParams(dimension_semantics=("arbitrary",), vmem_limit_bytes=VMEM_LIMIT),
        name="grouped_ffn",
    )(kind, e_s, c_s, blk, xs, row_w, wg, wu, wd)


FFN_IDX_SLOTS = 4
FFN_ROW_PITCH = 24


def _routed_ffn_kernel(kind_ref, e_ref, c_ref, blk_ref, nblk_ref, x_hbm, tab_hbm, w_ref, wg_ref, wu_ref,
                       wd_ref, y_hbm, wg_s, wu_s, wd_s, xbuf0, xbuf1, obuf0, obuf1, zbuf, idx_s,
                       isem, gsem, ssem, zsem):
    s = pl.program_id(0)
    kind = kind_ref[s]
    rg = wg_ref.shape[1]
    rd = wd_ref.shape[1]
    tm = FFN_TM
    last_blk = tab_hbm.shape[0] - 1
    ns = wd_ref.shape[2] // LANES
    xbufs, obufs = (xbuf0, xbuf1), (obuf0, obuf1)

    def idx_copy(b):
        return pltpu.make_async_copy(tab_hbm.at[jnp.minimum(b, last_blk)], idx_s.at[b % FFN_IDX_SLOTS],
                                     isem.at[b % FFN_IDX_SLOTS])

    def gather_copy(b, par, r):
        tok = idx_s[b % FFN_IDX_SLOTS, r]
        return pltpu.make_async_copy(x_hbm.at[pl.ds(pl.multiple_of(tok * ns, ns), ns), :],
                                     xbufs[par].at[pl.ds(r * FFN_ROW_PITCH, ns), :], gsem.at[par])

    def scatter_copy(b, par, r):
        dst = idx_s[b % FFN_IDX_SLOTS, tm + r]
        return pltpu.make_async_copy(obufs[par].at[pl.ds(r * FFN_ROW_PITCH, ns), :],
                                     y_hbm.at[pl.ds(pl.multiple_of(dst * ns, ns), ns), :], ssem.at[par])

    @pl.when(s == 0)
    def _():
        zbuf[...] = jnp.zeros_like(zbuf)
        idx_copy(0).start()
        idx_copy(1).start()
        idx_copy(0).wait()
        for r in range(tm):
            gather_copy(0, 0, r).start()

    @pl.when(kind == _STAGE)
    def _():
        c = c_ref[s]
        r0 = pl.multiple_of(c * rg, rg)
        wg_s[pl.ds(r0, rg), :] = wg_ref[0].astype(BF16)
        wu_s[pl.ds(r0, rg), :] = wu_ref[0].astype(BF16)
        r1 = pl.multiple_of(c * rd, rd)
        wd_s[pl.ds(r1, rd), :] = wd_ref[0].astype(BF16)

    def compute(b, par, first):
        idx_copy(b + 1).wait()
        for r in range(tm):
            gather_copy(b, par, r).wait()
        if not first:
            @pl.when(b >= 2)
            def _():
                for r in range(tm):
                    scatter_copy(b - 2, par, r).wait()

            for r in range(tm):
                scatter_copy(b - 1, 1 - par, r).start()
        for r in range(tm):
            gather_copy(b + 1, 1 - par, r).start()
        idx_copy(b + 2).start()
        x = jnp.concatenate([xbufs[par][pl.ds(j, tm, stride=FFN_ROW_PITCH), :] for j in range(ns)],
                            axis=1).astype(BF16)
        gate = jnp.dot(x, wg_s[...], preferred_element_type=F32)
        up = jnp.dot(x, wu_s[...], preferred_element_type=F32)
        h = (jax.nn.silu(gate) * up).astype(BF16)
        out = jnp.dot(h, wd_s[...], preferred_element_type=F32) * w_ref[...]
        for j in range(ns):
            obufs[par][pl.ds(j, tm, stride=FFN_ROW_PITCH), :] = out[:, j * LANES:(j + 1) * LANES]

    is_compute = kind == _COMPUTE
    blk = blk_ref[s]
    pl.when(is_compute & (blk == 0))(lambda: compute(blk, 0, True))
    pl.when(is_compute & (blk > 0) & (blk % 2 == 0))(lambda: compute(blk, 0, False))
    pl.when(is_compute & (blk % 2 == 1))(lambda: compute(blk, 1, False))

    @pl.when(kind == _ZERO)
    def _():
        row0 = pl.multiple_of(blk * (tm * ns), tm * ns)
        tail = pltpu.make_async_copy(zbuf, y_hbm.at[pl.ds(row0, tm * ns), :], zsem.at[0])
        tail.start()
        tail.wait()

    def drain(n_used, par):
        idx_copy(n_used + 1).wait()
        for r in range(tm):
            gather_copy(n_used, par, r).wait()
        for r in range(tm):
            scatter_copy(n_used - 1, 1 - par, r).start()
        for r in range(tm):
            scatter_copy(n_used - 2, par, r).wait()
        for r in range(tm):
            scatter_copy(n_used - 1, 1 - par, r).wait()

    is_last = s == pl.num_programs(0) - 1
    n_used = nblk_ref[0]
    pl.when(is_last & (n_used % 2 == 0))(lambda: drain(n_used, 0))
    pl.when(is_last & (n_used % 2 == 1))(lambda: drain(n_used, 1))


def _routed_ffn(x, tab, row_w, nb_e, wg, wu, wd):
    t, d = x.shape
    e, _, f = wg.shape
    n_blocks = tab.shape[0]
    ns = d // LANES
    assert ns % 8 == 0 and ns <= FFN_ROW_PITCH
    x = x.reshape(t * ns, LANES)
    row_buf = pltpu.VMEM((FFN_TM * FFN_ROW_PITCH, LANES), F32)
    zero_buf = pltpu.VMEM((FFN_TM * ns, LANES), F32)
    assert d % FFN_NCH == 0 and f % (FFN_NCH * 16) == 0 and tab.shape[1] == 2 * FFN_TM
    rg, rd = d // FFN_NCH, f // FFN_NCH
    kind, e_s, c_s, blk = _ffn_items(nb_e, n_blocks)
    n_used = jnp.sum(nb_e).astype(jnp.int32).reshape(1)
    chunk = lambda r, w: pl.BlockSpec((1, r, w), lambda s, k, es, cs, bs, nu: (es[s], cs[s], 0))
    grid_spec = pltpu.PrefetchScalarGridSpec(
        num_scalar_prefetch=5,
        grid=(FFN_NCH * e + n_blocks,),
        in_specs=[pl.BlockSpec(memory_space=pl.ANY), pl.BlockSpec(memory_space=pl.ANY),
                  pl.BlockSpec((FFN_TM, 1), lambda s, k, es, cs, bs, nu: (bs[s], 0)),
                  chunk(rg, f), chunk(rg, f), chunk(rd, d)],
        out_specs=pl.BlockSpec(memory_space=pl.ANY),
        scratch_shapes=[pltpu.VMEM((d, f), BF16), pltpu.VMEM((d, f), BF16), pltpu.VMEM((f, d), BF16),
                        row_buf, row_buf, row_buf, row_buf, zero_buf,
                        pltpu.SMEM((FFN_IDX_SLOTS, 2 * FFN_TM), jnp.int32),
                        pltpu.SemaphoreType.DMA((FFN_IDX_SLOTS,)), pltpu.SemaphoreType.DMA((2,)),
                        pltpu.SemaphoreType.DMA((2,)), pltpu.SemaphoreType.DMA((1,))],
    )
    return pl.pallas_call(
        _routed_ffn_kernel,
        out_shape=jax.ShapeDtypeStruct((n_blocks * FFN_TM * ns, LANES), F32),
        grid_spec=grid_spec,
        compiler_params=pltpu.CompilerParams(dimension_semantics=("arbitrary",), vmem_limit_bytes=VMEM_LIMIT),
        name="routed_ffn",
    )(kind, e_s, c_s, blk, n_used, x, tab, row_w, wg, wu, wd)


ML_L = 256
ML_HALO = 8
MLSTM_CONV = 4


def _mlstm_kernel(qk_ref, v_ref, o_ref, if_ref, convw_ref, convb_ref, ifb_ref, ng_ref, tril_ref,
                  out_ref, c_ref, m_ref, halo_ref):
    L, NH, DK, DV = ML_L, MLSTM_HEADS, MLSTM_QK_DIM, MLSTM_V_DIM

    @pl.when(pl.program_id(1) == 0)
    def _():
        c_ref[...] = jnp.zeros_like(c_ref)
        m_ref[...] = jnp.zeros_like(m_ref)
        halo_ref[...] = jnp.zeros_like(halo_ref)

    x = qk_ref[0]
    cat = jnp.concatenate([halo_ref[...], x], axis=0)
    halo_ref[...] = x[L - ML_HALO:, :]
    y = convb_ref[...]
    for j in range(MLSTM_CONV):
        lo = ML_HALO - (MLSTM_CONV - 1) + j
        y = y + convw_ref[j:j + 1, :] * cat[lo:lo + L, :]
    qk = y * jax.nn.sigmoid(y)

    lane = lax.broadcasted_iota(jnp.int32, (L, LANES), 1)
    z = if_ref[0] + ifb_ref[...]
    lg = jnp.where(lane < NH, z, jnp.minimum(z, 0.0) - jnp.log1p(jnp.exp(-jnp.abs(z))))
    hi = lg.astype(BF16)
    r1 = lg - hi.astype(F32)
    mid = r1.astype(BF16)
    lo3 = (r1 - mid.astype(F32)).astype(BF16)
    tril = tril_ref[...]
    bcum = (jnp.dot(tril, hi, preferred_element_type=F32) + jnp.dot(tril, mid, preferred_element_type=F32)
            + jnp.dot(tril, lo3, preferred_element_type=F32))

    row = lax.broadcasted_iota(jnp.int32, (L, L), 0)
    colm = lax.broadcasted_iota(jnp.int32, (L, L), 1)
    tri = row >= colm
    ones_col = (lane == 0).astype(BF16)
    for h in range(NH):
        li = lg[:, h:h + 1]
        b = bcum[:, NH + h:NH + h + 1]
        m_prev = m_ref[h:h + 1, 0:1]
        a_col = li - b
        a_row = jnp.transpose(jnp.broadcast_to(a_col, (L, LANES)))[0:1, :]
        dmat = jnp.where(tri, b + a_row, -jnp.inf)
        m_inter = b + m_prev
        m_t = jnp.maximum(m_inter, dmat.max(-1, keepdims=True))
        q = qk[:, h * DK:(h + 1) * DK].astype(BF16)
        k = qk[:, NH * DK + h * DK:NH * DK + (h + 1) * DK] * DK ** -0.5
        s = lax.dot_general(q, k.astype(BF16), (((1,), (1,)), ((), ())), preferred_element_type=F32)
        wqk = (jnp.exp(dmat - m_t) * s).astype(BF16)
        inter = jnp.exp(m_inter - m_t)
        v_ext = jnp.concatenate([v_ref[0, :, h * DV:(h + 1) * DV].astype(BF16), ones_col], axis=1)
        c_old = c_ref[h]
        nd = (jnp.dot(wqk, v_ext, preferred_element_type=F32)
              + inter * jnp.dot(q, c_old.astype(BF16), preferred_element_type=F32))
        num = nd[:, :DV]
        den = nd[:, DV:DV + 1]
        hh = num / jnp.maximum(jnp.abs(den), jnp.exp(-m_t))
        b_last = b[L - 1:L, :]
        g = b_last - b + li
        m_new = jnp.maximum(b_last + m_prev, g.max(0, keepdims=True))
        decay = jnp.exp(b_last + m_prev - m_new)
        w_s = jnp.exp(g - m_new)
        kw_t = jnp.transpose(k * w_s).astype(BF16)
        c_ref[h] = decay * c_old + jnp.dot(kw_t, v_ext, preferred_element_type=F32)
        m_ref[h:h + 1, :] = jnp.broadcast_to(m_new, (1, LANES))
        mu = hh.mean(-1, keepdims=True)
        var = jnp.mean(jnp.square(hh - mu), -1, keepdims=True)
        hn = (hh - mu) * lax.rsqrt(var + LN_EPS) * ng_ref[:, h * DV:(h + 1) * DV]
        out_ref[0, :, h * DV:(h + 1) * DV] = (
            jax.nn.sigmoid(o_ref[0, :, h * DV:(h + 1) * DV]) * hn).astype(out_ref.dtype)


def _mlstm(y3, conv_w, conv_b, if_bias, norm_g):
    b, s, _ = y3.shape
    L, NH = ML_L, MLSTM_HEADS
    assert s % L == 0

    def col(name, width):
        blk = _SEGS[name][0] // width
        return pl.BlockSpec((1, L, width), lambda i, j: (i, j, blk))

    full = lambda shp: pl.BlockSpec(shp, lambda i, j: (0,) * len(shp))
    ifb = jnp.pad(if_bias.reshape(1, 2 * NH), ((0, 0), (0, LANES - 2 * NH)))
    tril = jnp.asarray(np.tril(np.ones((L, L), np.float32)), dtype=BF16)
    conv_b = conv_b.reshape(1, -1)
    norm_g = norm_g.reshape(1, -1)
    return pl.pallas_call(
        _mlstm_kernel,
        out_shape=jax.ShapeDtypeStruct((b, s, ML_V_W), BF16),
        grid=(b, s // L),
        in_specs=[col("qk_ml", 2 * ML_QK_W), col("v_ml", ML_V_W), col("o_ml", ML_V_W), col("if_ml", LANES),
                  full(conv_w.shape), full(conv_b.shape), full(ifb.shape), full(norm_g.shape), full(tril.shape)],
        out_specs=pl.BlockSpec((1, L, ML_V_W), lambda i, j: (i, j, 0)),
        scratch_shapes=[pltpu.VMEM((NH, MLSTM_QK_DIM, MLSTM_V_DIM + LANES), F32),
                        pltpu.VMEM((8, LANES), F32),
                        pltpu.VMEM((ML_HALO, 2 * ML_QK_W), F32)],
        compiler_params=pltpu.CompilerParams(dimension_semantics=("parallel", "arbitrary"),
                                             vmem_limit_bytes=VMEM_LIMIT),
        name="mlstm",
    )(y3, y3, y3, y3, conv_w, conv_b, ifb, norm_g, tril)


MEM_TQ = 512


def _mem_attn_kernel(q_ref, kv_ref, o_ref):
    hd = MEM_HEAD_DIM
    for h in range(MEM_HEADS):
        q = q_ref[0, :, h * hd:(h + 1) * hd].astype(BF16)
        k = kv_ref[0, :, h * hd:(h + 1) * hd]
        v = kv_ref[0, :, MEM_W + h * hd:MEM_W + (h + 1) * hd]
        s = lax.dot_general(q, k, (((1,), (1,)), ((), ())), preferred_element_type=F32) * hd ** -0.5
        e = jnp.exp(s - s.max(-1, keepdims=True))
        p = (e / e.sum(-1, keepdims=True)).astype(BF16)
        o_ref[0, :, h * hd:(h + 1) * hd] = jnp.dot(p, v, preferred_element_type=F32).astype(o_ref.dtype)


def _mem_attention(y3, kv_mem):
    b, s, _ = y3.shape
    m = kv_mem.shape[1]
    q_blk = _SEGS["q_mem"][0] // MEM_W
    return pl.pallas_call(
        _mem_attn_kernel,
        out_shape=jax.ShapeDtypeStruct((b, s, MEM_W), BF16),
        grid=(b, s // MEM_TQ),
        in_specs=[pl.BlockSpec((1, MEM_TQ, MEM_W), lambda i, j: (i, j, q_blk)),
                  pl.BlockSpec((1, m, 2 * MEM_W), lambda i, j: (i, 0, 0))],
        out_specs=pl.BlockSpec((1, MEM_TQ, MEM_W), lambda i, j: (i, j, 0)),
        compiler_params=pltpu.CompilerParams(dimension_semantics=("parallel", "parallel"),
                                             vmem_limit_bytes=VMEM_LIMIT),
        name="mem_attention",
    )(y3, kv_mem)


MERGE_TM = 256


def _merge_kernel(yn_ref, ym_ref, yc_ref, g0_ref, g1_ref, g2_ref, x_ref, wb_ref, wo_ref, lg_ref, lb_ref,
                  x1_ref, x1b_ref):
    merged = None
    for i, (y_ref, g_ref) in enumerate(((yn_ref, g0_ref), (ym_ref, g1_ref), (yc_ref, g2_ref))):
        t = jax.nn.sigmoid(g_ref[...]) * jnp.dot(y_ref[...], wb_ref[i], preferred_element_type=F32)
        merged = t if merged is None else merged + t
    z = DN_ALPHA * x_ref[...] + jnp.dot(merged.astype(BF16), wo_ref[...], preferred_element_type=F32)
    mu = z.mean(-1, keepdims=True)
    var = jnp.mean(jnp.square(z - mu), -1, keepdims=True)
    x1 = (z - mu) * lax.rsqrt(var + LN_EPS) * lg_ref[...] + lb_ref[...]
    x1_ref[...] = x1
    x1b_ref[...] = x1.astype(BF16)


def _merge_ln1(y_nsa, y_ml, y_mem, y2, xt, wb, wo, ln_g, ln_b):
    t, d = xt.shape
    tm = MERGE_TM
    bw = y_nsa.shape[1]
    resident = pl.Buffered(1)
    row = lambda w: pl.BlockSpec((tm, w), lambda i: (i, 0))
    gate = lambda k: pl.BlockSpec((tm, d), lambda i: (i, k))
    return pl.pallas_call(
        _merge_kernel,
        out_shape=(jax.ShapeDtypeStruct((t, d), F32), jax.ShapeDtypeStruct((t, d), BF16)),
        grid=(t // tm,),
        in_specs=[row(bw), row(bw), row(bw), gate(0), gate(1), gate(2), row(d),
                  pl.BlockSpec((N_BRANCH, bw, d), lambda i: (0, 0, 0), pipeline_mode=resident),
                  pl.BlockSpec((d, d), lambda i: (0, 0), pipeline_mode=resident),
                  pl.BlockSpec((1, d), lambda i: (0, 0)), pl.BlockSpec((1, d), lambda i: (0, 0))],
        out_specs=(row(d), row(d)),
        compiler_params=pltpu.CompilerParams(dimension_semantics=("parallel",), vmem_limit_bytes=VMEM_LIMIT),
        name="merge_ln1",
    )(y_nsa, y_ml, y_mem, y2, y2, y2, xt, wb, wo, ln_g.reshape(1, d), ln_b.reshape(1, d))


ROUTER_TM = 256


def _router_kernel(x_ref, w_ref, bias_ref, idx_ref, wt_ref, cnt_ref, run_ref):
    tm = ROUTER_TM

    @pl.when(pl.program_id(0) == 0)
    def _():
        run_ref[...] = jnp.zeros_like(run_ref)

    x = x_ref[...]
    w = w_ref[...]
    x_hi = x.astype(BF16)
    x_lo = (x - x_hi.astype(F32)).astype(BF16)
    w_hi = w.astype(BF16)
    w_lo = (w - w_hi.astype(F32)).astype(BF16)
    logits = (jnp.dot(x_hi, w_hi, preferred_element_type=F32) + jnp.dot(x_lo, w_hi, preferred_element_type=F32)
              + jnp.dot(x_hi, w_lo, preferred_element_type=F32))
    lane = lax.broadcasted_iota(jnp.int32, (tm, LANES), 1)
    aff = jax.nn.sigmoid(logits)
    score = jnp.where(lane < N_EXPERTS, aff + bias_ref[...], -jnp.inf)
    idx_out = jnp.zeros((tm, LANES), jnp.int32)
    aff_out = jnp.zeros((tm, LANES), F32)
    chosen = jnp.zeros((tm, LANES), jnp.bool_)
    for k in range(TOP_K):
        best = score.max(-1, keepdims=True)
        cur = jnp.where(score == best, lane, LANES).min(-1, keepdims=True)
        here = lane == cur
        idx_out = jnp.where(lane == k, cur, idx_out)
        aff_out = jnp.where(lane == k, jnp.where(here, aff, 0.0).sum(-1, keepdims=True), aff_out)
        chosen = chosen | here
        score = jnp.where(here, -jnp.inf, score)
    run_new = run_ref[0:1, :] + chosen.astype(F32).sum(0, keepdims=True)
    run_ref[...] = jnp.broadcast_to(run_new, run_ref.shape)
    cnt_ref[...] = jnp.broadcast_to(run_new, cnt_ref.shape)
    idx_ref[...] = idx_out
    wt_ref[...] = aff_out / aff_out.sum(-1, keepdims=True) * ROUTED_SCALE


def _router(x1, router_w, router_bias):
    t, d = x1.shape
    tm = ROUTER_TM
    e = router_w.shape[1]
    w = jnp.pad(router_w, ((0, 0), (0, LANES - e)))
    bias = jnp.pad(router_bias.reshape(1, e), ((0, 0), (0, LANES - e)))
    out = pl.BlockSpec((tm, LANES), lambda i: (i, 0))
    const = lambda shp: pl.BlockSpec(shp, lambda i: (0, 0))
    idx, wt, cnt = pl.pallas_call(
        _router_kernel,
        out_shape=(jax.ShapeDtypeStruct((t, LANES), jnp.int32), jax.ShapeDtypeStruct((t, LANES), F32),
                   jax.ShapeDtypeStruct((8, LANES), F32)),
        grid=(t // tm,),
        in_specs=[pl.BlockSpec((tm, d), lambda i: (i, 0)), const(w.shape), const(bias.shape)],
        out_specs=(out, out, const((8, LANES))),
        scratch_shapes=[pltpu.VMEM((8, LANES), F32)],
        compiler_params=pltpu.CompilerParams(dimension_semantics=("arbitrary",), vmem_limit_bytes=VMEM_LIMIT),
        name="router",
    )(x1, w, bias)
    return idx[:, :TOP_K], wt[:, :TOP_K], cnt[0, :e].astype(jnp.int32)


def _moe_layout(idx, wts, counts):
    t = idx.shape[0]
    e = N_EXPERTS
    m = t * TOP_K
    assert m // FFN_TM >= 2
    nb_e = (counts + FFN_TM - 1) // FFN_TM
    blk_start = jnp.cumsum(nb_e) - nb_e
    start = jnp.cumsum(counts) - counts
    n_blocks = m // FFN_TM + e
    order = jnp.argsort(idx.reshape(m))
    bi = jnp.arange(n_blocks)
    blk_e = jnp.clip((bi[:, None] >= blk_start[None, :]).sum(1) - 1, 0, e - 1)
    j = ((bi - blk_start[blk_e]) * FFN_TM)[:, None] + jnp.arange(FFN_TM)[None, :]
    valid = j < counts[blk_e][:, None]
    flat = order[jnp.clip(start[blk_e][:, None] + j, 0, m - 1)]
    tok, slot = flat // TOP_K, flat % TOP_K
    pad_before = (blk_start * FFN_TM - start)[blk_e] - counts[blk_e]
    row_tok = jnp.where(valid, tok, 0)
    dest = jnp.where(valid, slot * t + tok, m + pad_before[:, None] + j)
    row_w = jnp.where(valid, wts.reshape(m)[flat], 0.0).reshape(n_blocks * FFN_TM, 1)
    tab = jnp.concatenate([row_tok, dest], axis=1).astype(jnp.int32)
    return tab, row_w, nb_e


def _combine_kernel(x_ref, *refs):
    slot_refs, (s_ref, g_ref, b_ref, o_ref) = refs[:TOP_K], refs[TOP_K:]
    tm, d = x_ref.shape
    ns = d // LANES

    def rows(r_ref):
        return jnp.concatenate([r_ref[pl.ds(j, tm, stride=ns), :] for j in range(ns)], axis=1)

    routed = rows(slot_refs[0])
    for r_ref in slot_refs[1:]:
        routed = routed + rows(r_ref)
    z = DN_ALPHA * x_ref[...] + (routed + s_ref[...])
    mu = z.mean(-1, keepdims=True)
    var = jnp.mean(jnp.square(z - mu), -1, keepdims=True)
    o_ref[...] = (z - mu) * lax.rsqrt(var + LN_EPS) * g_ref[...] + b_ref[...]


def _combine_ln2(x1, y_slots, shared, ln_g, ln_b, *, tm=256):
    t, d = x1.shape
    nt = t // tm
    ns = d // LANES
    row = pl.BlockSpec((tm, d), lambda i: (i, 0))
    vec = pl.BlockSpec((1, d), lambda i: (0, 0))
    slots = [pl.BlockSpec((tm * ns, LANES), functools.partial(lambda i, k: (k * nt + i, 0), k=k))
             for k in range(TOP_K)]
    return pl.pallas_call(
        _combine_kernel, out_shape=jax.ShapeDtypeStruct((t, d), F32), grid=(nt,),
        in_specs=[row] + slots + [row, vec, vec], out_specs=row,
        compiler_params=pltpu.CompilerParams(dimension_semantics=("parallel",), vmem_limit_bytes=VMEM_LIMIT),
        name="combine_ln2",
    )(x1, *([y_slots] * TOP_K), shared, ln_g.reshape(1, d), ln_b.reshape(1, d))


def _layer(x, mem, w_in, mlstm_conv_w, mlstm_conv_b, mlstm_if_bias, mlstm_norm_g,
           cmp_pos_k, cmp_k_w1, cmp_k_w2, cmp_pos_v, cmp_v_w1, cmp_v_w2,
           w_mem_kv, w_branch, w_o, ln1_g, ln1_b, router_w, router_bias,
           w_e_gate, w_e_up, w_e_down, w_s_gate, w_s_up, w_s_down, ln2_g, ln2_b):
    B, S, D = x.shape
    T = B * S
    xt = x.reshape(T, D)
    y2 = _matmul(xt.astype(BF16), _pad_in_proj(w_in).astype(BF16), tm=1024, tn=IN_TN, name="in_proj")
    y3 = y2.reshape(B, S, IN_PAD)

    cos, sin = _rope_tables(jnp.arange(S))
    cos_e, sin_e = _rope_tables(jnp.arange(S // CMP_STRIDE) * CMP_STRIDE + CMP_LEN - 1)
    kvc, ks, vs, kw, vw = _nsa_prep(y3, cos, sin)
    kc, vc = _compress(kvc, cmp_pos_k, cmp_k_w1, cmp_k_w2, cmp_pos_v, cmp_v_w1, cmp_v_w2, cos_e, sin_e)
    y_nsa = _nsa_attention(y3, kc, vc, ks, vs, kw, vw, cos, sin)

    y_ml = _mlstm(y3, mlstm_conv_w, mlstm_conv_b, mlstm_if_bias, mlstm_norm_g)

    m_len = mem.shape[1]
    kv_mem = _matmul(mem.reshape(B * m_len, D).astype(BF16), w_mem_kv.astype(BF16), tm=1024, tn=512,
                     out_dtype=BF16, name="mem_kv").reshape(B, m_len, 2 * MEM_W)
    y_mem = _mem_attention(y3, kv_mem)

    x1, x1b = _merge_ln1(y_nsa.reshape(T, -1), y_ml.reshape(T, -1), y_mem.reshape(T, -1), y2, xt,
                         w_branch.astype(BF16), w_o.astype(BF16), ln1_g, ln1_b)

    idx, wts, counts = _router(x1, router_w, router_bias)
    tab, row_w, nb_e = _moe_layout(idx, wts, counts)
    y_slots = _routed_ffn(x1, tab, row_w, nb_e, w_e_gate, w_e_up, w_e_down)
    shared = _grouped_ffn(x1b, jnp.ones((T, 1), F32), jnp.full((1,), T // FFN_TM, jnp.int32),
                          w_s_gate[None], w_s_up[None], w_s_down[None])
    return _combine_ln2(x1, y_slots, shared, ln2_g, ln2_b).reshape(B, S, D)


def kernel(x, mem, w_in, mlstm_conv_w, mlstm_conv_b, mlstm_if_bias, mlstm_norm_g, cmp_pos_k, cmp_k_w1, cmp_k_w2, cmp_pos_v, cmp_v_w1, cmp_v_w2, w_mem_kv, w_branch, w_o, ln1_g, ln1_b, router_w, router_bias, w_e_gate, w_e_up, w_e_down, w_s_gate, w_s_up, w_s_down, ln2_g, ln2_b):
    for layer in range(DEPTH):
        x = _layer(x, mem, w_in[layer], mlstm_conv_w[layer], mlstm_conv_b[layer],
                   mlstm_if_bias[layer], mlstm_norm_g[layer],
                   cmp_pos_k[layer], cmp_k_w1[layer], cmp_k_w2[layer],
                   cmp_pos_v[layer], cmp_v_w1[layer], cmp_v_w2[layer],
                   w_mem_kv[layer], w_branch[layer], w_o[layer], ln1_g[layer], ln1_b[layer],
                   router_w[layer], router_bias[layer], w_e_gate[layer], w_e_up[layer],
                   w_e_down[layer], w_s_gate[layer], w_s_up[layer], w_s_down[layer],
                   ln2_g[layer], ln2_b[layer])
    return x
```

```python
import functools

import jax
import jax.numpy as jnp
import numpy as np
from jax import lax
from jax.experimental import pallas as pl
from jax.experimental.pallas import tpu as pltpu

F32 = jnp.float32
BF16 = jnp.bfloat16

D_MODEL = 2048
HEAD_DIM = 128
NSA_HEADS = 8
NSA_KV_GROUPS = 2
NSA_REP = NSA_HEADS // NSA_KV_GROUPS
CMP_LEN = 32
CMP_STRIDE = 16
SLC_LEN = 64
SLC_TOPK = 16
WINDOW = 512
MLSTM_HEADS = 4
MLSTM_QK_DIM = 128
MLSTM_V_DIM = 256
MLSTM_CHUNK = 64
MEM_HEADS = 4
MEM_HEAD_DIM = 256
N_EXPERTS = 64
TOP_K = 6
EXPERT_DIM = 1408
ROUTED_SCALE = 2.5
ROPE_THETA = 10000.0
LN_EPS = 1e-5
NEG_INF = -1e30
DEPTH = 1
DN_ALPHA = (2 * DEPTH) ** 0.25

NSA_Q_W = NSA_HEADS * HEAD_DIM
NSA_KV_W = NSA_KV_GROUPS * HEAD_DIM
ML_QK_W = MLSTM_HEADS * MLSTM_QK_DIM
ML_V_W = MLSTM_HEADS * MLSTM_V_DIM
MEM_W = MEM_HEADS * MEM_HEAD_DIM
N_BRANCH = 3

LANES = 128
VMEM_LIMIT = 56 * 1024 * 1024

_SEGS = {}
_off = 0
for _name, _w, _blk in (("g_br", N_BRANCH * D_MODEL, D_MODEL), ("q_nsa", NSA_Q_W, NSA_REP * HEAD_DIM),
                        ("q_mem", MEM_W, MEM_W), ("v_ml", ML_V_W, ML_V_W), ("o_ml", ML_V_W, ML_V_W),
                        ("qk_ml", 2 * ML_QK_W, 2 * ML_QK_W), ("kv_cmp", 2 * NSA_KV_W, 2 * NSA_KV_W),
                        ("k_slc", NSA_KV_W, NSA_KV_W), ("v_slc", NSA_KV_W, NSA_KV_W),
                        ("k_win", NSA_KV_W, NSA_KV_W), ("v_win", NSA_KV_W, NSA_KV_W),
                        ("g_nsa", NSA_KV_GROUPS * LANES, LANES), ("if_ml", LANES, LANES)):
    assert _off % _blk == 0
    _SEGS[_name] = (_off, _w)
    _off += _w
IN_TN = 512
IN_PAD = -(-_off // IN_TN) * IN_TN


def _pad_in_proj(w_in):
    sizes = (NSA_Q_W,) + (NSA_KV_W,) * 6 + (NSA_HEADS * 3, ML_QK_W, ML_QK_W, ML_V_W, ML_V_W,
                                            MLSTM_HEADS, MLSTM_HEADS, MEM_W, N_BRANCH * D_MODEL)
    splits = np.cumsum(sizes)[:-1].tolist()
    (q_nsa, k_cmp, v_cmp, k_slc, v_slc, k_win, v_win, g_nsa,
     q_ml, k_ml, v_ml, o_ml, i_ml, f_ml, q_mem, g_br) = jnp.split(w_in, splits, axis=-1)
    per_group = NSA_REP * 3
    g_blocks = [jnp.pad(g_nsa[:, g * per_group:(g + 1) * per_group], ((0, 0), (0, LANES - per_group)))
                for g in range(NSA_KV_GROUPS)]
    if_blk = jnp.pad(jnp.concatenate([i_ml, f_ml], -1), ((0, 0), (0, LANES - 2 * MLSTM_HEADS)))
    cols = [g_br, q_nsa, q_mem, v_ml, o_ml, q_ml, k_ml, k_cmp, v_cmp,
            k_slc, v_slc, k_win, v_win] + g_blocks + [if_blk]
    w = jnp.concatenate(cols, axis=-1)
    return jnp.pad(w, ((0, 0), (0, IN_PAD - w.shape[1])))


def _mm_kernel(a_ref, b_ref, o_ref):
    o_ref[...] = jnp.dot(a_ref[...].astype(BF16), b_ref[...].astype(BF16),
                         preferred_element_type=F32).astype(o_ref.dtype)


def _matmul(a, b, *, tm, tn, out_dtype=F32, name="matmul"):
    m, k = a.shape
    _, n = b.shape
    assert m % tm == 0 and n % tn == 0
    return pl.pallas_call(
        _mm_kernel,
        out_shape=jax.ShapeDtypeStruct((m, n), out_dtype),
        grid=(m // tm, n // tn),
        in_specs=[pl.BlockSpec((tm, k), lambda i, j: (i, 0)),
                  pl.BlockSpec((k, tn), lambda i, j: (0, j))],
        out_specs=pl.BlockSpec((tm, tn), lambda i, j: (i, j)),
        compiler_params=pltpu.CompilerParams(dimension_semantics=("parallel", "arbitrary"),
                                             vmem_limit_bytes=VMEM_LIMIT),
        name=name,
    )(a, b)


def _rope_tables(pos):
    half = HEAD_DIM // 2
    inv = ROPE_THETA ** (-jnp.arange(half, dtype=F32) / half)
    ang = pos.astype(F32)[:, None] * inv[None, :]
    cos, sin = jnp.cos(ang), jnp.sin(ang)
    return jnp.concatenate([cos, cos], -1), jnp.concatenate([-sin, sin], -1)


def _rope(x, cos, sin_signed):
    return x * cos + pltpu.roll(x, HEAD_DIM // 2, 1) * sin_signed


def _nsa_prep_kernel(kvc_ref, ks_ref, vs_ref, kw_ref, vw_ref, cos_ref, sin_ref,
                     kvc_o, ks_o, vs_o, kw_o, vw_o):
    cos, sin = cos_ref[...], sin_ref[...]
    kvc_o[0] = kvc_ref[0]
    for src, dst in ((ks_ref, ks_o), (kw_ref, kw_o)):
        for g in range(NSA_KV_GROUPS):
            sl = slice(g * HEAD_DIM, (g + 1) * HEAD_DIM)
            dst[0, :, sl] = _rope(src[0, :, sl], cos, sin).astype(BF16)
    vs_o[0] = vs_ref[0].astype(BF16)
    vw_o[0] = vw_ref[0].astype(BF16)


def _nsa_prep(y3, cos, sin, *, ts=512):
    b, s, _ = y3.shape
    kvw = NSA_KV_W

    def col(name, width):
        blk = _SEGS[name][0] // width
        return pl.BlockSpec((1, ts, width), lambda i, j: (i, j, blk))

    tab = pl.BlockSpec((ts, HEAD_DIM), lambda i, j: (j, 0))
    out_kv = pl.BlockSpec((1, ts, kvw), lambda i, j: (i, j, 0))
    return pl.pallas_call(
        _nsa_prep_kernel,
        out_shape=(jax.ShapeDtypeStruct((b, s, 2 * kvw), F32),) + (jax.ShapeDtypeStruct((b, s, kvw), BF16),) * 4,
        grid=(b, s // ts),
        in_specs=[col("kv_cmp", 2 * kvw), col("k_slc", kvw), col("v_slc", kvw), col("k_win", kvw),
                  col("v_win", kvw), tab, tab],
        out_specs=(pl.BlockSpec((1, ts, 2 * kvw), lambda i, j: (i, j, 0)), out_kv, out_kv, out_kv, out_kv),
        compiler_params=pltpu.CompilerParams(dimension_semantics=("parallel", "parallel")),
        name="nsa_prep",
    )(y3, y3, y3, y3, y3, cos, sin)


def _cmp_kernel(u_ref, posa_ref, posb_ref, w1a_ref, w1b_ref, w2_ref, cos_ref, sin_ref, kc_o, vc_o):
    n_rows = u_ref.shape[2]
    for idx, out in ((0, kc_o), (1, vc_o)):
        u = u_ref[0, idx]
        a = jnp.dot((u + posa_ref[idx]).astype(BF16), w1a_ref[idx], preferred_element_type=F32)
        bm = jnp.dot((u + posb_ref[idx]).astype(BF16), w1b_ref[idx], preferred_element_type=F32)
        pre = a + pltpu.roll(bm, n_rows - 1, 0)
        h = jax.nn.gelu(pre).astype(BF16)
        c = jnp.dot(h, w2_ref[idx], preferred_element_type=F32)
        if idx == 0:
            for g in range(NSA_KV_GROUPS):
                sl = slice(g * HEAD_DIM, (g + 1) * HEAD_DIM)
                out[0, :, sl] = _rope(c[:, sl], cos_ref[...], sin_ref[...]).astype(BF16)
        else:
            out[0] = c.astype(BF16)


def _block_diag_groups(w):
    z = jnp.zeros_like(w)
    rows = [jnp.concatenate([w if gi == go else z for go in range(NSA_KV_GROUPS)], -1)
            for gi in range(NSA_KV_GROUPS)]
    return jnp.concatenate(rows, -2)


def _compress(kvc, cmp_pos_k, cmp_k_w1, cmp_k_w2, cmp_pos_v, cmp_v_w1, cmp_v_w2, cos_e, sin_e):
    b, s, _ = kvc.shape
    n_rows = s // CMP_STRIDE
    g, hd = NSA_KV_GROUPS, HEAD_DIM
    row_w = CMP_STRIDE * g * hd
    u = kvc.reshape(b, n_rows, CMP_STRIDE, 2, g * hd).transpose(0, 3, 1, 2, 4).reshape(b, 2, n_rows, row_w)

    def pos_rows(pos):
        p = jnp.broadcast_to(pos.reshape(2, CMP_STRIDE, 1, hd), (2, CMP_STRIDE, g, hd))
        return p.reshape(2, 1, row_w)

    def w1_halves(w1):
        w = w1.reshape(2, CMP_STRIDE, hd, hd)
        return _block_diag_groups(w).reshape(2, row_w, g * hd).astype(BF16)

    pk, pv = pos_rows(cmp_pos_k), pos_rows(cmp_pos_v)
    wk, wv = w1_halves(cmp_k_w1), w1_halves(cmp_v_w1)
    posa = jnp.stack([pk[0], pv[0]])
    posb = jnp.stack([pk[1], pv[1]])
    w1a = jnp.stack([wk[0], wv[0]])
    w1b = jnp.stack([wk[1], wv[1]])
    w2 = jnp.stack([_block_diag_groups(cmp_k_w2), _block_diag_groups(cmp_v_w2)]).astype(BF16)
    full = lambda shp: pl.BlockSpec(shp, lambda i: (0,) * len(shp))
    out_spec = pl.BlockSpec((1, n_rows, g * hd), lambda i: (i, 0, 0))
    return pl.pallas_call(
        _cmp_kernel,
        out_shape=(jax.ShapeDtypeStruct((b, n_rows, g * hd), BF16),) * 2,
        grid=(b,),
        in_specs=[pl.BlockSpec((1, 2, n_rows, row_w), lambda i: (i, 0, 0, 0)),
                  full(posa.shape), full(posb.shape), full(w1a.shape), full(w1b.shape), full(w2.shape),
                  full(cos_e.shape), full(sin_e.shape)],
        out_specs=(out_spec, out_spec),
        compiler_params=pltpu.CompilerParams(dimension_semantics=("parallel",), vmem_limit_bytes=VMEM_LIMIT),
        name="nsa_compress",
    )(u, posa, posb, w1a, w1b, w2, cos_e, sin_e)


NSA_TQ = 256
NSA_TK = 256


def _flash(q4, k_ref, v_ref, kt_lo, kt_hi, bias_fn, acc_ref):
    rows = q4.shape[0]
    tq = rows // NSA_REP
    scale = HEAD_DIM ** -0.5
    acc_ref[...] = jnp.zeros_like(acc_ref)

    def body(kt, carry):
        m, l = carry
        k0 = pl.multiple_of(kt * NSA_TK, NSA_TK)
        k = k_ref[0, pl.ds(k0, NSA_TK), :]
        v = v_ref[0, pl.ds(k0, NSA_TK), :]
        s = lax.dot_general(q4, k, (((1,), (1,)), ((), ())), preferred_element_type=F32) * scale
        s = s.reshape(NSA_REP, tq, NSA_TK) + bias_fn(k0)[None]
        m_new = jnp.maximum(m, s.max(-1, keepdims=True))
        alpha = jnp.exp(m - m_new)
        p = jnp.exp(s - m_new)
        l = alpha * l + p.sum(-1, keepdims=True)
        pv = jnp.dot(p.reshape(rows, NSA_TK).astype(BF16), v, preferred_element_type=F32)
        acc_ref[...] = acc_ref[...] * alpha.reshape(rows, 1) + pv
        return m_new, l

    init = (jnp.full((NSA_REP, tq, 1), NEG_INF, F32), jnp.zeros((NSA_REP, tq, 1), F32))
    _, l = lax.fori_loop(kt_lo, kt_hi, body, init)
    return acc_ref[...] / l.reshape(rows, 1)


def _nsa_kernel(q_ref, gate_ref, kc_ref, vc_ref, ks_ref, vs_ref, kw_ref, vw_ref, cos_ref, sin_ref,
                cover_ref, o_ref, acc_ref, *, n_slc):
    qi = pl.program_id(2)
    tq = NSA_TQ
    rows = NSA_REP * tq
    q0 = qi * tq
    cos, sin = cos_ref[...], sin_ref[...]
    q4 = jnp.concatenate(
        [_rope(q_ref[0, :, r * HEAD_DIM:(r + 1) * HEAD_DIM], cos, sin).astype(BF16) for r in range(NSA_REP)],
        axis=0)
    tpos = q0 + lax.broadcasted_iota(jnp.int32, (tq, LANES), 0)
    lane = lax.broadcasted_iota(jnp.int32, (tq, LANES), 1)

    s = lax.dot_general(q4, kc_ref[0], (((1,), (1,)), ((), ())), preferred_element_type=F32)
    s = (s * HEAD_DIM ** -0.5).reshape(NSA_REP, tq, LANES)
    cmask = (lane * CMP_STRIDE + (CMP_LEN - 1) <= tpos)[None]
    s = jnp.where(cmask, s, NEG_INF)
    e = jnp.exp(s - s.max(-1, keepdims=True))
    p = e / e.sum(-1, keepdims=True) * cmask.astype(F32)
    o_cmp = jnp.dot(p.reshape(rows, LANES).astype(BF16), vc_ref[0], preferred_element_type=F32)

    p_sum_t = jnp.transpose(p.sum(0))
    p_hi = p_sum_t.astype(BF16)
    p_lo = (p_sum_t - p_hi.astype(F32)).astype(BF16)
    cover_t = cover_ref[...]
    p_slc = (jnp.dot(cover_t, p_hi, preferred_element_type=F32)
             + jnp.dot(cover_t, p_lo, preferred_element_type=F32))
    blk = lax.broadcasted_iota(jnp.int32, (n_slc, tq), 0)
    t_blk = (q0 + lax.broadcasted_iota(jnp.int32, (n_slc, tq), 1)) // SLC_LEN
    forced = (blk == 0) | (blk == t_blk) | (blk == t_blk - 1)
    score = jnp.where(forced, jnp.inf, jnp.where(blk > t_blk, -jnp.inf, p_slc))
    cnt = jnp.zeros((n_slc, tq), jnp.int32)
    for i in range(n_slc):
        other = score[i:i + 1, :]
        ahead = (other > score) | ((other == score) & (blk > i))
        cnt = cnt + ahead.astype(jnp.int32)
    chosen = (cnt < min(SLC_TOPK, n_slc)) & (score > -jnp.inf)
    pow2 = jnp.left_shift(1, blk & 15).astype(F32)
    halves = []
    for lo_blk in (0, 16):
        half = jnp.where(chosen & (blk >= lo_blk) & (blk < lo_blk + 16), pow2, 0.0).sum(0, keepdims=True)
        halves.append(jnp.transpose(jnp.broadcast_to(half, (LANES, tq)))[:, 0:1].astype(jnp.int32))
    bits = halves[0] | (halves[1] << 16)

    tpos_k = q0 + lax.broadcasted_iota(jnp.int32, (tq, NSA_TK), 0)
    kiota = lax.broadcasted_iota(jnp.int32, (tq, NSA_TK), 1)
    bits_b = jnp.broadcast_to(bits, (tq, NSA_TK))

    def slc_bias(k0):
        kpos = k0 + kiota
        picked = (lax.shift_right_logical(bits_b, kpos // SLC_LEN) & 1) == 1
        return jnp.where(picked & (kpos <= tpos_k), 0.0, NEG_INF)

    o_slc = _flash(q4, ks_ref, vs_ref, 0, (q0 + tq) // NSA_TK, slc_bias, acc_ref)

    def win_bias(k0):
        kpos = k0 + kiota
        return jnp.where((kpos <= tpos_k) & (kpos > tpos_k - WINDOW), 0.0, NEG_INF)

    kt_lo = jnp.maximum(q0 - WINDOW, 0) // NSA_TK
    o_win = _flash(q4, kw_ref, vw_ref, kt_lo, (q0 + tq) // NSA_TK, win_bias, acc_ref)

    gate = jax.nn.sigmoid(gate_ref[0])
    for r in range(NSA_REP):
        rs = slice(r * tq, (r + 1) * tq)
        o = (gate[:, 3 * r:3 * r + 1] * o_cmp[rs] + gate[:, 3 * r + 1:3 * r + 2] * o_slc[rs]
             + gate[:, 3 * r + 2:3 * r + 3] * o_win[rs])
        o_ref[0, :, r * HEAD_DIM:(r + 1) * HEAD_DIM] = o.astype(o_ref.dtype)


def _cover_matrix(n_rows, n_slc):
    i = np.arange(n_rows)[None, :]
    j = np.arange(n_slc)[:, None]
    c0 = i * CMP_STRIDE
    c1 = c0 + CMP_LEN - 1
    s0 = j * SLC_LEN
    s1 = s0 + SLC_LEN - 1
    ok = (c1 >= s0) & (c0 <= s1) & (i < n_rows - 1)
    return jnp.asarray(ok.astype(np.float32), dtype=BF16)


def _nsa_attention(y3, kc, vc, ks, vs, kw, vw, cos, sin):
    b, s, _ = y3.shape
    g, hd, tq = NSA_KV_GROUPS, HEAD_DIM, NSA_TQ
    n_rows = s // CMP_STRIDE
    assert n_rows == LANES and s // SLC_LEN <= 32 and s % tq == 0
    cover = _cover_matrix(n_rows, s // SLC_LEN)
    qw = NSA_REP * hd
    gate_blk = _SEGS["g_nsa"][0] // LANES
    q_blk = _SEGS["q_nsa"][0] // qw
    seq_g = pl.BlockSpec((1, s, hd), lambda i, j, t: (i, 0, j))
    cmp_g = pl.BlockSpec((1, n_rows, hd), lambda i, j, t: (i, 0, j))
    tab = pl.BlockSpec((tq, hd), lambda i, j, t: (t, 0))
    return pl.pallas_call(
        functools.partial(_nsa_kernel, n_slc=s // SLC_LEN),
        out_shape=jax.ShapeDtypeStruct((b, s, NSA_Q_W), BF16),
        grid=(b, g, s // tq),
        in_specs=[pl.BlockSpec((1, tq, qw), lambda i, j, t: (i, t, q_blk + j)),
                  pl.BlockSpec((1, tq, LANES), lambda i, j, t: (i, t, gate_blk + j)),
                  cmp_g, cmp_g, seq_g, seq_g, seq_g, seq_g, tab, tab,
                  pl.BlockSpec(cover.shape, lambda i, j, t: (0, 0))],
        out_specs=pl.BlockSpec((1, tq, qw), lambda i, j, t: (i, t, j)),
        scratch_shapes=[pltpu.VMEM((NSA_REP * tq, hd), F32)],
        compiler_params=pltpu.CompilerParams(dimension_semantics=("parallel", "parallel", "arbitrary"),
                                             vmem_limit_bytes=VMEM_LIMIT),
        name="nsa_attention",
    )(y3, y3, kc, vc, ks, vs, kw, vw, cos, sin, cover)


FFN_TM = 256
FFN_NCH = 8
_STAGE, _COMPUTE, _ZERO = 0, 1, 2


def _ffn_kernel(kind_ref, e_ref, c_ref, blk_ref, x_ref, w_ref, wg_ref, wu_ref, wd_ref, o_ref,
                wg_s, wu_s, wd_s):
    s = pl.program_id(0)
    kind = kind_ref[s]
    rg = wg_ref.shape[1]
    rd = wd_ref.shape[1]

    @pl.when(kind == _STAGE)
    def _():
        c = c_ref[s]
        r0 = pl.multiple_of(c * rg, rg)
        wg_s[pl.ds(r0, rg), :] = wg_ref[0].astype(BF16)
        wu_s[pl.ds(r0, rg), :] = wu_ref[0].astype(BF16)
        r1 = pl.multiple_of(c * rd, rd)
        wd_s[pl.ds(r1, rd), :] = wd_ref[0].astype(BF16)

    @pl.when(kind == _COMPUTE)
    def _():
        x = x_ref[...]
        gate = jnp.dot(x, wg_s[...], preferred_element_type=F32)
        up = jnp.dot(x, wu_s[...], preferred_element_type=F32)
        h = (jax.nn.silu(gate) * up).astype(BF16)
        o_ref[...] = jnp.dot(h, wd_s[...], preferred_element_type=F32) * w_ref[...]

    @pl.when(kind == _ZERO)
    def _():
        o_ref[...] = jnp.zeros_like(o_ref)


def _ffn_items(nb_e, n_blocks):
    e = nb_e.shape[0]
    blk_start = jnp.cumsum(nb_e) - nb_e
    n_used = blk_start[-1] + nb_e[-1]
    base = FFN_NCH * jnp.arange(e) + blk_start
    s = jnp.arange(FFN_NCH * e + n_blocks)
    e_s = (s[:, None] >= base[None, :]).sum(1) - 1
    r = s - base[e_s]
    end = FFN_NCH * e + n_used
    kind = jnp.where(s >= end, _ZERO, jnp.where(r < FFN_NCH, _STAGE, _COMPUTE))
    c_s = jnp.minimum(r, FFN_NCH - 1)
    blk = jnp.where(kind == _ZERO, n_used + s - end,
                    jnp.where(kind == _STAGE, blk_start[e_s], blk_start[e_s] + r - FFN_NCH))
    blk = jnp.minimum(blk, n_blocks - 1)
    i32 = lambda a: a.astype(jnp.int32)
    return i32(kind), i32(e_s), i32(c_s), i32(blk)


def _grouped_ffn(xs, row_w, nb_e, wg, wu, wd):
    p, d = xs.shape
    e, _, f = wg.shape
    n_blocks = p // FFN_TM
    assert d % FFN_NCH == 0 and f % (FFN_NCH * 16) == 0
    rg, rd = d // FFN_NCH, f // FFN_NCH
    kind, e_s, c_s, blk = _ffn_items(nb_e, n_blocks)
    row = lambda w: pl.BlockSpec((FFN_TM, w), lambda s, k, es, cs, bs: (bs[s], 0))
    chunk = lambda r, w: pl.BlockSpec((1, r, w), lambda s, k, es, cs, bs: (es[s], cs[s], 0))
    grid_spec = pltpu.PrefetchScalarGridSpec(
        num_scalar_prefetch=4,
        grid=(FFN_NCH * e + n_blocks,),
        in_specs=[row(d), row(1), chunk(rg, f), chunk(rg, f), chunk(rd, d)],
        out_specs=row(d),
        scratch_shapes=[pltpu.VMEM((d, f), BF16), pltpu.VMEM((d, f), BF16), pltpu.VMEM((f, d), BF16)],
    )
    return pl.pallas_call(
        _ffn_kernel,
        out_shape=jax.ShapeDtypeStruct((p, d), F32),
        grid_spec=grid_spec,
        compiler_params=pltpu.CompilerParams(dimension_semantics=("arbitrary",), vmem_limit_bytes=VMEM_LIMIT),
        name="grouped_ffn",
    )(kind, e_s, c_s, blk, xs, row_w, wg, wu, wd)


FFN_IDX_SLOTS = 4
FFN_ROW_PITCH = 24


def _routed_ffn_kernel(kind_ref, e_ref, c_ref, blk_ref, nblk_ref, x_hbm, tab_hbm, w_ref, wg_ref, wu_ref,
                       wd_ref, y_hbm, wg_s, wu_s, wd_s, xbuf0, xbuf1, obuf0, obuf1, zbuf, idx_s,
                       isem, gsem, ssem, zsem):
    s = pl.program_id(0)
    kind = kind_ref[s]
    rg = wg_ref.shape[1]
    rd = wd_ref.shape[1]
    tm = FFN_TM
    last_blk = tab_hbm.shape[0] - 1
    ns = wd_ref.shape[2] // LANES
    xbufs, obufs = (xbuf0, xbuf1), (obuf0, obuf1)

    def idx_copy(b):
        return pltpu.make_async_copy(tab_hbm.at[jnp.minimum(b, last_blk)], idx_s.at[b % FFN_IDX_SLOTS],
                                     isem.at[b % FFN_IDX_SLOTS])

    def gather_copy(b, par, r):
        tok = idx_s[b % FFN_IDX_SLOTS, r]
        return pltpu.make_async_copy(x_hbm.at[pl.ds(pl.multiple_of(tok * ns, ns), ns), :],
                                     xbufs[par].at[pl.ds(r * FFN_ROW_PITCH, ns), :], gsem.at[par])

    def scatter_copy(b, par, r):
        dst = idx_s[b % FFN_IDX_SLOTS, tm + r]
        return pltpu.make_async_copy(obufs[par].at[pl.ds(r * FFN_ROW_PITCH, ns), :],
                                     y_hbm.at[pl.ds(pl.multiple_of(dst * ns, ns), ns), :], ssem.at[par])

    @pl.when(s == 0)
    def _():
        zbuf[...] = jnp.zeros_like(zbuf)
        idx_copy(0).start()
        idx_copy(1).start()
        idx_copy(0).wait()
        for r in range(tm):
            gather_copy(0, 0, r).start()

    @pl.when(kind == _STAGE)
    def _():
        c = c_ref[s]
        r0 = pl.multiple_of(c * rg, rg)
        wg_s[pl.ds(r0, rg), :] = wg_ref[0].astype(BF16)
        wu_s[pl.ds(r0, rg), :] = wu_ref[0].astype(BF16)
        r1 = pl.multiple_of(c * rd, rd)
        wd_s[pl.ds(r1, rd), :] = wd_ref[0].astype(BF16)

    def compute(b, par, first):
        idx_copy(b + 1).wait()
        for r in range(tm):
            gather_copy(b, par, r).wait()
        if not first:
            @pl.when(b >= 2)
            def _():
                for r in range(tm):
                    scatter_copy(b - 2, par, r).wait()

            for r in range(tm):
                scatter_copy(b - 1, 1 - par, r).start()
        for r in range(tm):
            gather_copy(b + 1, 1 - par, r).start()
        idx_copy(b + 2).start()
        x = jnp.concatenate([xbufs[par][pl.ds(j, tm, stride=FFN_ROW_PITCH), :] for j in range(ns)],
                            axis=1).astype(BF16)
        gate = jnp.dot(x, wg_s[...], preferred_element_type=F32)

        @pl.when(b >= 0)
        def _():
            up = jnp.dot(x, wu_s[...], preferred_element_type=F32)
            h = (jax.nn.silu(gate) * up).astype(BF16)
            out = jnp.dot(h, wd_s[...], preferred_element_type=F32) * w_ref[...]
            for j in range(ns):
                obufs[par][pl.ds(j, tm, stride=FFN_ROW_PITCH), :] = out[:, j * LANES:(j + 1) * LANES]

    is_compute = kind == _COMPUTE
    blk = blk_ref[s]
    pl.when(is_compute & (blk == 0))(lambda: compute(blk, 0, True))
    pl.when(is_compute & (blk > 0) & (blk % 2 == 0))(lambda: compute(blk, 0, False))
    pl.when(is_compute & (blk % 2 == 1))(lambda: compute(blk, 1, False))

    @pl.when(kind == _ZERO)
    def _():
        row0 = pl.multiple_of(blk * (tm * ns), tm * ns)
        tail = pltpu.make_async_copy(zbuf, y_hbm.at[pl.ds(row0, tm * ns), :], zsem.at[0])
        tail.start()
        tail.wait()

    def drain(n_used, par):
        idx_copy(n_used + 1).wait()
        for r in range(tm):
            gather_copy(n_used, par, r).wait()
        for r in range(tm):
            scatter_copy(n_used - 1, 1 - par, r).start()
        for r in range(tm):
            scatter_copy(n_used - 2, par, r).wait()
        for r in range(tm):
            scatter_copy(n_used - 1, 1 - par, r).wait()

    is_last = s == pl.num_programs(0) - 1
    n_used = nblk_ref[0]
    pl.when(is_last & (n_used % 2 == 0))(lambda: drain(n_used, 0))
    pl.when(is_last & (n_used % 2 == 1))(lambda: drain(n_used, 1))


def _routed_ffn(x, tab, row_w, nb_e, wg, wu, wd):
    e, d, f = wg.shape
    n_blocks = tab.shape[0]
    ns = d // LANES
    assert ns % 8 == 0 and ns <= FFN_ROW_PITCH and x.shape[1] == LANES
    row_buf = pltpu.VMEM((FFN_TM * FFN_ROW_PITCH, LANES), F32)
    zero_buf = pltpu.VMEM((FFN_TM * ns, LANES), F32)
    assert d % FFN_NCH == 0 and f % (FFN_NCH * 16) == 0 and tab.shape[1] == 2 * FFN_TM
    rg, rd = d // FFN_NCH, f // FFN_NCH
    kind, e_s, c_s, blk = _ffn_items(nb_e, n_blocks)
    n_used = jnp.sum(nb_e).astype(jnp.int32).reshape(1)
    chunk = lambda r, w: pl.BlockSpec((1, r, w), lambda s, k, es, cs, bs, nu: (es[s], cs[s], 0))
    grid_spec = pltpu.PrefetchScalarGridSpec(
        num_scalar_prefetch=5,
        grid=(FFN_NCH * e + n_blocks,),
        in_specs=[pl.BlockSpec(memory_space=pl.ANY), pl.BlockSpec(memory_space=pl.ANY),
                  pl.BlockSpec((FFN_TM, 1), lambda s, k, es, cs, bs, nu: (bs[s], 0)),
                  chunk(rg, f), chunk(rg, f), chunk(rd, d)],
        out_specs=pl.BlockSpec(memory_space=pl.ANY),
        scratch_shapes=[pltpu.VMEM((d, f), BF16), pltpu.VMEM((d, f), BF16), pltpu.VMEM((f, d), BF16),
                        row_buf, row_buf, row_buf, row_buf, zero_buf,
                        pltpu.SMEM((FFN_IDX_SLOTS, 2 * FFN_TM), jnp.int32),
                        pltpu.SemaphoreType.DMA((FFN_IDX_SLOTS,)), pltpu.SemaphoreType.DMA((2,)),
                        pltpu.SemaphoreType.DMA((2,)), pltpu.SemaphoreType.DMA((1,))],
    )
    return pl.pallas_call(
        _routed_ffn_kernel,
        out_shape=jax.ShapeDtypeStruct((n_blocks * FFN_TM * ns, LANES), F32),
        grid_spec=grid_spec,
        compiler_params=pltpu.CompilerParams(dimension_semantics=("arbitrary",), vmem_limit_bytes=VMEM_LIMIT),
        name="routed_ffn",
    )(kind, e_s, c_s, blk, n_used, x, tab, row_w, wg, wu, wd)


ML_L = 256
ML_HALO = 8
MLSTM_CONV = 4


def _mlstm_kernel(qk_ref, v_ref, o_ref, if_ref, convw_ref, convb_ref, ifb_ref, ng_ref, tril_ref,
                  out_ref, c_ref, m_ref, halo_ref):
    L, NH, DK, DV = ML_L, MLSTM_HEADS, MLSTM_QK_DIM, MLSTM_V_DIM

    @pl.when(pl.program_id(1) == 0)
    def _():
        c_ref[...] = jnp.zeros_like(c_ref)
        m_ref[...] = jnp.zeros_like(m_ref)
        halo_ref[...] = jnp.zeros_like(halo_ref)

    x = qk_ref[0]
    cat = jnp.concatenate([halo_ref[...], x], axis=0)
    halo_ref[...] = x[L - ML_HALO:, :]
    y = convb_ref[...]
    for j in range(MLSTM_CONV):
        lo = ML_HALO - (MLSTM_CONV - 1) + j
        y = y + convw_ref[j:j + 1, :] * cat[lo:lo + L, :]
    qk = y * jax.nn.sigmoid(y)

    lane = lax.broadcasted_iota(jnp.int32, (L, LANES), 1)
    z = if_ref[0] + ifb_ref[...]
    lg = jnp.where(lane < NH, z, jnp.minimum(z, 0.0) - jnp.log1p(jnp.exp(-jnp.abs(z))))
    hi = lg.astype(BF16)
    r1 = lg - hi.astype(F32)
    mid = r1.astype(BF16)
    lo3 = (r1 - mid.astype(F32)).astype(BF16)
    tril = tril_ref[...]
    bcum = (jnp.dot(tril, hi, preferred_element_type=F32) + jnp.dot(tril, mid, preferred_element_type=F32)
            + jnp.dot(tril, lo3, preferred_element_type=F32))

    row = lax.broadcasted_iota(jnp.int32, (L, L), 0)
    colm = lax.broadcasted_iota(jnp.int32, (L, L), 1)
    tri = row >= colm
    ones_col = (lane == 0).astype(BF16)
    for h in range(NH):
        li = lg[:, h:h + 1]
        b = bcum[:, NH + h:NH + h + 1]
        m_prev = m_ref[h:h + 1, 0:1]
        a_col = li - b
        a_row = jnp.transpose(jnp.broadcast_to(a_col, (L, LANES)))[0:1, :]
        dmat = jnp.where(tri, b + a_row, -jnp.inf)
        m_inter = b + m_prev
        m_t = jnp.maximum(m_inter, dmat.max(-1, keepdims=True))
        q = qk[:, h * DK:(h + 1) * DK].astype(BF16)
        k = qk[:, NH * DK + h * DK:NH * DK + (h + 1) * DK] * DK ** -0.5
        s = lax.dot_general(q, k.astype(BF16), (((1,), (1,)), ((), ())), preferred_element_type=F32)
        wqk = (jnp.exp(dmat - m_t) * s).astype(BF16)
        inter = jnp.exp(m_inter - m_t)
        v_ext = jnp.concatenate([v_ref[0, :, h * DV:(h + 1) * DV].astype(BF16), ones_col], axis=1)
        c_old = c_ref[h]
        nd = (jnp.dot(wqk, v_ext, preferred_element_type=F32)
              + inter * jnp.dot(q, c_old.astype(BF16), preferred_element_type=F32))
        num = nd[:, :DV]
        den = nd[:, DV:DV + 1]
        hh = num / jnp.maximum(jnp.abs(den), jnp.exp(-m_t))
        b_last = b[L - 1:L, :]
        g = b_last - b + li
        m_new = jnp.maximum(b_last + m_prev, g.max(0, keepdims=True))
        decay = jnp.exp(b_last + m_prev - m_new)
        w_s = jnp.exp(g - m_new)
        kw_t = jnp.transpose(k * w_s).astype(BF16)
        c_ref[h] = decay * c_old + jnp.dot(kw_t, v_ext, preferred_element_type=F32)
        m_ref[h:h + 1, :] = jnp.broadcast_to(m_new, (1, LANES))
        mu = hh.mean(-1, keepdims=True)
        var = jnp.mean(jnp.square(hh - mu), -1, keepdims=True)
        hn = (hh - mu) * lax.rsqrt(var + LN_EPS) * ng_ref[:, h * DV:(h + 1) * DV]
        out_ref[0, :, h * DV:(h + 1) * DV] = (
            jax.nn.sigmoid(o_ref[0, :, h * DV:(h + 1) * DV]) * hn).astype(out_ref.dtype)


def _mlstm(y3, conv_w, conv_b, if_bias, norm_g):
    b, s, _ = y3.shape
    L, NH = ML_L, MLSTM_HEADS
    assert s % L == 0

    def col(name, width):
        blk = _SEGS[name][0] // width
        return pl.BlockSpec((1, L, width), lambda i, j: (i, j, blk))

    full = lambda shp: pl.BlockSpec(shp, lambda i, j: (0,) * len(shp))
    ifb = jnp.pad(if_bias.reshape(1, 2 * NH), ((0, 0), (0, LANES - 2 * NH)))
    tril = jnp.asarray(np.tril(np.ones((L, L), np.float32)), dtype=BF16)
    conv_b = conv_b.reshape(1, -1)
    norm_g = norm_g.reshape(1, -1)
    return pl.pallas_call(
        _mlstm_kernel,
        out_shape=jax.ShapeDtypeStruct((b, s, ML_V_W), BF16),
        grid=(b, s // L),
        in_specs=[col("qk_ml", 2 * ML_QK_W), col("v_ml", ML_V_W), col("o_ml", ML_V_W), col("if_ml", LANES),
                  full(conv_w.shape), full(conv_b.shape), full(ifb.shape), full(norm_g.shape), full(tril.shape)],
        out_specs=pl.BlockSpec((1, L, ML_V_W), lambda i, j: (i, j, 0)),
        scratch_shapes=[pltpu.VMEM((NH, MLSTM_QK_DIM, MLSTM_V_DIM + LANES), F32),
                        pltpu.VMEM((8, LANES), F32),
                        pltpu.VMEM((ML_HALO, 2 * ML_QK_W), F32)],
        compiler_params=pltpu.CompilerParams(dimension_semantics=("parallel", "arbitrary"),
                                             vmem_limit_bytes=VMEM_LIMIT),
        name="mlstm",
    )(y3, y3, y3, y3, conv_w, conv_b, ifb, norm_g, tril)


MEM_TQ = 512


def _mem_attn_kernel(q_ref, kv_ref, o_ref):
    hd = MEM_HEAD_DIM
    for h in range(MEM_HEADS):
        q = q_ref[0, :, h * hd:(h + 1) * hd].astype(BF16)
        k = kv_ref[0, :, h * hd:(h + 1) * hd]
        v = kv_ref[0, :, MEM_W + h * hd:MEM_W + (h + 1) * hd]
        s = lax.dot_general(q, k, (((1,), (1,)), ((), ())), preferred_element_type=F32) * hd ** -0.5
        e = jnp.exp(s - s.max(-1, keepdims=True))
        p = (e / e.sum(-1, keepdims=True)).astype(BF16)
        o_ref[0, :, h * hd:(h + 1) * hd] = jnp.dot(p, v, preferred_element_type=F32).astype(o_ref.dtype)


def _mem_attention(y3, kv_mem):
    b, s, _ = y3.shape
    m = kv_mem.shape[1]
    q_blk = _SEGS["q_mem"][0] // MEM_W
    return pl.pallas_call(
        _mem_attn_kernel,
        out_shape=jax.ShapeDtypeStruct((b, s, MEM_W), BF16),
        grid=(b, s // MEM_TQ),
        in_specs=[pl.BlockSpec((1, MEM_TQ, MEM_W), lambda i, j: (i, j, q_blk)),
                  pl.BlockSpec((1, m, 2 * MEM_W), lambda i, j: (i, 0, 0))],
        out_specs=pl.BlockSpec((1, MEM_TQ, MEM_W), lambda i, j: (i, j, 0)),
        compiler_params=pltpu.CompilerParams(dimension_semantics=("parallel", "parallel"),
                                             vmem_limit_bytes=VMEM_LIMIT),
        name="mem_attention",
    )(y3, kv_mem)


MERGE_TM = 256


def _merge_kernel(yn_ref, ym_ref, yc_ref, g0_ref, g1_ref, g2_ref, x_ref, wb_ref, wo_ref, lg_ref, lb_ref,
                  x1_ref, x1b_ref, x1s_ref):
    merged = None
    for i, (y_ref, g_ref) in enumerate(((yn_ref, g0_ref), (ym_ref, g1_ref), (yc_ref, g2_ref))):
        t = jax.nn.sigmoid(g_ref[...]) * jnp.dot(y_ref[...], wb_ref[i], preferred_element_type=F32)
        merged = t if merged is None else merged + t
    z = DN_ALPHA * x_ref[...] + jnp.dot(merged.astype(BF16), wo_ref[...], preferred_element_type=F32)
    mu = z.mean(-1, keepdims=True)
    var = jnp.mean(jnp.square(z - mu), -1, keepdims=True)
    x1 = (z - mu) * lax.rsqrt(var + LN_EPS) * lg_ref[...] + lb_ref[...]
    x1_ref[...] = x1
    x1b_ref[...] = x1.astype(BF16)
    tm, d = x1.shape
    ns = d // LANES
    for j in range(ns):
        x1s_ref[pl.ds(j, tm, stride=ns), :] = x1[:, j * LANES:(j + 1) * LANES]


def _merge_ln1(y_nsa, y_ml, y_mem, y2, xt, wb, wo, ln_g, ln_b):
    t, d = xt.shape
    tm = MERGE_TM
    bw = y_nsa.shape[1]
    resident = pl.Buffered(1)
    row = lambda w: pl.BlockSpec((tm, w), lambda i: (i, 0))
    gate = lambda k: pl.BlockSpec((tm, d), lambda i: (i, k))
    return pl.pallas_call(
        _merge_kernel,
        out_shape=(jax.ShapeDtypeStruct((t, d), F32), jax.ShapeDtypeStruct((t, d), BF16),
                   jax.ShapeDtypeStruct((t * (d // LANES), LANES), F32)),
        grid=(t // tm,),
        in_specs=[row(bw), row(bw), row(bw), gate(0), gate(1), gate(2), row(d),
                  pl.BlockSpec((N_BRANCH, bw, d), lambda i: (0, 0, 0), pipeline_mode=resident),
                  pl.BlockSpec((d, d), lambda i: (0, 0), pipeline_mode=resident),
                  pl.BlockSpec((1, d), lambda i: (0, 0)), pl.BlockSpec((1, d), lambda i: (0, 0))],
        out_specs=(row(d), row(d), pl.BlockSpec((tm * (d // LANES), LANES), lambda i: (i, 0))),
        compiler_params=pltpu.CompilerParams(dimension_semantics=("parallel",), vmem_limit_bytes=VMEM_LIMIT),
        name="merge_ln1",
    )(y_nsa, y_ml, y_mem, y2, y2, y2, xt, wb, wo, ln_g.reshape(1, d), ln_b.reshape(1, d))


ROUTER_TM = 256


def _router_kernel(x_ref, w_ref, bias_ref, idx_ref, wt_ref, cnt_ref, run_ref):
    tm = ROUTER_TM

    @pl.when(pl.program_id(0) == 0)
    def _():
        run_ref[...] = jnp.zeros_like(run_ref)

    x = x_ref[...]
    w = w_ref[...]
    x_hi = x.astype(BF16)
    x_lo = (x - x_hi.astype(F32)).astype(BF16)
    w_hi = w.astype(BF16)
    w_lo = (w - w_hi.astype(F32)).astype(BF16)
    logits = (jnp.dot(x_hi, w_hi, preferred_element_type=F32) + jnp.dot(x_lo, w_hi, preferred_element_type=F32)
              + jnp.dot(x_hi, w_lo, preferred_element_type=F32))
    lane = lax.broadcasted_iota(jnp.int32, (tm, LANES), 1)
    aff = jax.nn.sigmoid(logits)
    score = jnp.where(lane < N_EXPERTS, aff + bias_ref[...], -jnp.inf)
    idx_out = jnp.zeros((tm, LANES), jnp.int32)
    aff_out = jnp.zeros((tm, LANES), F32)
    chosen = jnp.zeros((tm, LANES), jnp.bool_)
    for k in range(TOP_K):
        best = score.max(-1, keepdims=True)
        cur = jnp.where(score == best, lane, LANES).min(-1, keepdims=True)
        here = lane == cur
        idx_out = jnp.where(lane == k, cur, idx_out)
        aff_out = jnp.where(lane == k, jnp.where(here, aff, 0.0).sum(-1, keepdims=True), aff_out)
        chosen = chosen | here
        score = jnp.where(here, -jnp.inf, score)
    run_new = run_ref[0:1, :] + chosen.astype(F32).sum(0, keepdims=True)
    run_ref[...] = jnp.broadcast_to(run_new, run_ref.shape)
    cnt_ref[...] = jnp.broadcast_to(run_new, cnt_ref.shape)
    idx_ref[...] = idx_out
    wt_ref[...] = aff_out / aff_out.sum(-1, keepdims=True) * ROUTED_SCALE


def _router(x1, router_w, router_bias):
    t, d = x1.shape
    tm = ROUTER_TM
    e = router_w.shape[1]
    w = jnp.pad(router_w, ((0, 0), (0, LANES - e)))
    bias = jnp.pad(router_bias.reshape(1, e), ((0, 0), (0, LANES - e)))
    out = pl.BlockSpec((tm, LANES), lambda i: (i, 0))
    const = lambda shp: pl.BlockSpec(shp, lambda i: (0, 0))
    idx, wt, cnt = pl.pallas_call(
        _router_kernel,
        out_shape=(jax.ShapeDtypeStruct((t, LANES), jnp.int32), jax.ShapeDtypeStruct((t, LANES), F32),
                   jax.ShapeDtypeStruct((8, LANES), F32)),
        grid=(t // tm,),
        in_specs=[pl.BlockSpec((tm, d), lambda i: (i, 0)), const(w.shape), const(bias.shape)],
        out_specs=(out, out, const((8, LANES))),
        scratch_shapes=[pltpu.VMEM((8, LANES), F32)],
        compiler_params=pltpu.CompilerParams(dimension_semantics=("arbitrary",), vmem_limit_bytes=VMEM_LIMIT),
        name="router",
    )(x1, w, bias)
    return idx[:, :TOP_K], wt[:, :TOP_K], cnt[0, :e].astype(jnp.int32)


def _moe_layout(idx, wts, counts):
    t = idx.shape[0]
    e = N_EXPERTS
    m = t * TOP_K
    assert m // FFN_TM >= 2
    nb_e = (counts + FFN_TM - 1) // FFN_TM
    blk_start = jnp.cumsum(nb_e) - nb_e
    start = jnp.cumsum(counts) - counts
    n_blocks = m // FFN_TM + e
    order = jnp.argsort(idx.reshape(m))
    bi = jnp.arange(n_blocks)
    blk_e = jnp.clip((bi[:, None] >= blk_start[None, :]).sum(1) - 1, 0, e - 1)
    j = ((bi - blk_start[blk_e]) * FFN_TM)[:, None] + jnp.arange(FFN_TM)[None, :]
    valid = j < counts[blk_e][:, None]
    flat = order[jnp.clip(start[blk_e][:, None] + j, 0, m - 1)]
    tok, slot = flat // TOP_K, flat % TOP_K
    pad_before = (blk_start * FFN_TM - start)[blk_e] - counts[blk_e]
    row_tok = jnp.where(valid, tok, 0)
    dest = jnp.where(valid, slot * t + tok, m + pad_before[:, None] + j)
    row_w = jnp.where(valid, wts.reshape(m)[flat], 0.0).reshape(n_blocks * FFN_TM, 1)
    tab = jnp.concatenate([row_tok, dest], axis=1).astype(jnp.int32)
    return tab, row_w, nb_e


def _combine_kernel(x_ref, *refs):
    slot_refs, (s_ref, g_ref, b_ref, o_ref) = refs[:TOP_K], refs[TOP_K:]
    tm, d = x_ref.shape
    ns = d // LANES

    def rows(r_ref):
        return jnp.concatenate([r_ref[pl.ds(j, tm, stride=ns), :] for j in range(ns)], axis=1)

    routed = rows(slot_refs[0])
    for r_ref in slot_refs[1:]:
        routed = routed + rows(r_ref)
    z = DN_ALPHA * x_ref[...] + (routed + s_ref[...])
    mu = z.mean(-1, keepdims=True)
    var = jnp.mean(jnp.square(z - mu), -1, keepdims=True)
    o_ref[...] = (z - mu) * lax.rsqrt(var + LN_EPS) * g_ref[...] + b_ref[...]


def _combine_ln2(x1, y_slots, shared, ln_g, ln_b, *, tm=256):
    t, d = x1.shape
    nt = t // tm
    ns = d // LANES
    row = pl.BlockSpec((tm, d), lambda i: (i, 0))
    vec = pl.BlockSpec((1, d), lambda i: (0, 0))
    slots = [pl.BlockSpec((tm * ns, LANES), functools.partial(lambda i, k: (k * nt + i, 0), k=k))
             for k in range(TOP_K)]
    return pl.pallas_call(
        _combine_kernel, out_shape=jax.ShapeDtypeStruct((t, d), F32), grid=(nt,),
        in_specs=[row] + slots + [row, vec, vec], out_specs=row,
        compiler_params=pltpu.CompilerParams(dimension_semantics=("parallel",), vmem_limit_bytes=VMEM_LIMIT),
        name="combine_ln2",
    )(x1, *([y_slots] * TOP_K), shared, ln_g.reshape(1, d), ln_b.reshape(1, d))


def _layer(x, mem, w_in, mlstm_conv_w, mlstm_conv_b, mlstm_if_bias, mlstm_norm_g,
           cmp_pos_k, cmp_k_w1, cmp_k_w2, cmp_pos_v, cmp_v_w1, cmp_v_w2,
           w_mem_kv, w_branch, w_o, ln1_g, ln1_b, router_w, router_bias,
           w_e_gate, w_e_up, w_e_down, w_s_gate, w_s_up, w_s_down, ln2_g, ln2_b):
    B, S, D = x.shape
    T = B * S
    xt = x.reshape(T, D)
    y2 = _matmul(xt.astype(BF16), _pad_in_proj(w_in).astype(BF16), tm=1024, tn=IN_TN, name="in_proj")
    y3 = y2.reshape(B, S, IN_PAD)

    cos, sin = _rope_tables(jnp.arange(S))
    cos_e, sin_e = _rope_tables(jnp.arange(S // CMP_STRIDE) * CMP_STRIDE + CMP_LEN - 1)
    kvc, ks, vs, kw, vw = _nsa_prep(y3, cos, sin)
    kc, vc = _compress(kvc, cmp_pos_k, cmp_k_w1, cmp_k_w2, cmp_pos_v, cmp_v_w1, cmp_v_w2, cos_e, sin_e)
    y_nsa = _nsa_attention(y3, kc, vc, ks, vs, kw, vw, cos, sin)

    y_ml = _mlstm(y3, mlstm_conv_w, mlstm_conv_b, mlstm_if_bias, mlstm_norm_g)

    m_len = mem.shape[1]
    kv_mem = _matmul(mem.reshape(B * m_len, D).astype(BF16), w_mem_kv.astype(BF16), tm=1024, tn=512,
                     out_dtype=BF16, name="mem_kv").reshape(B, m_len, 2 * MEM_W)
    y_mem = _mem_attention(y3, kv_mem)

    x1, x1b, x1s = _merge_ln1(y_nsa.reshape(T, -1), y_ml.reshape(T, -1), y_mem.reshape(T, -1), y2, xt,
                         w_branch.astype(BF16), w_o.astype(BF16), ln1_g, ln1_b)

    idx, wts, counts = _router(x1, router_w, router_bias)
    tab, row_w, nb_e = _moe_layout(idx, wts, counts)
    y_slots = _routed_ffn(x1s, tab, row_w, nb_e, w_e_gate, w_e_up, w_e_down)
    shared = _grouped_ffn(x1b, jnp.ones((T, 1), F32), jnp.full((1,), T // FFN_TM, jnp.int32),
                          w_s_gate[None], w_s_up[None], w_s_down[None])
    return _combine_ln2(x1, y_slots, shared, ln2_g, ln2_b).reshape(B, S, D)


def kernel(x, mem, w_in, mlstm_conv_w, mlstm_conv_b, mlstm_if_bias, mlstm_norm_g, cmp_pos_k, cmp_k_w1, cmp_k_w2, cmp_pos_v, cmp_v_w1, cmp_v_w2, w_mem_kv, w_branch, w_o, ln1_g, ln1_b, router_w, router_bias, w_e_gate, w_e_up, w_e_down, w_s_gate, w_s_up, w_s_down, ln2_g, ln2_b):
    for layer in range(DEPTH):
        x = _layer(x, mem, w_in[layer], mlstm_conv_w[layer], mlstm_conv_b[layer],
                   mlstm_if_bias[layer], mlstm_norm_g[layer],
                   cmp_pos_k[layer], cmp_k_w1[layer], cmp_k_w2[layer],
                   cmp_pos_v[layer], cmp_v_w1[layer], cmp_v_w2[layer],
                   w_mem_kv[layer], w_branch[layer], w_o[layer], ln1_g[layer], ln1_b[layer],
                   router_w[layer], router_bias[layer], w_e_gate[layer], w_e_up[layer],
                   w_e_down[layer], w_s_gate[layer], w_s_up[layer], w_s_down[layer],
                   ln2_g[layer], ln2_b[layer])
    return x
```

```python
import functools

import jax
import jax.numpy as jnp
import numpy as np
from jax import lax
from jax.experimental import pallas as pl
from jax.experimental.pallas import tpu as pltpu

F32 = jnp.float32
BF16 = jnp.bfloat16

D_MODEL = 2048
HEAD_DIM = 128
NSA_HEADS = 8
NSA_KV_GROUPS = 2
NSA_REP = NSA_HEADS // NSA_KV_GROUPS
CMP_LEN = 32
CMP_STRIDE = 16
SLC_LEN = 64
SLC_TOPK = 16
WINDOW = 512
MLSTM_HEADS = 4
MLSTM_QK_DIM = 128
MLSTM_V_DIM = 256
MLSTM_CHUNK = 64
MEM_HEADS = 4
MEM_HEAD_DIM = 256
N_EXPERTS = 64
TOP_K = 6
EXPERT_DIM = 1408
ROUTED_SCALE = 2.5
ROPE_THETA = 10000.0
LN_EPS = 1e-5
NEG_INF = -1e30
DEPTH = 1
DN_ALPHA = (2 * DEPTH) ** 0.25

NSA_Q_W = NSA_HEADS * HEAD_DIM
NSA_KV_W = NSA_KV_GROUPS * HEAD_DIM
ML_QK_W = MLSTM_HEADS * MLSTM_QK_DIM
ML_V_W = MLSTM_HEADS * MLSTM_V_DIM
MEM_W = MEM_HEADS * MEM_HEAD_DIM
N_BRANCH = 3

LANES = 128
VMEM_LIMIT = 56 * 1024 * 1024

_SEGS = {}
_off = 0
for _name, _w, _blk in (("g_br", N_BRANCH * D_MODEL, D_MODEL), ("q_nsa", NSA_Q_W, NSA_REP * HEAD_DIM),
                        ("q_mem", MEM_W, MEM_W), ("v_ml", ML_V_W, ML_V_W), ("o_ml", ML_V_W, ML_V_W),
                        ("qk_ml", 2 * ML_QK_W, 2 * ML_QK_W), ("kv_cmp", 2 * NSA_KV_W, 2 * NSA_KV_W),
                        ("k_slc", NSA_KV_W, NSA_KV_W), ("v_slc", NSA_KV_W, NSA_KV_W),
                        ("k_win", NSA_KV_W, NSA_KV_W), ("v_win", NSA_KV_W, NSA_KV_W),
                        ("g_nsa", NSA_KV_GROUPS * LANES, LANES), ("if_ml", LANES, LANES)):
    assert _off % _blk == 0
    _SEGS[_name] = (_off, _w)
    _off += _w
IN_TN = 512
IN_PAD = -(-_off // IN_TN) * IN_TN


def _pad_in_proj(w_in):
    sizes = (NSA_Q_W,) + (NSA_KV_W,) * 6 + (NSA_HEADS * 3, ML_QK_W, ML_QK_W, ML_V_W, ML_V_W,
                                            MLSTM_HEADS, MLSTM_HEADS, MEM_W, N_BRANCH * D_MODEL)
    splits = np.cumsum(sizes)[:-1].tolist()
    (q_nsa, k_cmp, v_cmp, k_slc, v_slc, k_win, v_win, g_nsa,
     q_ml, k_ml, v_ml, o_ml, i_ml, f_ml, q_mem, g_br) = jnp.split(w_in, splits, axis=-1)
    per_group = NSA_REP * 3
    g_blocks = [jnp.pad(g_nsa[:, g * per_group:(g + 1) * per_group], ((0, 0), (0, LANES - per_group)))
                for g in range(NSA_KV_GROUPS)]
    if_blk = jnp.pad(jnp.concatenate([i_ml, f_ml], -1), ((0, 0), (0, LANES - 2 * MLSTM_HEADS)))
    cols = [g_br, q_nsa, q_mem, v_ml, o_ml, q_ml, k_ml, k_cmp, v_cmp,
            k_slc, v_slc, k_win, v_win] + g_blocks + [if_blk]
    w = jnp.concatenate(cols, axis=-1)
    return jnp.pad(w, ((0, 0), (0, IN_PAD - w.shape[1])))


def _mm_kernel(a_ref, b_ref, o_ref):
    o_ref[...] = jnp.dot(a_ref[...].astype(BF16), b_ref[...].astype(BF16),
                         preferred_element_type=F32).astype(o_ref.dtype)


def _matmul(a, b, *, tm, tn, out_dtype=F32, name="matmul"):
    m, k = a.shape
    _, n = b.shape
    assert m % tm == 0 and n % tn == 0
    return pl.pallas_call(
        _mm_kernel,
        out_shape=jax.ShapeDtypeStruct((m, n), out_dtype),
        grid=(m // tm, n // tn),
        in_specs=[pl.BlockSpec((tm, k), lambda i, j: (i, 0)),
                  pl.BlockSpec((k, tn), lambda i, j: (0, j))],
        out_specs=pl.BlockSpec((tm, tn), lambda i, j: (i, j)),
        compiler_params=pltpu.CompilerParams(dimension_semantics=("parallel", "arbitrary"),
                                             vmem_limit_bytes=VMEM_LIMIT),
        name=name,
    )(a, b)


def _rope_tables(pos):
    half = HEAD_DIM // 2
    inv = ROPE_THETA ** (-jnp.arange(half, dtype=F32) / half)
    ang = pos.astype(F32)[:, None] * inv[None, :]
    cos, sin = jnp.cos(ang), jnp.sin(ang)
    return jnp.concatenate([cos, cos], -1), jnp.concatenate([-sin, sin], -1)


def _rope(x, cos, sin_signed):
    return x * cos + pltpu.roll(x, HEAD_DIM // 2, 1) * sin_signed


def _nsa_prep_kernel(kvc_ref, ks_ref, vs_ref, kw_ref, vw_ref, cos_ref, sin_ref,
                     kvc_o, ks_o, vs_o, kw_o, vw_o):
    cos, sin = cos_ref[...], sin_ref[...]
    kvc_o[0] = kvc_ref[0]
    for src, dst in ((ks_ref, ks_o), (kw_ref, kw_o)):
        for g in range(NSA_KV_GROUPS):
            sl = slice(g * HEAD_DIM, (g + 1) * HEAD_DIM)
            dst[0, :, sl] = _rope(src[0, :, sl], cos, sin).astype(BF16)
    vs_o[0] = vs_ref[0].astype(BF16)
    vw_o[0] = vw_ref[0].astype(BF16)


def _nsa_prep(y3, cos, sin, *, ts=512):
    b, s, _ = y3.shape
    kvw = NSA_KV_W

    def col(name, width):
        blk = _SEGS[name][0] // width
        return pl.BlockSpec((1, ts, width), lambda i, j: (i, j, blk))

    tab = pl.BlockSpec((ts, HEAD_DIM), lambda i, j: (j, 0))
    out_kv = pl.BlockSpec((1, ts, kvw), lambda i, j: (i, j, 0))
    return pl.pallas_call(
        _nsa_prep_kernel,
        out_shape=(jax.ShapeDtypeStruct((b, s, 2 * kvw), F32),) + (jax.ShapeDtypeStruct((b, s, kvw), BF16),) * 4,
        grid=(b, s // ts),
        in_specs=[col("kv_cmp", 2 * kvw), col("k_slc", kvw), col("v_slc", kvw), col("k_win", kvw),
                  col("v_win", kvw), tab, tab],
        out_specs=(pl.BlockSpec((1, ts, 2 * kvw), lambda i, j: (i, j, 0)), out_kv, out_kv, out_kv, out_kv),
        compiler_params=pltpu.CompilerParams(dimension_semantics=("parallel", "parallel")),
        name="nsa_prep",
    )(y3, y3, y3, y3, y3, cos, sin)


def _cmp_kernel(u_ref, posa_ref, posb_ref, w1a_ref, w1b_ref, w2_ref, cos_ref, sin_ref, kc_o, vc_o):
    n_rows = u_ref.shape[2]
    for idx, out in ((0, kc_o), (1, vc_o)):
        u = u_ref[0, idx]
        a = jnp.dot((u + posa_ref[idx]).astype(BF16), w1a_ref[idx], preferred_element_type=F32)
        bm = jnp.dot((u + posb_ref[idx]).astype(BF16), w1b_ref[idx], preferred_element_type=F32)
        pre = a + pltpu.roll(bm, n_rows - 1, 0)
        h = jax.nn.gelu(pre).astype(BF16)
        c = jnp.dot(h, w2_ref[idx], preferred_element_type=F32)
        if idx == 0:
            for g in range(NSA_KV_GROUPS):
                sl = slice(g * HEAD_DIM, (g + 1) * HEAD_DIM)
                out[0, :, sl] = _rope(c[:, sl], cos_ref[...], sin_ref[...]).astype(BF16)
        else:
            out[0] = c.astype(BF16)


def _block_diag_groups(w):
    z = jnp.zeros_like(w)
    rows = [jnp.concatenate([w if gi == go else z for go in range(NSA_KV_GROUPS)], -1)
            for gi in range(NSA_KV_GROUPS)]
    return jnp.concatenate(rows, -2)


def _compress(kvc, cmp_pos_k, cmp_k_w1, cmp_k_w2, cmp_pos_v, cmp_v_w1, cmp_v_w2, cos_e, sin_e):
    b, s, _ = kvc.shape
    n_rows = s // CMP_STRIDE
    g, hd = NSA_KV_GROUPS, HEAD_DIM
    row_w = CMP_STRIDE * g * hd
    u = kvc.reshape(b, n_rows, CMP_STRIDE, 2, g * hd).transpose(0, 3, 1, 2, 4).reshape(b, 2, n_rows, row_w)

    def pos_rows(pos):
        p = jnp.broadcast_to(pos.reshape(2, CMP_STRIDE, 1, hd), (2, CMP_STRIDE, g, hd))
        return p.reshape(2, 1, row_w)

    def w1_halves(w1):
        w = w1.reshape(2, CMP_STRIDE, hd, hd)
        return _block_diag_groups(w).reshape(2, row_w, g * hd).astype(BF16)

    pk, pv = pos_rows(cmp_pos_k), pos_rows(cmp_pos_v)
    wk, wv = w1_halves(cmp_k_w1), w1_halves(cmp_v_w1)
    posa = jnp.stack([pk[0], pv[0]])
    posb = jnp.stack([pk[1], pv[1]])
    w1a = jnp.stack([wk[0], wv[0]])
    w1b = jnp.stack([wk[1], wv[1]])
    w2 = jnp.stack([_block_diag_groups(cmp_k_w2), _block_diag_groups(cmp_v_w2)]).astype(BF16)
    full = lambda shp: pl.BlockSpec(shp, lambda i: (0,) * len(shp))
    out_spec = pl.BlockSpec((1, n_rows, g * hd), lambda i: (i, 0, 0))
    return pl.pallas_call(
        _cmp_kernel,
        out_shape=(jax.ShapeDtypeStruct((b, n_rows, g * hd), BF16),) * 2,
        grid=(b,),
        in_specs=[pl.BlockSpec((1, 2, n_rows, row_w), lambda i: (i, 0, 0, 0)),
                  full(posa.shape), full(posb.shape), full(w1a.shape), full(w1b.shape), full(w2.shape),
                  full(cos_e.shape), full(sin_e.shape)],
        out_specs=(out_spec, out_spec),
        compiler_params=pltpu.CompilerParams(dimension_semantics=("parallel",), vmem_limit_bytes=VMEM_LIMIT),
        name="nsa_compress",
    )(u, posa, posb, w1a, w1b, w2, cos_e, sin_e)


NSA_TQ = 256
NSA_TK = 256


def _flash(q4, k_ref, v_ref, kt_lo, kt_hi, bias_fn, acc_ref):
    rows = q4.shape[0]
    tq = rows // NSA_REP
    scale = HEAD_DIM ** -0.5
    acc_ref[...] = jnp.zeros_like(acc_ref)

    def body(kt, carry):
        m, l = carry
        k0 = pl.multiple_of(kt * NSA_TK, NSA_TK)
        k = k_ref[0, pl.ds(k0, NSA_TK), :]
        v = v_ref[0, pl.ds(k0, NSA_TK), :]
        s = lax.dot_general(q4, k, (((1,), (1,)), ((), ())), preferred_element_type=F32) * scale
        s = s.reshape(NSA_REP, tq, NSA_TK) + bias_fn(k0)[None]
        m_new = jnp.maximum(m, s.max(-1, keepdims=True))
        alpha = jnp.exp(m - m_new)
        p = jnp.exp(s - m_new)
        l = alpha * l + p.sum(-1, keepdims=True)
        pv = jnp.dot(p.reshape(rows, NSA_TK).astype(BF16), v, preferred_element_type=F32)
        acc_ref[...] = acc_ref[...] * alpha.reshape(rows, 1) + pv
        return m_new, l

    init = (jnp.full((NSA_REP, tq, 1), NEG_INF, F32), jnp.zeros((NSA_REP, tq, 1), F32))
    _, l = lax.fori_loop(kt_lo, kt_hi, body, init)
    return acc_ref[...] / l.reshape(rows, 1)


def _nsa_kernel(q_ref, gate_ref, kc_ref, vc_ref, ks_ref, vs_ref, kw_ref, vw_ref, cos_ref, sin_ref,
                cover_ref, o_ref, acc_ref, *, n_slc):
    qi = pl.program_id(2)
    tq = NSA_TQ
    rows = NSA_REP * tq
    q0 = qi * tq
    cos, sin = cos_ref[...], sin_ref[...]
    q4 = jnp.concatenate(
        [_rope(q_ref[0, :, r * HEAD_DIM:(r + 1) * HEAD_DIM], cos, sin).astype(BF16) for r in range(NSA_REP)],
        axis=0)
    tpos = q0 + lax.broadcasted_iota(jnp.int32, (tq, LANES), 0)
    lane = lax.broadcasted_iota(jnp.int32, (tq, LANES), 1)

    s = lax.dot_general(q4, kc_ref[0], (((1,), (1,)), ((), ())), preferred_element_type=F32)
    s = (s * HEAD_DIM ** -0.5).reshape(NSA_REP, tq, LANES)
    cmask = (lane * CMP_STRIDE + (CMP_LEN - 1) <= tpos)[None]
    s = jnp.where(cmask, s, NEG_INF)
    e = jnp.exp(s - s.max(-1, keepdims=True))
    p = e / e.sum(-1, keepdims=True) * cmask.astype(F32)
    o_cmp = jnp.dot(p.reshape(rows, LANES).astype(BF16), vc_ref[0], preferred_element_type=F32)

    p_sum_t = jnp.transpose(p.sum(0))
    p_hi = p_sum_t.astype(BF16)
    p_lo = (p_sum_t - p_hi.astype(F32)).astype(BF16)
    cover_t = cover_ref[...]
    p_slc = (jnp.dot(cover_t, p_hi, preferred_element_type=F32)
             + jnp.dot(cover_t, p_lo, preferred_element_type=F32))
    blk = lax.broadcasted_iota(jnp.int32, (n_slc, tq), 0)
    t_blk = (q0 + lax.broadcasted_iota(jnp.int32, (n_slc, tq), 1)) // SLC_LEN
    forced = (blk == 0) | (blk == t_blk) | (blk == t_blk - 1)
    score = jnp.where(forced, jnp.inf, jnp.where(blk > t_blk, -jnp.inf, p_slc))
    cnt = jnp.zeros((n_slc, tq), jnp.int32)
    for i in range(n_slc):
        other = score[i:i + 1, :]
        ahead = (other > score) | ((other == score) & (blk > i))
        cnt = cnt + ahead.astype(jnp.int32)
    chosen = (cnt < min(SLC_TOPK, n_slc)) & (score > -jnp.inf)
    pow2 = jnp.left_shift(1, blk & 15).astype(F32)
    halves = []
    for lo_blk in (0, 16):
        half = jnp.where(chosen & (blk >= lo_blk) & (blk < lo_blk + 16), pow2, 0.0).sum(0, keepdims=True)
        halves.append(jnp.transpose(jnp.broadcast_to(half, (LANES, tq)))[:, 0:1].astype(jnp.int32))
    bits = halves[0] | (halves[1] << 16)

    tpos_k = q0 + lax.broadcasted_iota(jnp.int32, (tq, NSA_TK), 0)
    kiota = lax.broadcasted_iota(jnp.int32, (tq, NSA_TK), 1)
    bits_b = jnp.broadcast_to(bits, (tq, NSA_TK))

    def slc_bias(k0):
        kpos = k0 + kiota
        picked = (lax.shift_right_logical(bits_b, kpos // SLC_LEN) & 1) == 1
        return jnp.where(picked & (kpos <= tpos_k), 0.0, NEG_INF)

    o_slc = _flash(q4, ks_ref, vs_ref, 0, (q0 + tq) // NSA_TK, slc_bias, acc_ref)

    def win_bias(k0):
        kpos = k0 + kiota
        return jnp.where((kpos <= tpos_k) & (kpos > tpos_k - WINDOW), 0.0, NEG_INF)

    kt_lo = jnp.maximum(q0 - WINDOW, 0) // NSA_TK
    o_win = _flash(q4, kw_ref, vw_ref, kt_lo, (q0 + tq) // NSA_TK, win_bias, acc_ref)

    gate = jax.nn.sigmoid(gate_ref[0])
    for r in range(NSA_REP):
        rs = slice(r * tq, (r + 1) * tq)
        o = (gate[:, 3 * r:3 * r + 1] * o_cmp[rs] + gate[:, 3 * r + 1:3 * r + 2] * o_slc[rs]
             + gate[:, 3 * r + 2:3 * r + 3] * o_win[rs])
        o_ref[0, :, r * HEAD_DIM:(r + 1) * HEAD_DIM] = o.astype(o_ref.dtype)


def _cover_matrix(n_rows, n_slc):
    i = np.arange(n_rows)[None, :]
    j = np.arange(n_slc)[:, None]
    c0 = i * CMP_STRIDE
    c1 = c0 + CMP_LEN - 1
    s0 = j * SLC_LEN
    s1 = s0 + SLC_LEN - 1
    ok = (c1 >= s0) & (c0 <= s1) & (i < n_rows - 1)
    return jnp.asarray(ok.astype(np.float32), dtype=BF16)


def _nsa_attention(y3, kc, vc, ks, vs, kw, vw, cos, sin):
    b, s, _ = y3.shape
    g, hd, tq = NSA_KV_GROUPS, HEAD_DIM, NSA_TQ
    n_rows = s // CMP_STRIDE
    assert n_rows == LANES and s // SLC_LEN <= 32 and s % tq == 0
    cover = _cover_matrix(n_rows, s // SLC_LEN)
    qw = NSA_REP * hd
    gate_blk = _SEGS["g_nsa"][0] // LANES
    q_blk = _SEGS["q_nsa"][0] // qw
    seq_g = pl.BlockSpec((1, s, hd), lambda i, j, t: (i, 0, j))
    cmp_g = pl.BlockSpec((1, n_rows, hd), lambda i, j, t: (i, 0, j))
    tab = pl.BlockSpec((tq, hd), lambda i, j, t: (t, 0))
    return pl.pallas_call(
        functools.partial(_nsa_kernel, n_slc=s // SLC_LEN),
        out_shape=jax.ShapeDtypeStruct((b, s, NSA_Q_W), BF16),
        grid=(b, g, s // tq),
        in_specs=[pl.BlockSpec((1, tq, qw), lambda i, j, t: (i, t, q_blk + j)),
                  pl.BlockSpec((1, tq, LANES), lambda i, j, t: (i, t, gate_blk + j)),
                  cmp_g, cmp_g, seq_g, seq_g, seq_g, seq_g, tab, tab,
                  pl.BlockSpec(cover.shape, lambda i, j, t: (0, 0))],
        out_specs=pl.BlockSpec((1, tq, qw), lambda i, j, t: (i, t, j)),
        scratch_shapes=[pltpu.VMEM((NSA_REP * tq, hd), F32)],
        compiler_params=pltpu.CompilerParams(dimension_semantics=("parallel", "parallel", "arbitrary"),
                                             vmem_limit_bytes=VMEM_LIMIT),
        name="nsa_attention",
    )(y3, y3, kc, vc, ks, vs, kw, vw, cos, sin, cover)


FFN_TM = 256
FFN_NCH = 8
_STAGE, _COMPUTE, _ZERO = 0, 1, 2


def _ffn_kernel(kind_ref, e_ref, c_ref, blk_ref, x_ref, w_ref, wg_ref, wu_ref, wd_ref, o_ref,
                wg_s, wu_s, wd_s):
    s = pl.program_id(0)
    kind = kind_ref[s]
    rg = wg_ref.shape[1]
    rd = wd_ref.shape[1]

    @pl.when(kind == _STAGE)
    def _():
        c = c_ref[s]
        r0 = pl.multiple_of(c * rg, rg)
        wg_s[pl.ds(r0, rg), :] = wg_ref[0].astype(BF16)
        wu_s[pl.ds(r0, rg), :] = wu_ref[0].astype(BF16)
        r1 = pl.multiple_of(c * rd, rd)
        wd_s[pl.ds(r1, rd), :] = wd_ref[0].astype(BF16)

    @pl.when(kind == _COMPUTE)
    def _():
        x = x_ref[...]
        gate = jnp.dot(x, wg_s[...], preferred_element_type=F32)
        up = jnp.dot(x, wu_s[...], preferred_element_type=F32)
        h = (jax.nn.silu(gate) * up).astype(BF16)
        o_ref[...] = jnp.dot(h, wd_s[...], preferred_element_type=F32) * w_ref[...]

    @pl.when(kind == _ZERO)
    def _():
        o_ref[...] = jnp.zeros_like(o_ref)


def _ffn_items(nb_e, n_blocks):
    e = nb_e.shape[0]
    blk_start = jnp.cumsum(nb_e) - nb_e
    n_used = blk_start[-1] + nb_e[-1]
    base = FFN_NCH * jnp.arange(e) + blk_start
    s = jnp.arange(FFN_NCH * e + n_blocks)
    e_s = (s[:, None] >= base[None, :]).sum(1) - 1
    r = s - base[e_s]
    end = FFN_NCH * e + n_used
    kind = jnp.where(s >= end, _ZERO, jnp.where(r < FFN_NCH, _STAGE, _COMPUTE))
    c_s = jnp.minimum(r, FFN_NCH - 1)
    blk = jnp.where(kind == _ZERO, n_used + s - end,
                    jnp.where(kind == _STAGE, blk_start[e_s], blk_start[e_s] + r - FFN_NCH))
    blk = jnp.minimum(blk, n_blocks - 1)
    i32 = lambda a: a.astype(jnp.int32)
    return i32(kind), i32(e_s), i32(c_s), i32(blk)


def _grouped_ffn(xs, row_w, nb_e, wg, wu, wd):
    p, d = xs.shape
    e, _, f = wg.shape
    n_blocks = p // FFN_TM
    assert d % FFN_NCH == 0 and f % (FFN_NCH * 16) == 0
    rg, rd = d // FFN_NCH, f // FFN_NCH
    kind, e_s, c_s, blk = _ffn_items(nb_e, n_blocks)
    row = lambda w: pl.BlockSpec((FFN_TM, w), lambda s, k, es, cs, bs: (bs[s], 0))
    chunk = lambda r, w: pl.BlockSpec((1, r, w), lambda s, k, es, cs, bs: (es[s], cs[s], 0))
    grid_spec = pltpu.PrefetchScalarGridSpec(
        num_scalar_prefetch=4,
        grid=(FFN_NCH * e + n_blocks,),
        in_specs=[row(d), row(1), chunk(rg, f), chunk(rg, f), chunk(rd, d)],
        out_specs=row(d),
        scratch_shapes=[pltpu.VMEM((d, f), BF16), pltpu.VMEM((d, f), BF16), pltpu.VMEM((f, d), BF16)],
    )
    return pl.pallas_call(
        _ffn_kernel,
        out_shape=jax.ShapeDtypeStruct((p, d), F32),
        grid_spec=grid_spec,
        compiler_params=pltpu.CompilerParams(dimension_semantics=("arbitrary",), vmem_limit_bytes=VMEM_LIMIT),
        name="grouped_ffn",
    )(kind, e_s, c_s, blk, xs, row_w, wg, wu, wd)


FFN_IDX_SLOTS = 4
FFN_ROW_PITCH = 24


def _routed_ffn_kernel(kind_ref, e_ref, c_ref, blk_ref, nblk_ref, x_hbm, tab_hbm, w_ref, wg_ref, wu_ref,
                       wd_ref, y_hbm, wg_s, wu_s, wd_s, xbuf0, xbuf1, obuf0, obuf1, zbuf, idx_s,
                       isem, gsem, ssem, zsem):
    s = pl.program_id(0)
    kind = kind_ref[s]
    rg = wg_ref.shape[1]
    rd = wd_ref.shape[1]
    tm = FFN_TM
    last_blk = tab_hbm.shape[0] - 1
    ns = wd_ref.shape[2] // LANES
    xbufs, obufs = (xbuf0, xbuf1), (obuf0, obuf1)

    def idx_copy(b):
        return pltpu.make_async_copy(tab_hbm.at[jnp.minimum(b, last_blk)], idx_s.at[b % FFN_IDX_SLOTS],
                                     isem.at[b % FFN_IDX_SLOTS])

    def gather_copy(b, par, r):
        tok = idx_s[b % FFN_IDX_SLOTS, r]
        return pltpu.make_async_copy(x_hbm.at[pl.ds(pl.multiple_of(tok * ns, ns), ns), :],
                                     xbufs[par].at[pl.ds(r * FFN_ROW_PITCH, ns), :], gsem.at[par])

    def scatter_copy(b, par, r):
        dst = idx_s[b % FFN_IDX_SLOTS, tm + r]
        return pltpu.make_async_copy(obufs[par].at[pl.ds(r * FFN_ROW_PITCH, ns), :],
                                     y_hbm.at[pl.ds(pl.multiple_of(dst * ns, ns), ns), :], ssem.at[par])

    @pl.when(s == 0)
    def _():
        zbuf[...] = jnp.zeros_like(zbuf)
        idx_copy(0).start()
        idx_copy(1).start()
        idx_copy(0).wait()
        for r in range(tm):
            gather_copy(0, 0, r).start()

    @pl.when(kind == _STAGE)
    def _():
        c = c_ref[s]
        r0 = pl.multiple_of(c * rg, rg)
        wg_s[pl.ds(r0, rg), :] = wg_ref[0].astype(BF16)
        wu_s[pl.ds(r0, rg), :] = wu_ref[0].astype(BF16)
        r1 = pl.multiple_of(c * rd, rd)
        wd_s[pl.ds(r1, rd), :] = wd_ref[0].astype(BF16)

    def compute(b, par, first):
        idx_copy(b + 1).wait()
        for r in range(tm):
            gather_copy(b, par, r).wait()
        if not first:
            @pl.when(b >= 2)
            def _():
                for r in range(tm):
                    scatter_copy(b - 2, par, r).wait()

            for r in range(tm):
                scatter_copy(b - 1, 1 - par, r).start(priority=r % 2)
        for r in range(tm):
            gather_copy(b + 1, 1 - par, r).start(priority=r % 2)
        idx_copy(b + 2).start()
        x = jnp.concatenate([xbufs[par][pl.ds(j, tm, stride=FFN_ROW_PITCH), :] for j in range(ns)],
                            axis=1).astype(BF16)
        gate = jnp.dot(x, wg_s[...], preferred_element_type=F32)

        @pl.when(b >= 0)
        def _():
            up = jnp.dot(x, wu_s[...], preferred_element_type=F32)
            h = (jax.nn.silu(gate) * up).astype(BF16)
            out = jnp.dot(h, wd_s[...], preferred_element_type=F32) * w_ref[...]
            for j in range(ns):
                obufs[par][pl.ds(j, tm, stride=FFN_ROW_PITCH), :] = out[:, j * LANES:(j + 1) * LANES]

    is_compute = kind == _COMPUTE
    blk = blk_ref[s]
    pl.when(is_compute & (blk == 0))(lambda: compute(blk, 0, True))
    pl.when(is_compute & (blk > 0) & (blk % 2 == 0))(lambda: compute(blk, 0, False))
    pl.when(is_compute & (blk % 2 == 1))(lambda: compute(blk, 1, False))

    @pl.when(kind == _ZERO)
    def _():
        row0 = pl.multiple_of(blk * (tm * ns), tm * ns)
        tail = pltpu.make_async_copy(zbuf, y_hbm.at[pl.ds(row0, tm * ns), :], zsem.at[0])
        tail.start()
        tail.wait()

    def drain(n_used, par):
        idx_copy(n_used + 1).wait()
        for r in range(tm):
            gather_copy(n_used, par, r).wait()
        for r in range(tm):
            scatter_copy(n_used - 1, 1 - par, r).start()
        for r in range(tm):
            scatter_copy(n_used - 2, par, r).wait()
        for r in range(tm):
            scatter_copy(n_used - 1, 1 - par, r).wait()

    is_last = s == pl.num_programs(0) - 1
    n_used = nblk_ref[0]
    pl.when(is_last & (n_used % 2 == 0))(lambda: drain(n_used, 0))
    pl.when(is_last & (n_used % 2 == 1))(lambda: drain(n_used, 1))


def _routed_ffn(x, tab, row_w, nb_e, wg, wu, wd):
    e, d, f = wg.shape
    n_blocks = tab.shape[0]
    ns = d // LANES
    assert ns % 8 == 0 and ns <= FFN_ROW_PITCH and x.shape[1] == LANES
    row_buf = pltpu.VMEM((FFN_TM * FFN_ROW_PITCH, LANES), F32)
    zero_buf = pltpu.VMEM((FFN_TM * ns, LANES), F32)
    assert d % FFN_NCH == 0 and f % (FFN_NCH * 16) == 0 and tab.shape[1] == 2 * FFN_TM
    rg, rd = d // FFN_NCH, f // FFN_NCH
    kind, e_s, c_s, blk = _ffn_items(nb_e, n_blocks)
    n_used = jnp.sum(nb_e).astype(jnp.int32).reshape(1)
    chunk = lambda r, w: pl.BlockSpec((1, r, w), lambda s, k, es, cs, bs, nu: (es[s], cs[s], 0))
    grid_spec = pltpu.PrefetchScalarGridSpec(
        num_scalar_prefetch=5,
        grid=(FFN_NCH * e + n_blocks,),
        in_specs=[pl.BlockSpec(memory_space=pl.ANY), pl.BlockSpec(memory_space=pl.ANY),
                  pl.BlockSpec((FFN_TM, 1), lambda s, k, es, cs, bs, nu: (bs[s], 0)),
                  chunk(rg, f), chunk(rg, f), chunk(rd, d)],
        out_specs=pl.BlockSpec(memory_space=pl.ANY),
        scratch_shapes=[pltpu.VMEM((d, f), BF16), pltpu.VMEM((d, f), BF16), pltpu.VMEM((f, d), BF16),
                        row_buf, row_buf, row_buf, row_buf, zero_buf,
                        pltpu.SMEM((FFN_IDX_SLOTS, 2 * FFN_TM), jnp.int32),
                        pltpu.SemaphoreType.DMA((FFN_IDX_SLOTS,)), pltpu.SemaphoreType.DMA((2,)),
                        pltpu.SemaphoreType.DMA((2,)), pltpu.SemaphoreType.DMA((1,))],
    )
    return pl.pallas_call(
        _routed_ffn_kernel,
        out_shape=jax.ShapeDtypeStruct((n_blocks * FFN_TM * ns, LANES), F32),
        grid_spec=grid_spec,
        compiler_params=pltpu.CompilerParams(dimension_semantics=("arbitrary",), vmem_limit_bytes=VMEM_LIMIT),
        name="routed_ffn",
    )(kind, e_s, c_s, blk, n_used, x, tab, row_w, wg, wu, wd)


ML_L = 256
ML_HALO = 8
MLSTM_CONV = 4


def _mlstm_kernel(qk_ref, v_ref, o_ref, if_ref, convw_ref, convb_ref, ifb_ref, ng_ref, tril_ref,
                  out_ref, c_ref, m_ref, halo_ref):
    L, NH, DK, DV = ML_L, MLSTM_HEADS, MLSTM_QK_DIM, MLSTM_V_DIM

    @pl.when(pl.program_id(1) == 0)
    def _():
        c_ref[...] = jnp.zeros_like(c_ref)
        m_ref[...] = jnp.zeros_like(m_ref)
        halo_ref[...] = jnp.zeros_like(halo_ref)

    x = qk_ref[0]
    cat = jnp.concatenate([halo_ref[...], x], axis=0)
    halo_ref[...] = x[L - ML_HALO:, :]
    y = convb_ref[...]
    for j in range(MLSTM_CONV):
        lo = ML_HALO - (MLSTM_CONV - 1) + j
        y = y + convw_ref[j:j + 1, :] * cat[lo:lo + L, :]
    qk = y * jax.nn.sigmoid(y)

    lane = lax.broadcasted_iota(jnp.int32, (L, LANES), 1)
    z = if_ref[0] + ifb_ref[...]
    lg = jnp.where(lane < NH, z, jnp.minimum(z, 0.0) - jnp.log1p(jnp.exp(-jnp.abs(z))))
    hi = lg.astype(BF16)
    r1 = lg - hi.astype(F32)
    mid = r1.astype(BF16)
    lo3 = (r1 - mid.astype(F32)).astype(BF16)
    tril = tril_ref[...]
    bcum = (jnp.dot(tril, hi, preferred_element_type=F32) + jnp.dot(tril, mid, preferred_element_type=F32)
            + jnp.dot(tril, lo3, preferred_element_type=F32))

    row = lax.broadcasted_iota(jnp.int32, (L, L), 0)
    colm = lax.broadcasted_iota(jnp.int32, (L, L), 1)
    tri = row >= colm
    ones_col = (lane == 0).astype(BF16)
    for h in range(NH):
        li = lg[:, h:h + 1]
        b = bcum[:, NH + h:NH + h + 1]
        m_prev = m_ref[h:h + 1, 0:1]
        a_col = li - b
        a_row = jnp.transpose(jnp.broadcast_to(a_col, (L, LANES)))[0:1, :]
        dmat = jnp.where(tri, b + a_row, -jnp.inf)
        m_inter = b + m_prev
        m_t = jnp.maximum(m_inter, dmat.max(-1, keepdims=True))
        q = qk[:, h * DK:(h + 1) * DK].astype(BF16)
        k = qk[:, NH * DK + h * DK:NH * DK + (h + 1) * DK] * DK ** -0.5
        s = lax.dot_general(q, k.astype(BF16), (((1,), (1,)), ((), ())), preferred_element_type=F32)
        wqk = (jnp.exp(dmat - m_t) * s).astype(BF16)
        inter = jnp.exp(m_inter - m_t)
        v_ext = jnp.concatenate([v_ref[0, :, h * DV:(h + 1) * DV].astype(BF16), ones_col], axis=1)
        c_old = c_ref[h]
        nd = (jnp.dot(wqk, v_ext, preferred_element_type=F32)
              + inter * jnp.dot(q, c_old.astype(BF16), preferred_element_type=F32))
        num = nd[:, :DV]
        den = nd[:, DV:DV + 1]
        hh = num / jnp.maximum(jnp.abs(den), jnp.exp(-m_t))
        b_last = b[L - 1:L, :]
        g = b_last - b + li
        m_new = jnp.maximum(b_last + m_prev, g.max(0, keepdims=True))
        decay = jnp.exp(b_last + m_prev - m_new)
        w_s = jnp.exp(g - m_new)
        kw_t = jnp.transpose(k * w_s).astype(BF16)
        c_ref[h] = decay * c_old + jnp.dot(kw_t, v_ext, preferred_element_type=F32)
        m_ref[h:h + 1, :] = jnp.broadcast_to(m_new, (1, LANES))
        mu = hh.mean(-1, keepdims=True)
        var = jnp.mean(jnp.square(hh - mu), -1, keepdims=True)
        hn = (hh - mu) * lax.rsqrt(var + LN_EPS) * ng_ref[:, h * DV:(h + 1) * DV]
        out_ref[0, :, h * DV:(h + 1) * DV] = (
            jax.nn.sigmoid(o_ref[0, :, h * DV:(h + 1) * DV]) * hn).astype(out_ref.dtype)


def _mlstm(y3, conv_w, conv_b, if_bias, norm_g):
    b, s, _ = y3.shape
    L, NH = ML_L, MLSTM_HEADS
    assert s % L == 0

    def col(name, width):
        blk = _SEGS[name][0] // width
        return pl.BlockSpec((1, L, width), lambda i, j: (i, j, blk))

    full = lambda shp: pl.BlockSpec(shp, lambda i, j: (0,) * len(shp))
    ifb = jnp.pad(if_bias.reshape(1, 2 * NH), ((0, 0), (0, LANES - 2 * NH)))
    tril = jnp.asarray(np.tril(np.ones((L, L), np.float32)), dtype=BF16)
    conv_b = conv_b.reshape(1, -1)
    norm_g = norm_g.reshape(1, -1)
    return pl.pallas_call(
        _mlstm_kernel,
        out_shape=jax.ShapeDtypeStruct((b, s, ML_V_W), BF16),
        grid=(b, s // L),
        in_specs=[col("qk_ml", 2 * ML_QK_W), col("v_ml", ML_V_W), col("o_ml", ML_V_W), col("if_ml", LANES),
                  full(conv_w.shape), full(conv_b.shape), full(ifb.shape), full(norm_g.shape), full(tril.shape)],
        out_specs=pl.BlockSpec((1, L, ML_V_W), lambda i, j: (i, j, 0)),
        scratch_shapes=[pltpu.VMEM((NH, MLSTM_QK_DIM, MLSTM_V_DIM + LANES), F32),
                        pltpu.VMEM((8, LANES), F32),
                        pltpu.VMEM((ML_HALO, 2 * ML_QK_W), F32)],
        compiler_params=pltpu.CompilerParams(dimension_semantics=("parallel", "arbitrary"),
                                             vmem_limit_bytes=VMEM_LIMIT),
        name="mlstm",
    )(y3, y3, y3, y3, conv_w, conv_b, ifb, norm_g, tril)


MEM_TQ = 512


def _mem_attn_kernel(q_ref, kv_ref, o_ref):
    hd = MEM_HEAD_DIM
    for h in range(MEM_HEADS):
        q = q_ref[0, :, h * hd:(h + 1) * hd].astype(BF16)
        k = kv_ref[0, :, h * hd:(h + 1) * hd]
        v = kv_ref[0, :, MEM_W + h * hd:MEM_W + (h + 1) * hd]
        s = lax.dot_general(q, k, (((1,), (1,)), ((), ())), preferred_element_type=F32) * hd ** -0.5
        e = jnp.exp(s - s.max(-1, keepdims=True))
        p = (e / e.sum(-1, keepdims=True)).astype(BF16)
        o_ref[0, :, h * hd:(h + 1) * hd] = jnp.dot(p, v, preferred_element_type=F32).astype(o_ref.dtype)


def _mem_attention(y3, kv_mem):
    b, s, _ = y3.shape
    m = kv_mem.shape[1]
    q_blk = _SEGS["q_mem"][0] // MEM_W
    return pl.pallas_call(
        _mem_attn_kernel,
        out_shape=jax.ShapeDtypeStruct((b, s, MEM_W), BF16),
        grid=(b, s // MEM_TQ),
        in_specs=[pl.BlockSpec((1, MEM_TQ, MEM_W), lambda i, j: (i, j, q_blk)),
                  pl.BlockSpec((1, m, 2 * MEM_W), lambda i, j: (i, 0, 0))],
        out_specs=pl.BlockSpec((1, MEM_TQ, MEM_W), lambda i, j: (i, j, 0)),
        compiler_params=pltpu.CompilerParams(dimension_semantics=("parallel", "parallel"),
                                             vmem_limit_bytes=VMEM_LIMIT),
        name="mem_attention",
    )(y3, kv_mem)


MERGE_TM = 256


def _merge_kernel(yn_ref, ym_ref, yc_ref, g0_ref, g1_ref, g2_ref, x_ref, wb_ref, wo_ref, lg_ref, lb_ref,
                  x1_ref, x1b_ref, x1s_ref):
    merged = None
    for i, (y_ref, g_ref) in enumerate(((yn_ref, g0_ref), (ym_ref, g1_ref), (yc_ref, g2_ref))):
        t = jax.nn.sigmoid(g_ref[...]) * jnp.dot(y_ref[...], wb_ref[i], preferred_element_type=F32)
        merged = t if merged is None else merged + t
    z = DN_ALPHA * x_ref[...] + jnp.dot(merged.astype(BF16), wo_ref[...], preferred_element_type=F32)
    mu = z.mean(-1, keepdims=True)
    var = jnp.mean(jnp.square(z - mu), -1, keepdims=True)
    x1 = (z - mu) * lax.rsqrt(var + LN_EPS) * lg_ref[...] + lb_ref[...]
    x1_ref[...] = x1
    x1b_ref[...] = x1.astype(BF16)
    tm, d = x1.shape
    ns = d // LANES
    for j in range(ns):
        x1s_ref[pl.ds(j, tm, stride=ns), :] = x1[:, j * LANES:(j + 1) * LANES]


def _merge_ln1(y_nsa, y_ml, y_mem, y2, xt, wb, wo, ln_g, ln_b):
    t, d = xt.shape
    tm = MERGE_TM
    bw = y_nsa.shape[1]
    resident = pl.Buffered(1)
    row = lambda w: pl.BlockSpec((tm, w), lambda i: (i, 0))
    gate = lambda k: pl.BlockSpec((tm, d), lambda i: (i, k))
    return pl.pallas_call(
        _merge_kernel,
        out_shape=(jax.ShapeDtypeStruct((t, d), F32), jax.ShapeDtypeStruct((t, d), BF16),
                   jax.ShapeDtypeStruct((t * (d // LANES), LANES), F32)),
        grid=(t // tm,),
        in_specs=[row(bw), row(bw), row(bw), gate(0), gate(1), gate(2), row(d),
                  pl.BlockSpec((N_BRANCH, bw, d), lambda i: (0, 0, 0), pipeline_mode=resident),
                  pl.BlockSpec((d, d), lambda i: (0, 0), pipeline_mode=resident),
                  pl.BlockSpec((1, d), lambda i: (0, 0)), pl.BlockSpec((1, d), lambda i: (0, 0))],
        out_specs=(row(d), row(d), pl.BlockSpec((tm * (d // LANES), LANES), lambda i: (i, 0))),
        compiler_params=pltpu.CompilerParams(dimension_semantics=("parallel",), vmem_limit_bytes=VMEM_LIMIT),
        name="merge_ln1",
    )(y_nsa, y_ml, y_mem, y2, y2, y2, xt, wb, wo, ln_g.reshape(1, d), ln_b.reshape(1, d))


ROUTER_TM = 256


def _router_kernel(x_ref, w_ref, bias_ref, idx_ref, wt_ref, cnt_ref, run_ref):
    tm = ROUTER_TM

    @pl.when(pl.program_id(0) == 0)
    def _():
        run_ref[...] = jnp.zeros_like(run_ref)

    x = x_ref[...]
    w = w_ref[...]
    x_hi = x.astype(BF16)
    x_lo = (x - x_hi.astype(F32)).astype(BF16)
    w_hi = w.astype(BF16)
    w_lo = (w - w_hi.astype(F32)).astype(BF16)
    logits = (jnp.dot(x_hi, w_hi, preferred_element_type=F32) + jnp.dot(x_lo, w_hi, preferred_element_type=F32)
              + jnp.dot(x_hi, w_lo, preferred_element_type=F32))
    lane = lax.broadcasted_iota(jnp.int32, (tm, LANES), 1)
    aff = jax.nn.sigmoid(logits)
    score = jnp.where(lane < N_EXPERTS, aff + bias_ref[...], -jnp.inf)
    idx_out = jnp.zeros((tm, LANES), jnp.int32)
    aff_out = jnp.zeros((tm, LANES), F32)
    chosen = jnp.zeros((tm, LANES), jnp.bool_)
    for k in range(TOP_K):
        best = score.max(-1, keepdims=True)
        cur = jnp.where(score == best, lane, LANES).min(-1, keepdims=True)
        here = lane == cur
        idx_out = jnp.where(lane == k, cur, idx_out)
        aff_out = jnp.where(lane == k, jnp.where(here, aff, 0.0).sum(-1, keepdims=True), aff_out)
        chosen = chosen | here
        score = jnp.where(here, -jnp.inf, score)
    run_new = run_ref[0:1, :] + chosen.astype(F32).sum(0, keepdims=True)
    run_ref[...] = jnp.broadcast_to(run_new, run_ref.shape)
    cnt_ref[...] = jnp.broadcast_to(run_new, cnt_ref.shape)
    idx_ref[...] = idx_out
    wt_ref[...] = aff_out / aff_out.sum(-1, keepdims=True) * ROUTED_SCALE


def _router(x1, router_w, router_bias):
    t, d = x1.shape
    tm = ROUTER_TM
    e = router_w.shape[1]
    w = jnp.pad(router_w, ((0, 0), (0, LANES - e)))
    bias = jnp.pad(router_bias.reshape(1, e), ((0, 0), (0, LANES - e)))
    out = pl.BlockSpec((tm, LANES), lambda i: (i, 0))
    const = lambda shp: pl.BlockSpec(shp, lambda i: (0, 0))
    idx, wt, cnt = pl.pallas_call(
        _router_kernel,
        out_shape=(jax.ShapeDtypeStruct((t, LANES), jnp.int32), jax.ShapeDtypeStruct((t, LANES), F32),
                   jax.ShapeDtypeStruct((8, LANES), F32)),
        grid=(t // tm,),
        in_specs=[pl.BlockSpec((tm, d), lambda i: (i, 0)), const(w.shape), const(bias.shape)],
        out_specs=(out, out, const((8, LANES))),
        scratch_shapes=[pltpu.VMEM((8, LANES), F32)],
        compiler_params=pltpu.CompilerParams(dimension_semantics=("arbitrary",), vmem_limit_bytes=VMEM_LIMIT),
        name="router",
    )(x1, w, bias)
    return idx[:, :TOP_K], wt[:, :TOP_K], cnt[0, :e].astype(jnp.int32)


def _moe_layout(idx, wts, counts):
    t = idx.shape[0]
    e = N_EXPERTS
    m = t * TOP_K
    assert m // FFN_TM >= 2
    nb_e = (counts + FFN_TM - 1) // FFN_TM
    blk_start = jnp.cumsum(nb_e) - nb_e
    start = jnp.cumsum(counts) - counts
    n_blocks = m // FFN_TM + e
    order = jnp.argsort(idx.reshape(m))
    bi = jnp.arange(n_blocks)
    blk_e = jnp.clip((bi[:, None] >= blk_start[None, :]).sum(1) - 1, 0, e - 1)
    j = ((bi - blk_start[blk_e]) * FFN_TM)[:, None] + jnp.arange(FFN_TM)[None, :]
    valid = j < counts[blk_e][:, None]
    flat = order[jnp.clip(start[blk_e][:, None] + j, 0, m - 1)]
    tok, slot = flat // TOP_K, flat % TOP_K
    pad_before = (blk_start * FFN_TM - start)[blk_e] - counts[blk_e]
    row_tok = jnp.where(valid, tok, 0)
    dest = jnp.where(valid, slot * t + tok, m + pad_before[:, None] + j)
    row_w = jnp.where(valid, wts.reshape(m)[flat], 0.0).reshape(n_blocks * FFN_TM, 1)
    tab = jnp.concatenate([row_tok, dest], axis=1).astype(jnp.int32)
    return tab, row_w, nb_e


def _combine_kernel(x_ref, *refs):
    slot_refs, (s_ref, g_ref, b_ref, o_ref) = refs[:TOP_K], refs[TOP_K:]
    tm, d = x_ref.shape
    ns = d // LANES

    def rows(r_ref):
        return jnp.concatenate([r_ref[pl.ds(j, tm, stride=ns), :] for j in range(ns)], axis=1)

    routed = rows(slot_refs[0])
    for r_ref in slot_refs[1:]:
        routed = routed + rows(r_ref)
    z = DN_ALPHA * x_ref[...] + (routed + s_ref[...])
    mu = z.mean(-1, keepdims=True)
    var = jnp.mean(jnp.square(z - mu), -1, keepdims=True)
    o_ref[...] = (z - mu) * lax.rsqrt(var + LN_EPS) * g_ref[...] + b_ref[...]


def _combine_ln2(x1, y_slots, shared, ln_g, ln_b, *, tm=256):
    t, d = x1.shape
    nt = t // tm
    ns = d // LANES
    row = pl.BlockSpec((tm, d), lambda i: (i, 0))
    vec = pl.BlockSpec((1, d), lambda i: (0, 0))
    slots = [pl.BlockSpec((tm * ns, LANES), functools.partial(lambda i, k: (k * nt + i, 0), k=k))
             for k in range(TOP_K)]
    return pl.pallas_call(
        _combine_kernel, out_shape=jax.ShapeDtypeStruct((t, d), F32), grid=(nt,),
        in_specs=[row] + slots + [row, vec, vec], out_specs=row,
        compiler_params=pltpu.CompilerParams(dimension_semantics=("parallel",), vmem_limit_bytes=VMEM_LIMIT),
        name="combine_ln2",
    )(x1, *([y_slots] * TOP_K), shared, ln_g.reshape(1, d), ln_b.reshape(1, d))


def _layer(x, mem, w_in, mlstm_conv_w, mlstm_conv_b, mlstm_if_bias, mlstm_norm_g,
           cmp_pos_k, cmp_k_w1, cmp_k_w2, cmp_pos_v, cmp_v_w1, cmp_v_w2,
           w_mem_kv, w_branch, w_o, ln1_g, ln1_b, router_w, router_bias,
           w_e_gate, w_e_up, w_e_down, w_s_gate, w_s_up, w_s_down, ln2_g, ln2_b):
    B, S, D = x.shape
    T = B * S
    xt = x.reshape(T, D)
    y2 = _matmul(xt.astype(BF16), _pad_in_proj(w_in).astype(BF16), tm=1024, tn=IN_TN, name="in_proj")
    y3 = y2.reshape(B, S, IN_PAD)

    cos, sin = _rope_tables(jnp.arange(S))
    cos_e, sin_e = _rope_tables(jnp.arange(S // CMP_STRIDE) * CMP_STRIDE + CMP_LEN - 1)
    kvc, ks, vs, kw, vw = _nsa_prep(y3, cos, sin)
    kc, vc = _compress(kvc, cmp_pos_k, cmp_k_w1, cmp_k_w2, cmp_pos_v, cmp_v_w1, cmp_v_w2, cos_e, sin_e)
    y_nsa = _nsa_attention(y3, kc, vc, ks, vs, kw, vw, cos, sin)

    y_ml = _mlstm(y3, mlstm_conv_w, mlstm_conv_b, mlstm_if_bias, mlstm_norm_g)

    m_len = mem.shape[1]
    kv_mem = _matmul(mem.reshape(B * m_len, D).astype(BF16), w_mem_kv.astype(BF16), tm=1024, tn=512,
                     out_dtype=BF16, name="mem_kv").reshape(B, m_len, 2 * MEM_W)
    y_mem = _mem_attention(y3, kv_mem)

    x1, x1b, x1s = _merge_ln1(y_nsa.reshape(T, -1), y_ml.reshape(T, -1), y_mem.reshape(T, -1), y2, xt,
                         w_branch.astype(BF16), w_o.astype(BF16), ln1_g, ln1_b)

    idx, wts, counts = _router(x1, router_w, router_bias)
    tab, row_w, nb_e = _moe_layout(idx, wts, counts)
    y_slots = _routed_ffn(x1s, tab, row_w, nb_e, w_e_gate, w_e_up, w_e_down)
    shared = _grouped_ffn(x1b, jnp.ones((T, 1), F32), jnp.full((1,), T // FFN_TM, jnp.int32),
                          w_s_gate[None], w_s_up[None], w_s_down[None])
    return _combine_ln2(x1, y_slots, shared, ln2_g, ln2_b).reshape(B, S, D)


def kernel(x, mem, w_in, mlstm_conv_w, mlstm_conv_b, mlstm_if_bias, mlstm_norm_g, cmp_pos_k, cmp_k_w1, cmp_k_w2, cmp_pos_v, cmp_v_w1, cmp_v_w2, w_mem_kv, w_branch, w_o, ln1_g, ln1_b, router_w, router_bias, w_e_gate, w_e_up, w_e_down, w_s_gate, w_s_up, w_s_down, ln2_g, ln2_b):
    for layer in range(DEPTH):
        x = _layer(x, mem, w_in[layer], mlstm_conv_w[layer], mlstm_conv_b[layer],
                   mlstm_if_bias[layer], mlstm_norm_g[layer],
                   cmp_pos_k[layer], cmp_k_w1[layer], cmp_k_w2[layer],
                   cmp_pos_v[layer], cmp_v_w1[layer], cmp_v_w2[layer],
                   w_mem_kv[layer], w_branch[layer], w_o[layer], ln1_g[layer], ln1_b[layer],
                   router_w[layer], router_bias[layer], w_e_gate[layer], w_e_up[layer],
                   w_e_down[layer], w_s_gate[layer], w_s_up[layer], w_s_down[layer],
                   ln2_g[layer], ln2_b[layer])
    return x
```

```python
import functools

import jax
import jax.numpy as jnp
import numpy as np
from jax import lax
from jax.experimental import pallas as pl
from jax.experimental.pallas import tpu as pltpu

F32 = jnp.float32
BF16 = jnp.bfloat16

D_MODEL = 2048
HEAD_DIM = 128
NSA_HEADS = 8
NSA_KV_GROUPS = 2
NSA_REP = NSA_HEADS // NSA_KV_GROUPS
CMP_LEN = 32
CMP_STRIDE = 16
SLC_LEN = 64
SLC_TOPK = 16
WINDOW = 512
MLSTM_HEADS = 4
MLSTM_QK_DIM = 128
MLSTM_V_DIM = 256
MLSTM_CHUNK = 64
MEM_HEADS = 4
MEM_HEAD_DIM = 256
N_EXPERTS = 64
TOP_K = 6
EXPERT_DIM = 1408
ROUTED_SCALE = 2.5
ROPE_THETA = 10000.0
LN_EPS = 1e-5
NEG_INF = -1e30
DEPTH = 1
DN_ALPHA = (2 * DEPTH) ** 0.25

NSA_Q_W = NSA_HEADS * HEAD_DIM
NSA_KV_W = NSA_KV_GROUPS * HEAD_DIM
ML_QK_W = MLSTM_HEADS * MLSTM_QK_DIM
ML_V_W = MLSTM_HEADS * MLSTM_V_DIM
MEM_W = MEM_HEADS * MEM_HEAD_DIM
N_BRANCH = 3

LANES = 128
VMEM_LIMIT = 56 * 1024 * 1024

_SEGS = {}
_off = 0
for _name, _w, _blk in (("g_br", N_BRANCH * D_MODEL, D_MODEL), ("q_nsa", NSA_Q_W, NSA_REP * HEAD_DIM),
                        ("q_mem", MEM_W, MEM_W), ("v_ml", ML_V_W, ML_V_W), ("o_ml", ML_V_W, ML_V_W),
                        ("qk_ml", 2 * ML_QK_W, 2 * ML_QK_W), ("kv_cmp", 2 * NSA_KV_W, 2 * NSA_KV_W),
                        ("k_slc", NSA_KV_W, NSA_KV_W), ("v_slc", NSA_KV_W, NSA_KV_W),
                        ("k_win", NSA_KV_W, NSA_KV_W), ("v_win", NSA_KV_W, NSA_KV_W),
                        ("g_nsa", NSA_KV_GROUPS * LANES, LANES), ("if_ml", LANES, LANES)):
    assert _off % _blk == 0
    _SEGS[_name] = (_off, _w)
    _off += _w
IN_TN = 512
IN_PAD = -(-_off // IN_TN) * IN_TN


def _pad_in_proj(w_in):
    sizes = (NSA_Q_W,) + (NSA_KV_W,) * 6 + (NSA_HEADS * 3, ML_QK_W, ML_QK_W, ML_V_W, ML_V_W,
                                            MLSTM_HEADS, MLSTM_HEADS, MEM_W, N_BRANCH * D_MODEL)
    splits = np.cumsum(sizes)[:-1].tolist()
    (q_nsa, k_cmp, v_cmp, k_slc, v_slc, k_win, v_win, g_nsa,
     q_ml, k_ml, v_ml, o_ml, i_ml, f_ml, q_mem, g_br) = jnp.split(w_in, splits, axis=-1)
    per_group = NSA_REP * 3
    g_blocks = [jnp.pad(g_nsa[:, g * per_group:(g + 1) * per_group], ((0, 0), (0, LANES - per_group)))
                for g in range(NSA_KV_GROUPS)]
    if_blk = jnp.pad(jnp.concatenate([i_ml, f_ml], -1), ((0, 0), (0, LANES - 2 * MLSTM_HEADS)))
    cols = [g_br, q_nsa, q_mem, v_ml, o_ml, q_ml, k_ml, k_cmp, v_cmp,
            k_slc, v_slc, k_win, v_win] + g_blocks + [if_blk]
    w = jnp.concatenate(cols, axis=-1)
    return jnp.pad(w, ((0, 0), (0, IN_PAD - w.shape[1])))


def _mm_kernel(a_ref, b_ref, o_ref):
    o_ref[...] = jnp.dot(a_ref[...].astype(BF16), b_ref[...].astype(BF16),
                         preferred_element_type=F32).astype(o_ref.dtype)


def _matmul(a, b, *, tm, tn, out_dtype=F32, name="matmul"):
    m, k = a.shape
    _, n = b.shape
    assert m % tm == 0 and n % tn == 0
    return pl.pallas_call(
        _mm_kernel,
        out_shape=jax.ShapeDtypeStruct((m, n), out_dtype),
        grid=(m // tm, n // tn),
        in_specs=[pl.BlockSpec((tm, k), lambda i, j: (i, 0)),
                  pl.BlockSpec((k, tn), lambda i, j: (0, j))],
        out_specs=pl.BlockSpec((tm, tn), lambda i, j: (i, j)),
        compiler_params=pltpu.CompilerParams(dimension_semantics=("parallel", "arbitrary"),
                                             vmem_limit_bytes=VMEM_LIMIT),
        name=name,
    )(a, b)


def _rope_tables(pos):
    half = HEAD_DIM // 2
    inv = ROPE_THETA ** (-jnp.arange(half, dtype=F32) / half)
    ang = pos.astype(F32)[:, None] * inv[None, :]
    cos, sin = jnp.cos(ang), jnp.sin(ang)
    return jnp.concatenate([cos, cos], -1), jnp.concatenate([-sin, sin], -1)


def _rope(x, cos, sin_signed):
    return x * cos + pltpu.roll(x, HEAD_DIM // 2, 1) * sin_signed


def _nsa_prep_kernel(kvc_ref, ks_ref, vs_ref, kw_ref, vw_ref, cos_ref, sin_ref,
                     kvc_o, ks_o, vs_o, kw_o, vw_o):
    cos, sin = cos_ref[...], sin_ref[...]
    kvc_o[0] = kvc_ref[0]
    for src, dst in ((ks_ref, ks_o), (kw_ref, kw_o)):
        for g in range(NSA_KV_GROUPS):
            sl = slice(g * HEAD_DIM, (g + 1) * HEAD_DIM)
            dst[0, :, sl] = _rope(src[0, :, sl], cos, sin).astype(BF16)
    vs_o[0] = vs_ref[0].astype(BF16)
    vw_o[0] = vw_ref[0].astype(BF16)


def _nsa_prep(y3, cos, sin, *, ts=512):
    b, s, _ = y3.shape
    kvw = NSA_KV_W

    def col(name, width):
        blk = _SEGS[name][0] // width
        return pl.BlockSpec((1, ts, width), lambda i, j: (i, j, blk))

    tab = pl.BlockSpec((ts, HEAD_DIM), lambda i, j: (j, 0))
    out_kv = pl.BlockSpec((1, ts, kvw), lambda i, j: (i, j, 0))
    return pl.pallas_call(
        _nsa_prep_kernel,
        out_shape=(jax.ShapeDtypeStruct((b, s, 2 * kvw), F32),) + (jax.ShapeDtypeStruct((b, s, kvw), BF16),) * 4,
        grid=(b, s // ts),
        in_specs=[col("kv_cmp", 2 * kvw), col("k_slc", kvw), col("v_slc", kvw), col("k_win", kvw),
                  col("v_win", kvw), tab, tab],
        out_specs=(pl.BlockSpec((1, ts, 2 * kvw), lambda i, j: (i, j, 0)), out_kv, out_kv, out_kv, out_kv),
        compiler_params=pltpu.CompilerParams(dimension_semantics=("parallel", "parallel")),
        name="nsa_prep",
    )(y3, y3, y3, y3, y3, cos, sin)


def _cmp_kernel(u_ref, posa_ref, posb_ref, w1a_ref, w1b_ref, w2_ref, cos_ref, sin_ref, kc_o, vc_o):
    n_rows = u_ref.shape[2]
    for idx, out in ((0, kc_o), (1, vc_o)):
        u = u_ref[0, idx]
        a = jnp.dot((u + posa_ref[idx]).astype(BF16), w1a_ref[idx], preferred_element_type=F32)
        bm = jnp.dot((u + posb_ref[idx]).astype(BF16), w1b_ref[idx], preferred_element_type=F32)
        pre = a + pltpu.roll(bm, n_rows - 1, 0)
        h = jax.nn.gelu(pre).astype(BF16)
        c = jnp.dot(h, w2_ref[idx], preferred_element_type=F32)
        if idx == 0:
            for g in range(NSA_KV_GROUPS):
                sl = slice(g * HEAD_DIM, (g + 1) * HEAD_DIM)
                out[0, :, sl] = _rope(c[:, sl], cos_ref[...], sin_ref[...]).astype(BF16)
        else:
            out[0] = c.astype(BF16)


def _block_diag_groups(w):
    z = jnp.zeros_like(w)
    rows = [jnp.concatenate([w if gi == go else z for go in range(NSA_KV_GROUPS)], -1)
            for gi in range(NSA_KV_GROUPS)]
    return jnp.concatenate(rows, -2)


def _compress(kvc, cmp_pos_k, cmp_k_w1, cmp_k_w2, cmp_pos_v, cmp_v_w1, cmp_v_w2, cos_e, sin_e):
    b, s, _ = kvc.shape
    n_rows = s // CMP_STRIDE
    g, hd = NSA_KV_GROUPS, HEAD_DIM
    row_w = CMP_STRIDE * g * hd
    u = kvc.reshape(b, n_rows, CMP_STRIDE, 2, g * hd).transpose(0, 3, 1, 2, 4).reshape(b, 2, n_rows, row_w)

    def pos_rows(pos):
        p = jnp.broadcast_to(pos.reshape(2, CMP_STRIDE, 1, hd), (2, CMP_STRIDE, g, hd))
        return p.reshape(2, 1, row_w)

    def w1_halves(w1):
        w = w1.reshape(2, CMP_STRIDE, hd, hd)
        return _block_diag_groups(w).reshape(2, row_w, g * hd).astype(BF16)

    pk, pv = pos_rows(cmp_pos_k), pos_rows(cmp_pos_v)
    wk, wv = w1_halves(cmp_k_w1), w1_halves(cmp_v_w1)
    posa = jnp.stack([pk[0], pv[0]])
    posb = jnp.stack([pk[1], pv[1]])
    w1a = jnp.stack([wk[0], wv[0]])
    w1b = jnp.stack([wk[1], wv[1]])
    w2 = jnp.stack([_block_diag_groups(cmp_k_w2), _block_diag_groups(cmp_v_w2)]).astype(BF16)
    full = lambda shp: pl.BlockSpec(shp, lambda i: (0,) * len(shp))
    out_spec = pl.BlockSpec((1, n_rows, g * hd), lambda i: (i, 0, 0))
    return pl.pallas_call(
        _cmp_kernel,
        out_shape=(jax.ShapeDtypeStruct((b, n_rows, g * hd), BF16),) * 2,
        grid=(b,),
        in_specs=[pl.BlockSpec((1, 2, n_rows, row_w), lambda i: (i, 0, 0, 0)),
                  full(posa.shape), full(posb.shape), full(w1a.shape), full(w1b.shape), full(w2.shape),
                  full(cos_e.shape), full(sin_e.shape)],
        out_specs=(out_spec, out_spec),
        compiler_params=pltpu.CompilerParams(dimension_semantics=("parallel",), vmem_limit_bytes=VMEM_LIMIT),
        name="nsa_compress",
    )(u, posa, posb, w1a, w1b, w2, cos_e, sin_e)


NSA_TQ = 256
NSA_TK = 256


def _flash(q4, k_ref, v_ref, kt_lo, kt_hi, bias_fn, acc_ref):
    rows = q4.shape[0]
    tq = rows // NSA_REP
    acc_ref[...] = jnp.zeros_like(acc_ref)

    def body(kt, carry):
        m, l = carry
        k0 = pl.multiple_of(kt * NSA_TK, NSA_TK)
        k = k_ref[0, pl.ds(k0, NSA_TK), :]
        v = v_ref[0, pl.ds(k0, NSA_TK), :]
        s = lax.dot_general(q4, k, (((1,), (1,)), ((), ())), preferred_element_type=F32)
        s = s.reshape(NSA_REP, tq, NSA_TK) + bias_fn(k0)[None]
        m_new = jnp.maximum(m, s.max(-1, keepdims=True))
        alpha = jnp.exp(m - m_new)
        p = jnp.exp(s - m_new)
        l = alpha * l + p.sum(-1, keepdims=True)
        pv = jnp.dot(p.reshape(rows, NSA_TK).astype(BF16), v, preferred_element_type=F32)
        acc_ref[...] = acc_ref[...] * alpha.reshape(rows, 1) + pv
        return m_new, l

    init = (jnp.full((NSA_REP, tq, 1), NEG_INF, F32), jnp.zeros((NSA_REP, tq, 1), F32))
    _, l = lax.fori_loop(kt_lo, kt_hi, body, init)
    return acc_ref[...] / l.reshape(rows, 1)


def _nsa_kernel(q_ref, gate_ref, kc_ref, vc_ref, ks_ref, vs_ref, kw_ref, vw_ref, cos_ref, sin_ref,
                cover_ref, o_ref, acc_ref, *, n_slc):
    qi = pl.program_id(2)
    tq = NSA_TQ
    rows = NSA_REP * tq
    q0 = qi * tq
    cos, sin = cos_ref[...], sin_ref[...]
    scale = HEAD_DIM ** -0.5
    q4 = jnp.concatenate(
        [(_rope(q_ref[0, :, r * HEAD_DIM:(r + 1) * HEAD_DIM], cos, sin) * scale).astype(BF16)
         for r in range(NSA_REP)], axis=0)
    tpos = q0 + lax.broadcasted_iota(jnp.int32, (tq, LANES), 0)
    lane = lax.broadcasted_iota(jnp.int32, (tq, LANES), 1)

    s = lax.dot_general(q4, kc_ref[0], (((1,), (1,)), ((), ())), preferred_element_type=F32)
    s = s.reshape(NSA_REP, tq, LANES)
    cmask = (lane * CMP_STRIDE + (CMP_LEN - 1) <= tpos)[None]
    s = jnp.where(cmask, s, NEG_INF)
    e = jnp.exp(s - s.max(-1, keepdims=True))
    p = e / e.sum(-1, keepdims=True) * cmask.astype(F32)
    o_cmp = jnp.dot(p.reshape(rows, LANES).astype(BF16), vc_ref[0], preferred_element_type=F32)

    p_sum_t = jnp.transpose(p.sum(0))
    p_hi = p_sum_t.astype(BF16)
    p_lo = (p_sum_t - p_hi.astype(F32)).astype(BF16)
    cover_t = cover_ref[...]
    p_slc = (jnp.dot(cover_t, p_hi, preferred_element_type=F32)
             + jnp.dot(cover_t, p_lo, preferred_element_type=F32))
    blk = lax.broadcasted_iota(jnp.int32, (n_slc, tq), 0)
    t_blk = (q0 + lax.broadcasted_iota(jnp.int32, (n_slc, tq), 1)) // SLC_LEN
    forced = (blk == 0) | (blk == t_blk) | (blk == t_blk - 1)
    score = jnp.where(forced, jnp.inf, jnp.where(blk > t_blk, -jnp.inf, p_slc))
    cnt = jnp.zeros((n_slc, tq), jnp.int32)
    for i in range(n_slc):
        other = score[i:i + 1, :]
        ahead = (other > score) | ((other == score) & (blk > i))
        cnt = cnt + ahead.astype(jnp.int32)
    chosen = (cnt < min(SLC_TOPK, n_slc)) & (score > -jnp.inf)
    pow2 = jnp.left_shift(1, blk & 15).astype(F32)
    halves = []
    for lo_blk in (0, 16):
        half = jnp.where(chosen & (blk >= lo_blk) & (blk < lo_blk + 16), pow2, 0.0).sum(0, keepdims=True)
        halves.append(jnp.transpose(jnp.broadcast_to(half, (LANES, tq)))[:, 0:1].astype(jnp.int32))
    bits = halves[0] | (halves[1] << 16)

    tpos_k = q0 + lax.broadcasted_iota(jnp.int32, (tq, NSA_TK), 0)
    kiota = lax.broadcasted_iota(jnp.int32, (tq, NSA_TK), 1)
    bits_b = jnp.broadcast_to(bits, (tq, NSA_TK))

    def slc_bias(k0):
        kpos = k0 + kiota
        picked = (lax.shift_right_logical(bits_b, kpos // SLC_LEN) & 1) == 1
        return jnp.where(picked & (kpos <= tpos_k), 0.0, NEG_INF)

    o_slc = _flash(q4, ks_ref, vs_ref, 0, (q0 + tq) // NSA_TK, slc_bias, acc_ref)

    def win_bias(k0):
        kpos = k0 + kiota
        return jnp.where((kpos <= tpos_k) & (kpos > tpos_k - WINDOW), 0.0, NEG_INF)

    kt_lo = jnp.maximum(q0 - WINDOW, 0) // NSA_TK
    o_win = _flash(q4, kw_ref, vw_ref, kt_lo, (q0 + tq) // NSA_TK, win_bias, acc_ref)

    gate = jax.nn.sigmoid(gate_ref[0])
    for r in range(NSA_REP):
        rs = slice(r * tq, (r + 1) * tq)
        o = (gate[:, 3 * r:3 * r + 1] * o_cmp[rs] + gate[:, 3 * r + 1:3 * r + 2] * o_slc[rs]
             + gate[:, 3 * r + 2:3 * r + 3] * o_win[rs])
        o_ref[0, :, r * HEAD_DIM:(r + 1) * HEAD_DIM] = o.astype(o_ref.dtype)


def _cover_matrix(n_rows, n_slc):
    i = np.arange(n_rows)[None, :]
    j = np.arange(n_slc)[:, None]
    c0 = i * CMP_STRIDE
    c1 = c0 + CMP_LEN - 1
    s0 = j * SLC_LEN
    s1 = s0 + SLC_LEN - 1
    ok = (c1 >= s0) & (c0 <= s1) & (i < n_rows - 1)
    return jnp.asarray(ok.astype(np.float32), dtype=BF16)


def _nsa_attention(y3, kc, vc, ks, vs, kw, vw, cos, sin):
    b, s, _ = y3.shape
    g, hd, tq = NSA_KV_GROUPS, HEAD_DIM, NSA_TQ
    n_rows = s // CMP_STRIDE
    assert n_rows == LANES and s // SLC_LEN <= 32 and s % tq == 0
    cover = _cover_matrix(n_rows, s // SLC_LEN)
    qw = NSA_REP * hd
    gate_blk = _SEGS["g_nsa"][0] // LANES
    q_blk = _SEGS["q_nsa"][0] // qw
    seq_g = pl.BlockSpec((1, s, hd), lambda i, j, t: (i, 0, j))
    cmp_g = pl.BlockSpec((1, n_rows, hd), lambda i, j, t: (i, 0, j))
    tab = pl.BlockSpec((tq, hd), lambda i, j, t: (t, 0))
    return pl.pallas_call(
        functools.partial(_nsa_kernel, n_slc=s // SLC_LEN),
        out_shape=jax.ShapeDtypeStruct((b, s, NSA_Q_W), BF16),
        grid=(b, g, s // tq),
        in_specs=[pl.BlockSpec((1, tq, qw), lambda i, j, t: (i, t, q_blk + j)),
                  pl.BlockSpec((1, tq, LANES), lambda i, j, t: (i, t, gate_blk + j)),
                  cmp_g, cmp_g, seq_g, seq_g, seq_g, seq_g, tab, tab,
                  pl.BlockSpec(cover.shape, lambda i, j, t: (0, 0))],
        out_specs=pl.BlockSpec((1, tq, qw), lambda i, j, t: (i, t, j)),
        scratch_shapes=[pltpu.VMEM((NSA_REP * tq, hd), F32)],
        compiler_params=pltpu.CompilerParams(dimension_semantics=("parallel", "parallel", "arbitrary"),
                                             vmem_limit_bytes=VMEM_LIMIT),
        name="nsa_attention",
    )(y3, y3, kc, vc, ks, vs, kw, vw, cos, sin, cover)


FFN_TM = 256
FFN_NCH = 8
_STAGE, _COMPUTE, _ZERO = 0, 1, 2


def _ffn_kernel(kind_ref, e_ref, c_ref, blk_ref, x_ref, w_ref, wg_ref, wu_ref, wd_ref, o_ref,
                wg_s, wu_s, wd_s):
    s = pl.program_id(0)
    kind = kind_ref[s]
    rg = wg_ref.shape[1]
    rd = wd_ref.shape[1]

    @pl.when(kind == _STAGE)
    def _():
        c = c_ref[s]
        r0 = pl.multiple_of(c * rg, rg)
        wg_s[pl.ds(r0, rg), :] = wg_ref[0].astype(BF16)
        wu_s[pl.ds(r0, rg), :] = wu_ref[0].astype(BF16)
        r1 = pl.multiple_of(c * rd, rd)
        wd_s[pl.ds(r1, rd), :] = wd_ref[0].astype(BF16)

    @pl.when(kind == _COMPUTE)
    def _():
        x = x_ref[...]
        gate = jnp.dot(x, wg_s[...], preferred_element_type=F32)
        up = jnp.dot(x, wu_s[...], preferred_element_type=F32)
        h = (jax.nn.silu(gate) * up).astype(BF16)
        o_ref[...] = jnp.dot(h, wd_s[...], preferred_element_type=F32) * w_ref[...]

    @pl.when(kind == _ZERO)
    def _():
        o_ref[...] = jnp.zeros_like(o_ref)


def _ffn_items(nb_e, n_blocks):
    e = nb_e.shape[0]
    blk_start = jnp.cumsum(nb_e) - nb_e
    n_used = blk_start[-1] + nb_e[-1]
    base = FFN_NCH * jnp.arange(e) + blk_start
    s = jnp.arange(FFN_NCH * e + n_blocks)
    e_s = (s[:, None] >= base[None, :]).sum(1) - 1
    r = s - base[e_s]
    end = FFN_NCH * e + n_used
    kind = jnp.where(s >= end, _ZERO, jnp.where(r < FFN_NCH, _STAGE, _COMPUTE))
    c_s = jnp.minimum(r, FFN_NCH - 1)
    blk = jnp.where(kind == _ZERO, n_used + s - end,
                    jnp.where(kind == _STAGE, blk_start[e_s], blk_start[e_s] + r - FFN_NCH))
    blk = jnp.minimum(blk, n_blocks - 1)
    i32 = lambda a: a.astype(jnp.int32)
    return i32(kind), i32(e_s), i32(c_s), i32(blk)


def _grouped_ffn(xs, row_w, nb_e, wg, wu, wd):
    p, d = xs.shape
    e, _, f = wg.shape
    n_blocks = p // FFN_TM
    assert d % FFN_NCH == 0 and f % (FFN_NCH * 16) == 0
    rg, rd = d // FFN_NCH, f // FFN_NCH
    kind, e_s, c_s, blk = _ffn_items(nb_e, n_blocks)
    row = lambda w: pl.BlockSpec((FFN_TM, w), lambda s, k, es, cs, bs: (bs[s], 0))
    chunk = lambda r, w: pl.BlockSpec((1, r, w), lambda s, k, es, cs, bs: (es[s], cs[s], 0))
    grid_spec = pltpu.PrefetchScalarGridSpec(
        num_scalar_prefetch=4,
        grid=(FFN_NCH * e + n_blocks,),
        in_specs=[row(d), row(1), chunk(rg, f), chunk(rg, f), chunk(rd, d)],
        out_specs=row(d),
        scratch_shapes=[pltpu.VMEM((d, f), BF16), pltpu.VMEM((d, f), BF16), pltpu.VMEM((f, d), BF16)],
    )
    return pl.pallas_call(
        _ffn_kernel,
        out_shape=jax.ShapeDtypeStruct((p, d), F32),
        grid_spec=grid_spec,
        compiler_params=pltpu.CompilerParams(dimension_semantics=("arbitrary",), vmem_limit_bytes=VMEM_LIMIT),
        name="grouped_ffn",
    )(kind, e_s, c_s, blk, xs, row_w, wg, wu, wd)


FFN_IDX_SLOTS = 4
FFN_ROW_PITCH = 24


def _routed_ffn_kernel(kind_ref, e_ref, c_ref, blk_ref, nblk_ref, x_hbm, tab_hbm, w_ref, wg_ref, wu_ref,
                       wd_ref, y_hbm, wg_s, wu_s, wd_s, xbuf0, xbuf1, obuf0, obuf1, zbuf, idx_s,
                       isem, gsem, ssem, zsem):
    s = pl.program_id(0)
    kind = kind_ref[s]
    rg = wg_ref.shape[1]
    rd = wd_ref.shape[1]
    tm = FFN_TM
    last_blk = tab_hbm.shape[0] - 1
    ns = wd_ref.shape[2] // LANES
    xbufs, obufs = (xbuf0, xbuf1), (obuf0, obuf1)

    def idx_copy(b):
        return pltpu.make_async_copy(tab_hbm.at[jnp.minimum(b, last_blk)], idx_s.at[b % FFN_IDX_SLOTS],
                                     isem.at[b % FFN_IDX_SLOTS])

    def gather_copy(b, par, r):
        tok = idx_s[b % FFN_IDX_SLOTS, r]
        return pltpu.make_async_copy(x_hbm.at[pl.ds(pl.multiple_of(tok * ns, ns), ns), :],
                                     xbufs[par].at[pl.ds(r * FFN_ROW_PITCH, ns), :], gsem.at[par])

    def scatter_copy(b, par, r):
        dst = idx_s[b % FFN_IDX_SLOTS, tm + r]
        return pltpu.make_async_copy(obufs[par].at[pl.ds(r * FFN_ROW_PITCH, ns), :],
                                     y_hbm.at[pl.ds(pl.multiple_of(dst * ns, ns), ns), :], ssem.at[par])

    @pl.when(s == 0)
    def _():
        zbuf[...] = jnp.zeros_like(zbuf)
        idx_copy(0).start()
        idx_copy(1).start()
        idx_copy(0).wait()
        for r in range(tm):
            gather_copy(0, 0, r).start()

    @pl.when(kind == _STAGE)
    def _():
        c = c_ref[s]
        r0 = pl.multiple_of(c * rg, rg)
        wg_s[pl.ds(r0, rg), :] = wg_ref[0].astype(BF16)
        wu_s[pl.ds(r0, rg), :] = wu_ref[0].astype(BF16)
        r1 = pl.multiple_of(c * rd, rd)
        wd_s[pl.ds(r1, rd), :] = wd_ref[0].astype(BF16)

    def compute(b, par, first):
        idx_copy(b + 1).wait()
        for r in range(tm):
            gather_copy(b, par, r).wait()
        if not first:
            @pl.when(b >= 2)
            def _():
                for r in range(tm):
                    scatter_copy(b - 2, par, r).wait()

            for r in range(tm):
                scatter_copy(b - 1, 1 - par, r).start()
        for r in range(tm):
            gather_copy(b + 1, 1 - par, r).start()
        idx_copy(b + 2).start()
        x = jnp.concatenate([xbufs[par][pl.ds(j, tm, stride=FFN_ROW_PITCH), :] for j in range(ns)],
                            axis=1).astype(BF16)
        gate = jnp.dot(x, wg_s[...], preferred_element_type=F32)

        @pl.when(b >= 0)
        def _():
            up = jnp.dot(x, wu_s[...], preferred_element_type=F32)
            h = (jax.nn.silu(gate) * up).astype(BF16)
            out = jnp.dot(h, wd_s[...], preferred_element_type=F32) * w_ref[...]
            for j in range(ns):
                obufs[par][pl.ds(j, tm, stride=FFN_ROW_PITCH), :] = out[:, j * LANES:(j + 1) * LANES]

    is_compute = kind == _COMPUTE
    blk = blk_ref[s]
    pl.when(is_compute & (blk == 0))(lambda: compute(blk, 0, True))
    pl.when(is_compute & (blk > 0) & (blk % 2 == 0))(lambda: compute(blk, 0, False))
    pl.when(is_compute & (blk % 2 == 1))(lambda: compute(blk, 1, False))

    @pl.when(kind == _ZERO)
    def _():
        row0 = pl.multiple_of(blk * (tm * ns), tm * ns)
        tail = pltpu.make_async_copy(zbuf, y_hbm.at[pl.ds(row0, tm * ns), :], zsem.at[0])
        tail.start()
        tail.wait()

    def drain(n_used, par):
        idx_copy(n_used + 1).wait()
        for r in range(tm):
            gather_copy(n_used, par, r).wait()
        for r in range(tm):
            scatter_copy(n_used - 1, 1 - par, r).start()
        for r in range(tm):
            scatter_copy(n_used - 2, par, r).wait()
        for r in range(tm):
            scatter_copy(n_used - 1, 1 - par, r).wait()

    is_last = s == pl.num_programs(0) - 1
    n_used = nblk_ref[0]
    pl.when(is_last & (n_used % 2 == 0))(lambda: drain(n_used, 0))
    pl.when(is_last & (n_used % 2 == 1))(lambda: drain(n_used, 1))


def _routed_ffn(x, tab, row_w, nb_e, wg, wu, wd):
    e, d, f = wg.shape
    n_blocks = tab.shape[0]
    ns = d // LANES
    assert ns % 8 == 0 and ns <= FFN_ROW_PITCH and x.shape[1] == LANES
    row_buf = pltpu.VMEM((FFN_TM * FFN_ROW_PITCH, LANES), F32)
    zero_buf = pltpu.VMEM((FFN_TM * ns, LANES), F32)
    assert d % FFN_NCH == 0 and f % (FFN_NCH * 16) == 0 and tab.shape[1] == 2 * FFN_TM
    rg, rd = d // FFN_NCH, f // FFN_NCH
    kind, e_s, c_s, blk = _ffn_items(nb_e, n_blocks)
    n_used = jnp.sum(nb_e).astype(jnp.int32).reshape(1)
    chunk = lambda r, w: pl.BlockSpec((1, r, w), lambda s, k, es, cs, bs, nu: (es[s], cs[s], 0))
    grid_spec = pltpu.PrefetchScalarGridSpec(
        num_scalar_prefetch=5,
        grid=(FFN_NCH * e + n_blocks,),
        in_specs=[pl.BlockSpec(memory_space=pl.ANY), pl.BlockSpec(memory_space=pl.ANY),
                  pl.BlockSpec((FFN_TM, 1), lambda s, k, es, cs, bs, nu: (bs[s], 0)),
                  chunk(rg, f), chunk(rg, f), chunk(rd, d)],
        out_specs=pl.BlockSpec(memory_space=pl.ANY),
        scratch_shapes=[pltpu.VMEM((d, f), BF16), pltpu.VMEM((d, f), BF16), pltpu.VMEM((f, d), BF16),
                        row_buf, row_buf, row_buf, row_buf, zero_buf,
                        pltpu.SMEM((FFN_IDX_SLOTS, 2 * FFN_TM), jnp.int32),
                        pltpu.SemaphoreType.DMA((FFN_IDX_SLOTS,)), pltpu.SemaphoreType.DMA((2,)),
                        pltpu.SemaphoreType.DMA((2,)), pltpu.SemaphoreType.DMA((1,))],
    )
    return pl.pallas_call(
        _routed_ffn_kernel,
        out_shape=jax.ShapeDtypeStruct((n_blocks * FFN_TM * ns, LANES), F32),
        grid_spec=grid_spec,
        compiler_params=pltpu.CompilerParams(dimension_semantics=("arbitrary",), vmem_limit_bytes=VMEM_LIMIT),
        name="routed_ffn",
    )(kind, e_s, c_s, blk, n_used, x, tab, row_w, wg, wu, wd)


ML_L = 256
ML_HALO = 8
MLSTM_CONV = 4


def _mlstm_kernel(qk_ref, v_ref, o_ref, if_ref, convw_ref, convb_ref, ifb_ref, ng_ref, tril_ref,
                  out_ref, c_ref, m_ref, halo_ref):
    L, NH, DK, DV = ML_L, MLSTM_HEADS, MLSTM_QK_DIM, MLSTM_V_DIM

    @pl.when(pl.program_id(1) == 0)
    def _():
        c_ref[...] = jnp.zeros_like(c_ref)
        m_ref[...] = jnp.zeros_like(m_ref)
        halo_ref[...] = jnp.zeros_like(halo_ref)

    x = qk_ref[0]
    cat = jnp.concatenate([halo_ref[...], x], axis=0)
    halo_ref[...] = x[L - ML_HALO:, :]
    y = convb_ref[...]
    for j in range(MLSTM_CONV):
        lo = ML_HALO - (MLSTM_CONV - 1) + j
        y = y + convw_ref[j:j + 1, :] * cat[lo:lo + L, :]
    qk = y * jax.nn.sigmoid(y)

    lane = lax.broadcasted_iota(jnp.int32, (L, LANES), 1)
    z = if_ref[0] + ifb_ref[...]
    lg = jnp.where(lane < NH, z, jnp.minimum(z, 0.0) - jnp.log1p(jnp.exp(-jnp.abs(z))))
    hi = lg.astype(BF16)
    r1 = lg - hi.astype(F32)
    mid = r1.astype(BF16)
    lo3 = (r1 - mid.astype(F32)).astype(BF16)
    tril = tril_ref[...]
    bcum = (jnp.dot(tril, hi, preferred_element_type=F32) + jnp.dot(tril, mid, preferred_element_type=F32)
            + jnp.dot(tril, lo3, preferred_element_type=F32))

    row = lax.broadcasted_iota(jnp.int32, (L, L), 0)
    colm = lax.broadcasted_iota(jnp.int32, (L, L), 1)
    tri = row >= colm
    ones_col = (lane == 0).astype(BF16)
    for h in range(NH):
        li = lg[:, h:h + 1]
        b = bcum[:, NH + h:NH + h + 1]
        m_prev = m_ref[h:h + 1, 0:1]
        a_col = li - b
        a_row = jnp.transpose(jnp.broadcast_to(a_col, (L, LANES)))[0:1, :]
        dmat = jnp.where(tri, b + a_row, -jnp.inf)
        m_inter = b + m_prev
        m_t = jnp.maximum(m_inter, dmat.max(-1, keepdims=True))
        q = qk[:, h * DK:(h + 1) * DK].astype(BF16)
        k = qk[:, NH * DK + h * DK:NH * DK + (h + 1) * DK] * DK ** -0.5
        s = lax.dot_general(q, k.astype(BF16), (((1,), (1,)), ((), ())), preferred_element_type=F32)
        wqk = (jnp.exp(dmat - m_t) * s).astype(BF16)
        inter = jnp.exp(m_inter - m_t)
        v_ext = jnp.concatenate([v_ref[0, :, h * DV:(h + 1) * DV].astype(BF16), ones_col], axis=1)
        c_old = c_ref[h]
        nd = (jnp.dot(wqk, v_ext, preferred_element_type=F32)
              + inter * jnp.dot(q, c_old.astype(BF16), preferred_element_type=F32))
        num = nd[:, :DV]
        den = nd[:, DV:DV + 1]
        hh = num / jnp.maximum(jnp.abs(den), jnp.exp(-m_t))
        b_last = b[L - 1:L, :]
        g = b_last - b + li
        m_new = jnp.maximum(b_last + m_prev, g.max(0, keepdims=True))
        decay = jnp.exp(b_last + m_prev - m_new)
        w_s = jnp.exp(g - m_new)
        kw_t = jnp.transpose(k * w_s).astype(BF16)
        c_ref[h] = decay * c_old + jnp.dot(kw_t, v_ext, preferred_element_type=F32)
        m_ref[h:h + 1, :] = jnp.broadcast_to(m_new, (1, LANES))
        mu = hh.mean(-1, keepdims=True)
        var = jnp.mean(jnp.square(hh - mu), -1, keepdims=True)
        hn = (hh - mu) * lax.rsqrt(var + LN_EPS) * ng_ref[:, h * DV:(h + 1) * DV]
        out_ref[0, :, h * DV:(h + 1) * DV] = (
            jax.nn.sigmoid(o_ref[0, :, h * DV:(h + 1) * DV]) * hn).astype(out_ref.dtype)


def _mlstm(y3, conv_w, conv_b, if_bias, norm_g):
    b, s, _ = y3.shape
    L, NH = ML_L, MLSTM_HEADS
    assert s % L == 0

    def col(name, width):
        blk = _SEGS[name][0] // width
        return pl.BlockSpec((1, L, width), lambda i, j: (i, j, blk))

    full = lambda shp: pl.BlockSpec(shp, lambda i, j: (0,) * len(shp))
    ifb = jnp.pad(if_bias.reshape(1, 2 * NH), ((0, 0), (0, LANES - 2 * NH)))
    tril = jnp.asarray(np.tril(np.ones((L, L), np.float32)), dtype=BF16)
    conv_b = conv_b.reshape(1, -1)
    norm_g = norm_g.reshape(1, -1)
    return pl.pallas_call(
        _mlstm_kernel,
        out_shape=jax.ShapeDtypeStruct((b, s, ML_V_W), BF16),
        grid=(b, s // L),
        in_specs=[col("qk_ml", 2 * ML_QK_W), col("v_ml", ML_V_W), col("o_ml", ML_V_W), col("if_ml", LANES),
                  full(conv_w.shape), full(conv_b.shape), full(ifb.shape), full(norm_g.shape), full(tril.shape)],
        out_specs=pl.BlockSpec((1, L, ML_V_W), lambda i, j: (i, j, 0)),
        scratch_shapes=[pltpu.VMEM((NH, MLSTM_QK_DIM, MLSTM_V_DIM + LANES), F32),
                        pltpu.VMEM((8, LANES), F32),
                        pltpu.VMEM((ML_HALO, 2 * ML_QK_W), F32)],
        compiler_params=pltpu.CompilerParams(dimension_semantics=("parallel", "arbitrary"),
                                             vmem_limit_bytes=VMEM_LIMIT),
        name="mlstm",
    )(y3, y3, y3, y3, conv_w, conv_b, ifb, norm_g, tril)


MEM_TQ = 512


def _mem_attn_kernel(q_ref, kv_ref, o_ref):
    hd = MEM_HEAD_DIM
    for h in range(MEM_HEADS):
        q = q_ref[0, :, h * hd:(h + 1) * hd].astype(BF16)
        k = kv_ref[0, :, h * hd:(h + 1) * hd]
        v = kv_ref[0, :, MEM_W + h * hd:MEM_W + (h + 1) * hd]
        s = lax.dot_general(q, k, (((1,), (1,)), ((), ())), preferred_element_type=F32) * hd ** -0.5
        e = jnp.exp(s - s.max(-1, keepdims=True))
        p = (e / e.sum(-1, keepdims=True)).astype(BF16)
        o_ref[0, :, h * hd:(h + 1) * hd] = jnp.dot(p, v, preferred_element_type=F32).astype(o_ref.dtype)


def _mem_attention(y3, kv_mem):
    b, s, _ = y3.shape
    m = kv_mem.shape[1]
    q_blk = _SEGS["q_mem"][0] // MEM_W
    return pl.pallas_call(
        _mem_attn_kernel,
        out_shape=jax.ShapeDtypeStruct((b, s, MEM_W), BF16),
        grid=(b, s // MEM_TQ),
        in_specs=[pl.BlockSpec((1, MEM_TQ, MEM_W), lambda i, j: (i, j, q_blk)),
                  pl.BlockSpec((1, m, 2 * MEM_W), lambda i, j: (i, 0, 0))],
        out_specs=pl.BlockSpec((1, MEM_TQ, MEM_W), lambda i, j: (i, j, 0)),
        compiler_params=pltpu.CompilerParams(dimension_semantics=("parallel", "parallel"),
                                             vmem_limit_bytes=VMEM_LIMIT),
        name="mem_attention",
    )(y3, kv_mem)


MERGE_TM = 256


def _merge_kernel(yn_ref, ym_ref, yc_ref, g0_ref, g1_ref, g2_ref, x_ref, wb_ref, wo_ref, lg_ref, lb_ref,
                  x1_ref, x1b_ref, x1s_ref):
    merged = None
    for i, (y_ref, g_ref) in enumerate(((yn_ref, g0_ref), (ym_ref, g1_ref), (yc_ref, g2_ref))):
        t = jax.nn.sigmoid(g_ref[...]) * jnp.dot(y_ref[...], wb_ref[i], preferred_element_type=F32)
        merged = t if merged is None else merged + t
    z = DN_ALPHA * x_ref[...] + jnp.dot(merged.astype(BF16), wo_ref[...], preferred_element_type=F32)
    mu = z.mean(-1, keepdims=True)
    var = jnp.mean(jnp.square(z - mu), -1, keepdims=True)
    x1 = (z - mu) * lax.rsqrt(var + LN_EPS) * lg_ref[...] + lb_ref[...]
    x1_ref[...] = x1
    x1b_ref[...] = x1.astype(BF16)
    tm, d = x1.shape
    ns = d // LANES
    for j in range(ns):
        x1s_ref[pl.ds(j, tm, stride=ns), :] = x1[:, j * LANES:(j + 1) * LANES]


def _merge_ln1(y_nsa, y_ml, y_mem, y2, xt, wb, wo, ln_g, ln_b):
    t, d = xt.shape
    tm = MERGE_TM
    bw = y_nsa.shape[1]
    resident = pl.Buffered(1)
    row = lambda w: pl.BlockSpec((tm, w), lambda i: (i, 0))
    gate = lambda k: pl.BlockSpec((tm, d), lambda i: (i, k))
    return pl.pallas_call(
        _merge_kernel,
        out_shape=(jax.ShapeDtypeStruct((t, d), F32), jax.ShapeDtypeStruct((t, d), BF16),
                   jax.ShapeDtypeStruct((t * (d // LANES), LANES), F32)),
        grid=(t // tm,),
        in_specs=[row(bw), row(bw), row(bw), gate(0), gate(1), gate(2), row(d),
                  pl.BlockSpec((N_BRANCH, bw, d), lambda i: (0, 0, 0), pipeline_mode=resident),
                  pl.BlockSpec((d, d), lambda i: (0, 0), pipeline_mode=resident),
                  pl.BlockSpec((1, d), lambda i: (0, 0)), pl.BlockSpec((1, d), lambda i: (0, 0))],
        out_specs=(row(d), row(d), pl.BlockSpec((tm * (d // LANES), LANES), lambda i: (i, 0))),
        compiler_params=pltpu.CompilerParams(dimension_semantics=("parallel",), vmem_limit_bytes=VMEM_LIMIT),
        name="merge_ln1",
    )(y_nsa, y_ml, y_mem, y2, y2, y2, xt, wb, wo, ln_g.reshape(1, d), ln_b.reshape(1, d))


ROUTER_TM = 256


def _router_kernel(x_ref, w_ref, bias_ref, idx_ref, wt_ref, cnt_ref, run_ref):
    tm = ROUTER_TM

    @pl.when(pl.program_id(0) == 0)
    def _():
        run_ref[...] = jnp.zeros_like(run_ref)

    x = x_ref[...]
    w = w_ref[...]
    x_hi = x.astype(BF16)
    x_lo = (x - x_hi.astype(F32)).astype(BF16)
    w_hi = w.astype(BF16)
    w_lo = (w - w_hi.astype(F32)).astype(BF16)
    logits = (jnp.dot(x_hi, w_hi, preferred_element_type=F32) + jnp.dot(x_lo, w_hi, preferred_element_type=F32)
              + jnp.dot(x_hi, w_lo, preferred_element_type=F32))
    lane = lax.broadcasted_iota(jnp.int32, (tm, LANES), 1)
    aff = jax.nn.sigmoid(logits)
    score = jnp.where(lane < N_EXPERTS, aff + bias_ref[...], -jnp.inf)
    idx_out = jnp.zeros((tm, LANES), jnp.int32)
    aff_out = jnp.zeros((tm, LANES), F32)
    chosen = jnp.zeros((tm, LANES), jnp.bool_)
    for k in range(TOP_K):
        best = score.max(-1, keepdims=True)
        cur = jnp.where(score == best, lane, LANES).min(-1, keepdims=True)
        here = lane == cur
        idx_out = jnp.where(lane == k, cur, idx_out)
        aff_out = jnp.where(lane == k, jnp.where(here, aff, 0.0).sum(-1, keepdims=True), aff_out)
        chosen = chosen | here
        score = jnp.where(here, -jnp.inf, score)
    run_new = run_ref[0:1, :] + chosen.astype(F32).sum(0, keepdims=True)
    run_ref[...] = jnp.broadcast_to(run_new, run_ref.shape)
    cnt_ref[...] = jnp.broadcast_to(run_new, cnt_ref.shape)
    idx_ref[...] = idx_out
    wt_ref[...] = aff_out / aff_out.sum(-1, keepdims=True) * ROUTED_SCALE


def _router(x1, router_w, router_bias):
    t, d = x1.shape
    tm = ROUTER_TM
    e = router_w.shape[1]
    w = jnp.pad(router_w, ((0, 0), (0, LANES - e)))
    bias = jnp.pad(router_bias.reshape(1, e), ((0, 0), (0, LANES - e)))
    out = pl.BlockSpec((tm, LANES), lambda i: (i, 0))
    const = lambda shp: pl.BlockSpec(shp, lambda i: (0, 0))
    idx, wt, cnt = pl.pallas_call(
        _router_kernel,
        out_shape=(jax.ShapeDtypeStruct((t, LANES), jnp.int32), jax.ShapeDtypeStruct((t, LANES), F32),
                   jax.ShapeDtypeStruct((8, LANES), F32)),
        grid=(t // tm,),
        in_specs=[pl.BlockSpec((tm, d), lambda i: (i, 0)), const(w.shape), const(bias.shape)],
        out_specs=(out, out, const((8, LANES))),
        scratch_shapes=[pltpu.VMEM((8, LANES), F32)],
        compiler_params=pltpu.CompilerParams(dimension_semantics=("arbitrary",), vmem_limit_bytes=VMEM_LIMIT),
        name="router",
    )(x1, w, bias)
    return idx[:, :TOP_K], wt[:, :TOP_K], cnt[0, :e].astype(jnp.int32)


def _moe_layout(idx, wts, counts):
    t = idx.shape[0]
    e = N_EXPERTS
    m = t * TOP_K
    assert m // FFN_TM >= 2
    nb_e = (counts + FFN_TM - 1) // FFN_TM
    blk_start = jnp.cumsum(nb_e) - nb_e
    start = jnp.cumsum(counts) - counts
    n_blocks = m // FFN_TM + e
    order = jnp.argsort(idx.reshape(m))
    bi = jnp.arange(n_blocks)
    blk_e = jnp.clip((bi[:, None] >= blk_start[None, :]).sum(1) - 1, 0, e - 1)
    j = ((bi - blk_start[blk_e]) * FFN_TM)[:, None] + jnp.arange(FFN_TM)[None, :]
    valid = j < counts[blk_e][:, None]
    flat = order[jnp.clip(start[blk_e][:, None] + j, 0, m - 1)]
    tok, slot = flat // TOP_K, flat % TOP_K
    pad_before = (blk_start * FFN_TM - start)[blk_e] - counts[blk_e]
    row_tok = jnp.where(valid, tok, 0)
    dest = jnp.where(valid, slot * t + tok, m + pad_before[:, None] + j)
    row_w = jnp.where(valid, wts.reshape(m)[flat], 0.0).reshape(n_blocks * FFN_TM, 1)
    tab = jnp.concatenate([row_tok, dest], axis=1).astype(jnp.int32)
    return tab, row_w, nb_e


def _combine_kernel(x_ref, *refs):
    slot_refs, (s_ref, g_ref, b_ref, o_ref) = refs[:TOP_K], refs[TOP_K:]
    tm, d = x_ref.shape
    ns = d // LANES

    def rows(r_ref):
        return jnp.concatenate([r_ref[pl.ds(j, tm, stride=ns), :] for j in range(ns)], axis=1)

    routed = rows(slot_refs[0])
    for r_ref in slot_refs[1:]:
        routed = routed + rows(r_ref)
    z = DN_ALPHA * x_ref[...] + (routed + s_ref[...])
    mu = z.mean(-1, keepdims=True)
    var = jnp.mean(jnp.square(z - mu), -1, keepdims=True)
    o_ref[...] = (z - mu) * lax.rsqrt(var + LN_EPS) * g_ref[...] + b_ref[...]


def _combine_ln2(x1, y_slots, shared, ln_g, ln_b, *, tm=256):
    t, d = x1.shape
    nt = t // tm
    ns = d // LANES
    row = pl.BlockSpec((tm, d), lambda i: (i, 0))
    vec = pl.BlockSpec((1, d), lambda i: (0, 0))
    slots = [pl.BlockSpec((tm * ns, LANES), functools.partial(lambda i, k: (k * nt + i, 0), k=k))
             for k in range(TOP_K)]
    return pl.pallas_call(
        _combine_kernel, out_shape=jax.ShapeDtypeStruct((t, d), F32), grid=(nt,),
        in_specs=[row] + slots + [row, vec, vec], out_specs=row,
        compiler_params=pltpu.CompilerParams(dimension_semantics=("parallel",), vmem_limit_bytes=VMEM_LIMIT),
        name="combine_ln2",
    )(x1, *([y_slots] * TOP_K), shared, ln_g.reshape(1, d), ln_b.reshape(1, d))


def _layer(x, mem, w_in, mlstm_conv_w, mlstm_conv_b, mlstm_if_bias, mlstm_norm_g,
           cmp_pos_k, cmp_k_w1, cmp_k_w2, cmp_pos_v, cmp_v_w1, cmp_v_w2,
           w_mem_kv, w_branch, w_o, ln1_g, ln1_b, router_w, router_bias,
           w_e_gate, w_e_up, w_e_down, w_s_gate, w_s_up, w_s_down, ln2_g, ln2_b):
    B, S, D = x.shape
    T = B * S
    xt = x.reshape(T, D)
    y2 = _matmul(xt.astype(BF16), _pad_in_proj(w_in).astype(BF16), tm=1024, tn=IN_TN, name="in_proj")
    y3 = y2.reshape(B, S, IN_PAD)

    cos, sin = _rope_tables(jnp.arange(S))
    cos_e, sin_e = _rope_tables(jnp.arange(S // CMP_STRIDE) * CMP_STRIDE + CMP_LEN - 1)
    kvc, ks, vs, kw, vw = _nsa_prep(y3, cos, sin)
    kc, vc = _compress(kvc, cmp_pos_k, cmp_k_w1, cmp_k_w2, cmp_pos_v, cmp_v_w1, cmp_v_w2, cos_e, sin_e)
    y_nsa = _nsa_attention(y3, kc, vc, ks, vs, kw, vw, cos, sin)

    y_ml = _mlstm(y3, mlstm_conv_w, mlstm_conv_b, mlstm_if_bias, mlstm_norm_g)

    m_len = mem.shape[1]
    kv_mem = _matmul(mem.reshape(B * m_len, D).astype(BF16), w_mem_kv.astype(BF16), tm=1024, tn=512,
                     out_dtype=BF16, name="mem_kv").reshape(B, m_len, 2 * MEM_W)
    y_mem = _mem_attention(y3, kv_mem)

    x1, x1b, x1s = _merge_ln1(y_nsa.reshape(T, -1), y_ml.reshape(T, -1), y_mem.reshape(T, -1), y2, xt,
                         w_branch.astype(BF16), w_o.astype(BF16), ln1_g, ln1_b)

    idx, wts, counts = _router(x1, router_w, router_bias)
    tab, row_w, nb_e = _moe_layout(idx, wts, counts)
    y_slots = _routed_ffn(x1s, tab, row_w, nb_e, w_e_gate, w_e_up, w_e_down)
    shared = _grouped_ffn(x1b, jnp.ones((T, 1), F32), jnp.full((1,), T // FFN_TM, jnp.int32),
                          w_s_gate[None], w_s_up[None], w_s_down[None])
    return _combine_ln2(x1, y_slots, shared, ln2_g, ln2_b).reshape(B, S, D)


def kernel(x, mem, w_in, mlstm_conv_w, mlstm_conv_b, mlstm_if_bias, mlstm_norm_g, cmp_pos_k, cmp_k_w1, cmp_k_w2, cmp_pos_v, cmp_v_w1, cmp_v_w2, w_mem_kv, w_branch, w_o, ln1_g, ln1_b, router_w, router_bias, w_e_gate, w_e_up, w_e_down, w_s_gate, w_s_up, w_s_down, ln2_g, ln2_b):
    for layer in range(DEPTH):
        x = _layer(x, mem, w_in[layer], mlstm_conv_w[layer], mlstm_conv_b[layer],
                   mlstm_if_bias[layer], mlstm_norm_g[layer],
                   cmp_pos_k[layer], cmp_k_w1[layer], cmp_k_w2[layer],
                   cmp_pos_v[layer], cmp_v_w1[layer], cmp_v_w2[layer],
                   w_mem_kv[layer], w_branch[layer], w_o[layer], ln1_g[layer], ln1_b[layer],
                   router_w[layer], router_bias[layer], w_e_gate[layer], w_e_up[layer],
                   w_e_down[layer], w_s_gate[layer], w_s_up[layer], w_s_down[layer],
                   ln2_g[layer], ln2_b[layer])
    return x
```
